```python
import math
import jax, jax.numpy as jnp
from jax import lax
import numpy as np

D_MODEL = 1024
BATCH = 8
SEQ = 2048
DEPTH = 2
DEC_BATCH = 32
DEC_SEQ = 8
PAST_LEN = 8192
PAGE_SIZE = 128

HEAD_DIM = 64
SB_HEADS = D_MODEL // 128
SB_WIDTH = SB_HEADS * HEAD_DIM
GLA_HEADS = 4
GLA_DV = D_MODEL // (2 * GLA_HEADS)
GLA_DK = GLA_DV // 2
GLA_GATE_RANK = 16
GLA_TAU = 16.0
GLA_CHUNK = 64
EVEN_IN = 3 * SB_WIDTH + 2 * GLA_HEADS * GLA_DK + 2 * GLA_HEADS * GLA_DV + GLA_GATE_RANK
EVEN_MIX = SB_WIDTH + GLA_HEADS * GLA_DV
NSA_HEADS = D_MODEL // HEAD_DIM
NSA_KV_HEADS = 4
NSA_GROUP = NSA_HEADS // NSA_KV_HEADS
CMP_BLOCK = 32
CMP_STRIDE = 16
CMP_HIDDEN = 2 * HEAD_DIM
SLC_BLOCK = 64
SLC_TOPN = 16
WINDOW = 512
ODD_IN = NSA_HEADS * HEAD_DIM + 6 * NSA_KV_HEADS * HEAD_DIM + 3 * NSA_HEADS
ODD_MIX = NSA_HEADS * HEAD_DIM
REL_BUCKETS = 32
REL_MAX_DIST = 1024
D_FF = 256 * ((8 * D_MODEL // 3 + 255) // 256)
Q_BLOCK = 128
N_EVEN = (DEPTH + 1) // 2
N_ODD = DEPTH // 2
EPS = 1e-6
NEG = -1e30
FORCE = 1e6

kernel_name = "hybrid_stickbreak_gla_nsa_macaron_step"


def _offsets(sizes):
    out, acc = [], 0
    for s in sizes[:-1]:
        acc += s
        out.append(acc)
    return out


def rms(x, g):
    xf = x.astype(jnp.float32)
    y = xf * lax.rsqrt(jnp.mean(xf * xf, axis=-1, keepdims=True) + EPS)
    return (y * g.astype(jnp.float32)).astype(x.dtype)


def modulate(x, g, shift, scale):
    return rms(x, g) * (1 + scale) + shift


def swiglu(h, w_in, w_out):
    a, b = jnp.split(h @ w_in, 2, axis=-1)
    return (jax.nn.silu(a) * b) @ w_out


def masked_softmax(s, mask):
    s = jnp.where(mask, s, NEG)
    m = jnp.max(s, axis=-1, keepdims=True)
    p = jnp.where(mask, jnp.exp(s - m), 0.0)
    return p / jnp.maximum(jnp.sum(p, axis=-1, keepdims=True), 1e-30)


def rel_bucket(dist):
    exact = REL_BUCKETS // 2
    d = jnp.maximum(dist, 0)
    ratio = jnp.log(jnp.maximum(d, exact).astype(jnp.float32) / exact) / math.log(REL_MAX_DIST / exact)
    large = jnp.minimum(exact + (ratio * (REL_BUCKETS - exact)).astype(jnp.int32), REL_BUCKETS - 1)
    return jnp.where(d < exact, d, large)


def gather_pages(pool, page_table):
    g = pool[page_table]
    return g.reshape(g.shape[0], -1, pool.shape[2], pool.shape[3])


def sb_mixer(q, k, v, q_base):
    B, L, H, dh = q.shape
    blk = min(Q_BLOCK, L)
    nblk = L // blk
    k_pos = jnp.arange(k.shape[1])
    qb = q.reshape(B, nblk, blk, H, dh).swapaxes(0, 1)

    def one(args):
        qi, i = args
        q_pos = q_base + i * blk + jnp.arange(blk)
        z = jnp.einsum('bqhd,bkhd->bhqk', qi, k).astype(jnp.float32) * (dh ** -0.5)
        before = k_pos[None, :] < q_pos[:, None]
        log_keep = jnp.where(before, jax.nn.log_sigmoid(-z), 0.0)
        after = lax.cumsum(log_keep, axis=3, reverse=True) - log_keep
        w = jnp.where(before, jnp.exp(jax.nn.log_sigmoid(z) + after), 0.0)
        return jnp.einsum('bhqk,bkhd->bqhd', w.astype(v.dtype), v)

    o = lax.map(one, (qb, jnp.arange(nblk)))
    return o.swapaxes(0, 1).reshape(B, L, H, dh)


def gla_chunked(q, k, v, log_a, s0):
    B, L, H, dk = q.shape
    dv = v.shape[-1]
    C = min(GLA_CHUNK, L)
    n = L // C
    f32 = jnp.float32

    def to_chunks(t):
        return t.astype(f32).reshape(B, n, C, H, t.shape[-1]).transpose(1, 0, 3, 2, 4)

    qc, kc, vc, ac = to_chunks(q), to_chunks(k), to_chunks(v), to_chunks(log_a)
    causal = jnp.tril(jnp.ones((C, C), bool))

    def step(s, inp):
        qi, ki, vi, ai = inp
        b = jnp.cumsum(ai, axis=2)
        b_last = b[:, :, -1:, :]
        q_in = qi * jnp.exp(b)
        k_in = ki * jnp.exp(-b)
        att = jnp.where(causal, jnp.einsum('bhcd,bhed->bhce', q_in, k_in), 0.0)
        o = jnp.einsum('bhce,bhev->bhcv', att, vi) + jnp.einsum('bhcd,bhdv->bhcv', q_in, s)
        k_dec = ki * jnp.exp(b_last - b)
        s_new = jnp.exp(b_last[:, :, 0, :])[..., None] * s + jnp.einsum('bhcd,bhcv->bhdv', k_dec, vi)
        return s_new, o

    s_fin, o = lax.scan(step, s0.astype(f32), (qc, kc, vc, ac))
    return o.transpose(1, 0, 3, 2, 4).reshape(B, L, H, dv), s_fin


def even_mixer(h, w_in, w_a2, b_a2, onorm_g, w_out, past):
    B, L, _ = h.shape
    sizes = [SB_WIDTH] * 3 + [GLA_HEADS * GLA_DK] * 2 + [GLA_HEADS * GLA_DV] * 2 + [GLA_GATE_RANK]
    qa, ka, va, qb, kb, vb, rb, a1 = jnp.split(h @ w_in, _offsets(sizes), axis=-1)
    qa = qa.reshape(B, L, SB_HEADS, HEAD_DIM)
    ka = ka.reshape(B, L, SB_HEADS, HEAD_DIM)
    va = va.reshape(B, L, SB_HEADS, HEAD_DIM)
    qb = qb.reshape(B, L, GLA_HEADS, GLA_DK) * (GLA_DK ** -0.5)
    kb = kb.reshape(B, L, GLA_HEADS, GLA_DK)
    vb = vb.reshape(B, L, GLA_HEADS, GLA_DV)
    log_a = (jax.nn.log_sigmoid((a1 @ w_a2 + b_a2).astype(jnp.float32)) / GLA_TAU).reshape(B, L, GLA_HEADS, GLA_DK)
    if past is None:
        q_base = 0
        k_all, v_all = ka, va
        s0 = jnp.zeros((B, GLA_HEADS, GLA_DK, GLA_DV), jnp.float32)
    else:
        q_base = past['q_base']
        k_all = jnp.concatenate([past['sb_k'], ka], axis=1)
        v_all = jnp.concatenate([past['sb_v'], va], axis=1)
        s0 = past['gla']
    o_a = sb_mixer(qa, k_all, v_all, q_base)
    o_b, s_fin = gla_chunked(qb, kb, vb, log_a, s0)
    o_b = rms(o_b.astype(h.dtype), onorm_g) * jax.nn.silu(rb).reshape(B, L, GLA_HEADS, GLA_DV)
    out = jnp.concatenate([o_a.reshape(B, L, -1), o_b.reshape(B, L, -1)], axis=-1) @ w_out
    return out, (ka, va, s_fin.astype(h.dtype))


def compress(x, w1, b1, w2, b2):
    B, T, H, dh = x.shape
    nseg = T // CMP_STRIDE
    seg = x[:, :nseg * CMP_STRIDE].reshape(B, nseg, CMP_STRIDE, H, dh).transpose(0, 1, 3, 2, 4)
    seg = seg.reshape(B, nseg, H, CMP_STRIDE * dh)
    r = CMP_BLOCK // CMP_STRIDE
    w1s = w1.reshape(r, CMP_STRIDE * dh, w1.shape[-1])
    nb = nseg - r + 1
    hid = b1
    for m in range(r):
        hid = hid + jnp.einsum('bnhf,fe->bnhe', seg[:, m:m + nb], w1s[m])
    return jax.nn.gelu(hid) @ w2 + b2


def nsa_block(q, g, q_pos, kc, vc, c_end, ks_t, vs_t, cover, kw, vw, w_pos, rel_table):
    B, Lq, Hkv, G, dh = q.shape
    f32 = jnp.float32
    scale = dh ** -0.5

    def bias_2d(dist):
        b = rel_table[rel_bucket(dist)]
        return b.reshape(Lq, -1, Hkv, G).transpose(2, 3, 0, 1).astype(f32)

    dist_c = q_pos[:, None] - c_end[None, :]
    s_c = jnp.einsum('bqhgd,bnhd->bhgqn', q, kc).astype(f32) * scale + bias_2d(dist_c)
    p_c = masked_softmax(s_c, dist_c >= 0)
    o_c = jnp.einsum('bhgqn,bnhd->bqhgd', p_c.astype(vc.dtype), vc)
    n_sel = ks_t.shape[2]
    imp = jnp.einsum('bhgqn,nj->bhqj', p_c, cover)
    j = jnp.arange(n_sel)[None, :]
    cur = (q_pos // SLC_BLOCK)[:, None]
    valid = j * SLC_BLOCK <= q_pos[:, None]
    forced = valid & ((j == 0) | (j == cur) | (j == cur - 1))
    score = jnp.where(forced, FORCE, jnp.where(valid, imp, -FORCE))
    _, idx = lax.top_k(score, min(SLC_TOPN, n_sel))
    bi = jnp.arange(B)[:, None, None, None]
    hi = jnp.arange(Hkv)[None, :, None, None]
    kg = ks_t[bi, hi, idx].reshape(B, Hkv, Lq, -1, dh)
    vg = vs_t[bi, hi, idx].reshape(B, Hkv, Lq, -1, dh)
    s_pos = (idx[..., None] * SLC_BLOCK + jnp.arange(SLC_BLOCK)).reshape(B, Hkv, Lq, -1)
    tg = rel_table.reshape(REL_BUCKETS, Hkv, G)
    b_s = tg[rel_bucket(q_pos[:, None] - s_pos), hi].transpose(0, 1, 4, 2, 3).astype(f32)
    s_s = jnp.einsum('bqhgd,bhqkd->bhgqk', q, kg).astype(f32) * scale + b_s
    p_s = masked_softmax(s_s, (s_pos <= q_pos[:, None])[:, :, None])
    o_s = jnp.einsum('bhgqk,bhqkd->bqhgd', p_s.astype(vg.dtype), vg)
    dist_w = q_pos[:, None] - w_pos[None, :]
    s_w = jnp.einsum('bqhgd,bkhd->bhgqk', q, kw).astype(f32) * scale + bias_2d(dist_w)
    p_w = masked_softmax(s_w, (dist_w >= 0) & (dist_w < WINDOW) & (w_pos[None, :] >= 0))
    o_w = jnp.einsum('bhgqk,bkhd->bqhgd', p_w.astype(vw.dtype), vw)
    return g[..., 0:1] * o_c + g[..., 1:2] * o_s + g[..., 2:3] * o_w


def nsa_mixer(q, gates, kc, vc, ks_t, vs_t, kw_src, vw_src, q_base, rel_table):
    B, L, Hkv, G, dh = q.shape
    blk = min(Q_BLOCK, L)
    nblk = L // blk
    nb = kc.shape[1]
    c_end = jnp.arange(nb) * CMP_STRIDE + CMP_BLOCK - 1
    ci = jnp.arange(nb) * CMP_STRIDE
    sj = jnp.arange(ks_t.shape[2]) * SLC_BLOCK
    cover = ((ci[:, None] < sj[None, :] + SLC_BLOCK) & (ci[:, None] + CMP_BLOCK > sj[None, :])).astype(jnp.float32)
    qb = q.reshape(B, nblk, blk, Hkv, G, dh).swapaxes(0, 1)
    gb = gates.reshape(B, nblk, blk, Hkv, G, 3).swapaxes(0, 1)

    def one(args):
        qi, gi, i = args
        q_pos = q_base + i * blk + jnp.arange(blk)
        kw = lax.dynamic_slice_in_dim(kw_src, i * blk, blk + WINDOW, axis=1)
        vw = lax.dynamic_slice_in_dim(vw_src, i * blk, blk + WINDOW, axis=1)
        w_pos = q_base + i * blk - WINDOW + jnp.arange(blk + WINDOW)
        return nsa_block(qi, gi, q_pos, kc, vc, c_end, ks_t, vs_t, cover, kw, vw, w_pos, rel_table)

    o = lax.map(one, (qb, gb, jnp.arange(nblk)))
    return o.swapaxes(0, 1).reshape(B, L, Hkv, G, dh)


def odd_mixer(h, w_in, qk_g, cmp_w1, cmp_b1, cmp_w2, cmp_b2, rel_table, w_out, past):
    B, L, _ = h.shape
    kvw = NSA_KV_HEADS * HEAD_DIM
    sizes = [NSA_HEADS * HEAD_DIM] + [kvw] * 6 + [3 * NSA_HEADS]
    q, kc, vc, ks, vs, kw, vw, gt = jnp.split(h @ w_in, _offsets(sizes), axis=-1)

    def kv(t):
        return t.reshape(B, L, NSA_KV_HEADS, HEAD_DIM)

    q = rms(q.reshape(B, L, NSA_HEADS, HEAD_DIM), qk_g[0]).reshape(B, L, NSA_KV_HEADS, NSA_GROUP, HEAD_DIM)
    kc, vc, vs, vw = kv(kc), kv(vc), kv(vs), kv(vw)
    ks = rms(kv(ks), qk_g[2])
    kw = rms(kv(kw), qk_g[3])
    gates = jax.nn.sigmoid(gt.reshape(B, L, NSA_KV_HEADS, NSA_GROUP, 3))
    pad4 = lambda t, n: jnp.pad(t, ((0, 0), (n, 0), (0, 0), (0, 0)))
    if past is None:
        q_base = 0
        kc_all, vc_all, ks_all, vs_all = kc, vc, ks, vs
        kw_src, vw_src = pad4(kw, WINDOW), pad4(vw, WINDOW)
        w_keep = min(WINDOW, L)
        win_k_new, win_v_new = kw[:, L - w_keep:], vw[:, L - w_keep:]
    else:
        q_base = past['q_base']
        kc_all = jnp.concatenate([past['cmp_k'], kc], axis=1)
        vc_all = jnp.concatenate([past['cmp_v'], vc], axis=1)
        ks_all = jnp.concatenate([past['slc_k'], ks], axis=1)
        vs_all = jnp.concatenate([past['slc_v'], vs], axis=1)
        wk = jnp.concatenate([past['win_k'], kw], axis=1)
        wv = jnp.concatenate([past['win_v'], vw], axis=1)
        front = WINDOW - past['win_k'].shape[1]
        kw_src, vw_src = pad4(wk, front), pad4(wv, front)
        win_k_new, win_v_new = wk[:, L:], wv[:, L:]
    k_cmp = rms(compress(kc_all, cmp_w1[0], cmp_b1[0], cmp_w2[0], cmp_b2[0]), qk_g[1])
    v_cmp = compress(vc_all, cmp_w1[1], cmp_b1[1], cmp_w2[1], cmp_b2[1])
    T = ks_all.shape[1]
    n_sel = -(-T // SLC_BLOCK)
    tail = n_sel * SLC_BLOCK - T
    ks_p = jnp.pad(ks_all, ((0, 0), (0, tail), (0, 0), (0, 0)))
    vs_p = jnp.pad(vs_all, ((0, 0), (0, tail), (0, 0), (0, 0)))
    ks_t = ks_p.reshape(B, n_sel, SLC_BLOCK, NSA_KV_HEADS, HEAD_DIM).transpose(0, 3, 1, 2, 4)
    vs_t = vs_p.reshape(B, n_sel, SLC_BLOCK, NSA_KV_HEADS, HEAD_DIM).transpose(0, 3, 1, 2, 4)
    o = nsa_mixer(q, gates, k_cmp, v_cmp, ks_t, vs_t, kw_src, vw_src, q_base, rel_table)
    out = o.reshape(B, L, ODD_MIX) @ w_out
    return out, (kc, vc, ks, vs, win_k_new, win_v_new)


def trunk(x, c, past, norm_g, ada_w, ada_b, ffn_w_in, ffn_w_out, even_w_in, gla_w_a2, gla_b_a2,
          gla_onorm_g, even_w_out, odd_w_in, nsa_qk_g, cmp_w1, cmp_b1, cmp_w2, cmp_b2, rel_table, odd_w_out):
    B = x.shape[0]
    even_states, odd_states = [], []
    for li in range(DEPTH):
        mod = (jax.nn.silu(c) @ ada_w[li] + ada_b[li]).reshape(B, 9, 1, D_MODEL)
        h = modulate(x, norm_g[li, 0], mod[:, 0], mod[:, 1])
        x = x + 0.5 * mod[:, 2] * swiglu(h, ffn_w_in[li, 0], ffn_w_out[li, 0])
        h = modulate(x, norm_g[li, 1], mod[:, 3], mod[:, 4])
        if li % 2 == 0:
            e = li // 2
            lp = None
            if past is not None:
                pt = past['page_table']
                lp = {'q_base': past['q_base'], 'sb_k': gather_pages(past['sb_k'][e], pt),
                      'sb_v': gather_pages(past['sb_v'][e], pt), 'gla': past['gla'][e]}
            out, st = even_mixer(h, even_w_in[e], gla_w_a2[e], gla_b_a2[e], gla_onorm_g[e], even_w_out[e], lp)
            even_states.append(st)
        else:
            o_i = li // 2
            lp = None
            if past is not None:
                pt = past['page_table']
                lp = {'q_base': past['q_base'],
                      'cmp_k': gather_pages(past['cmp_k'][o_i], pt), 'cmp_v': gather_pages(past['cmp_v'][o_i], pt),
                      'slc_k': gather_pages(past['slc_k'][o_i], pt), 'slc_v': gather_pages(past['slc_v'][o_i], pt),
                      'win_k': past['win_k'][o_i], 'win_v': past['win_v'][o_i]}
            out, st = odd_mixer(h, odd_w_in[o_i], nsa_qk_g[o_i], cmp_w1[o_i], cmp_b1[o_i], cmp_w2[o_i], cmp_b2[o_i],
                                rel_table, odd_w_out[o_i], lp)
            odd_states.append(st)
        x = x + mod[:, 5] * out
        h = modulate(x, norm_g[li, 2], mod[:, 6], mod[:, 7])
        x = x + 0.5 * mod[:, 8] * swiglu(h, ffn_w_in[li, 1], ffn_w_out[li, 1])
    even_new = [jnp.stack([s[i] for s in even_states]) for i in range(3)]
    odd_new = [jnp.stack([s[i] for s in odd_states]) for i in range(6)]
    return x, even_new, odd_new


def setup_inputs(seed: int = 0) -> dict:
    key = jax.random.key(seed)
    ks = jax.random.split(key, 40)
    n_pages = PAST_LEN // PAGE_SIZE
    n_pool = (DEC_BATCH * n_pages * 5) // 4
    w_buf = min(WINDOW, PAST_LEN)
    f32 = jnp.float32

    def nrm(k, shape, s=1.0):
        return s * jax.random.normal(k, shape, f32)

    page_table = jax.random.permutation(ks[0], n_pool)[:DEC_BATCH * n_pages].reshape(DEC_BATCH, n_pages).astype(jnp.int32)
    return {
        'x_prompt': nrm(ks[1], (BATCH, SEQ, D_MODEL)),
        'x_sample': nrm(ks[2], (DEC_BATCH, DEC_SEQ, D_MODEL)),
        'cache_sb_k': nrm(ks[3], (N_EVEN, n_pool, PAGE_SIZE, SB_HEADS, HEAD_DIM)),
        'cache_sb_v': nrm(ks[4], (N_EVEN, n_pool, PAGE_SIZE, SB_HEADS, HEAD_DIM)),
        'state_gla': nrm(ks[5], (N_EVEN, DEC_BATCH, GLA_HEADS, GLA_DK, GLA_DV)),
        'cache_cmp_k': nrm(ks[6], (N_ODD, n_pool, PAGE_SIZE, NSA_KV_HEADS, HEAD_DIM)),
        'cache_cmp_v': nrm(ks[7], (N_ODD, n_pool, PAGE_SIZE, NSA_KV_HEADS, HEAD_DIM)),
        'cache_slc_k': nrm(ks[8], (N_ODD, n_pool, PAGE_SIZE, NSA_KV_HEADS, HEAD_DIM)),
        'cache_slc_v': nrm(ks[9], (N_ODD, n_pool, PAGE_SIZE, NSA_KV_HEADS, HEAD_DIM)),
        'cache_win_k': nrm(ks[10], (N_ODD, DEC_BATCH, w_buf, NSA_KV_HEADS, HEAD_DIM)),
        'cache_win_v': nrm(ks[11], (N_ODD, DEC_BATCH, w_buf, NSA_KV_HEADS, HEAD_DIM)),
        'page_table': page_table,
        'c_prompt': nrm(ks[12], (BATCH, D_MODEL)),
        'c_sample': nrm(ks[13], (DEC_BATCH, D_MODEL)),
        'norm_g': 1.0 + nrm(ks[14], (DEPTH, 3, D_MODEL), 0.02),
        'ada_w': nrm(ks[15], (DEPTH, D_MODEL, 9 * D_MODEL), 0.5 * D_MODEL ** -0.5),
        'ada_b': nrm(ks[16], (DEPTH, 9 * D_MODEL), 0.02),
        'ffn_w_in': nrm(ks[17], (DEPTH, 2, D_MODEL, 2 * D_FF), D_MODEL ** -0.5),
        'ffn_w_out': nrm(ks[18], (DEPTH, 2, D_FF, D_MODEL), D_FF ** -0.5),
        'even_w_in': nrm(ks[19], (N_EVEN, D_MODEL, EVEN_IN), D_MODEL ** -0.5),
        'gla_w_a2': nrm(ks[20], (N_EVEN, GLA_GATE_RANK, GLA_HEADS * GLA_DK), GLA_GATE_RANK ** -0.5),
        'gla_b_a2': nrm(ks[21], (N_EVEN, GLA_HEADS * GLA_DK), 0.1),
        'gla_onorm_g': 1.0 + nrm(ks[22], (N_EVEN, GLA_DV), 0.02),
        'even_w_out': nrm(ks[23], (N_EVEN, EVEN_MIX, D_MODEL), EVEN_MIX ** -0.5),
        'odd_w_in': nrm(ks[24], (N_ODD, D_MODEL, ODD_IN), D_MODEL ** -0.5),
        'nsa_qk_g': 1.0 + nrm(ks[25], (N_ODD, 4, HEAD_DIM), 0.02),
        'cmp_w1': nrm(ks[26], (N_ODD, 2, CMP_BLOCK * HEAD_DIM, CMP_HIDDEN), (CMP_BLOCK * HEAD_DIM) ** -0.5),
        'cmp_b1': nrm(ks[27], (N_ODD, 2, CMP_HIDDEN), 0.02),
        'cmp_w2': nrm(ks[28], (N_ODD, 2, CMP_HIDDEN, HEAD_DIM), CMP_HIDDEN ** -0.5),
        'cmp_b2': nrm(ks[29], (N_ODD, 2, HEAD_DIM), 0.02),
        'rel_table': nrm(ks[30], (REL_BUCKETS, NSA_HEADS), 0.5),
        'odd_w_out': nrm(ks[31], (N_ODD, ODD_MIX, D_MODEL), ODD_MIX ** -0.5),
    }


def reference(x_prompt, x_sample, cache_sb_k, cache_sb_v, state_gla, cache_cmp_k, cache_cmp_v,
              cache_slc_k, cache_slc_v, cache_win_k, cache_win_v, page_table, c_prompt, c_sample,
              norm_g, ada_w, ada_b, ffn_w_in, ffn_w_out, even_w_in, gla_w_a2, gla_b_a2, gla_onorm_g,
              even_w_out, odd_w_in, nsa_qk_g, cmp_w1, cmp_b1, cmp_w2, cmp_b2, rel_table, odd_w_out):
    y_prompt, p_even, p_odd = trunk(
        x_prompt, c_prompt, None, norm_g, ada_w, ada_b, ffn_w_in, ffn_w_out, even_w_in, gla_w_a2, gla_b_a2,
        gla_onorm_g, even_w_out, odd_w_in, nsa_qk_g, cmp_w1, cmp_b1, cmp_w2, cmp_b2, rel_table, odd_w_out)
    past = {'q_base': page_table.shape[1] * cache_sb_k.shape[2], 'page_table': page_table,
            'sb_k': cache_sb_k, 'sb_v': cache_sb_v, 'gla': state_gla,
            'cmp_k': cache_cmp_k, 'cmp_v': cache_cmp_v, 'slc_k': cache_slc_k, 'slc_v': cache_slc_v,
            'win_k': cache_win_k, 'win_v': cache_win_v}
    y_sample, s_even, s_odd = trunk(
        x_sample, c_sample, past, norm_g, ada_w, ada_b, ffn_w_in, ffn_w_out, even_w_in, gla_w_a2, gla_b_a2,
        gla_onorm_g, even_w_out, odd_w_in, nsa_qk_g, cmp_w1, cmp_b1, cmp_w2, cmp_b2, rel_table, odd_w_out)
    p_sb_k, p_sb_v, p_gla = p_even
    p_cmp_k, p_cmp_v, p_slc_k, p_slc_v, p_win_k, p_win_v = p_odd
    s_sb_k, s_sb_v, s_gla = s_even
    s_cmp_k, s_cmp_v, s_slc_k, s_slc_v, s_win_k, s_win_v = s_odd
    return (y_prompt, y_sample, p_sb_k, p_sb_v, p_gla, p_cmp_k, p_cmp_v, p_slc_k, p_slc_v, p_win_k, p_win_v,
            s_sb_k, s_sb_v, s_gla, s_cmp_k, s_cmp_v, s_slc_k, s_slc_v, s_win_k, s_win_v)
```

```python
import functools
import math

import jax
import jax.numpy as jnp
from jax import lax
from jax.experimental import pallas as pl
from jax.experimental.pallas import tpu as pltpu

F32 = jnp.float32
BF16 = jnp.bfloat16

EPS = 1e-6
NEG = -1e30
FORCE = 1e6
HEAD_DIM = 64
GLA_TAU = 16.0
GLA_CHUNK = 64
CMP_BLOCK = 32
CMP_STRIDE = 16
SLC_BLOCK = 64
SLC_TOPN = 16
WINDOW = 512
REL_BUCKETS = 32
REL_MAX_DIST = 1024
PAGE_GROUP = 16
LANES = 128
VMEM_LIMIT = 56 * 1024 * 1024


def _cparams(sem, vmem=VMEM_LIMIT):
    return pltpu.CompilerParams(dimension_semantics=sem, vmem_limit_bytes=vmem)


def _dot(a, b):
    return jnp.dot(a, b, preferred_element_type=F32)


def _dot_nt(a, b):
    return lax.dot_general(a, b, (((1,), (1,)), ((), ())), preferred_element_type=F32)


def _dot_tn(a, b):
    return lax.dot_general(a, b, (((0,), (0,)), ((), ())), preferred_element_type=F32)


def _split2(x):
    hi = x.astype(BF16)
    lo = (x - hi.astype(F32)).astype(BF16)
    return hi, lo


def _dot_f32_by01(x, m01):
    hi, lo = _split2(x)
    return _dot(hi, m01) + _dot(lo, m01)


def _dot_01_by_f32(m01, x):
    hi, lo = _split2(x)
    return _dot(m01, hi) + _dot(m01, lo)


def _softplus(z):
    return jnp.maximum(z, 0.0) + jnp.log1p(jnp.exp(-jnp.abs(z)))


def _silu(a):
    return a * jax.nn.sigmoid(a)


def _iota(shape, dim):
    return lax.broadcasted_iota(jnp.int32, shape, dim)


def _modulate(x, g, shift, scale):
    ms = jnp.mean(x * x, axis=-1, keepdims=True)
    y = x * lax.rsqrt(ms + EPS) * g
    return y * (1.0 + scale) + shift


def _head_rms(x, bd, g):
    ms = _dot_f32_by01(x * x, bd) * (1.0 / HEAD_DIM)
    return x * lax.rsqrt(ms + EPS) * g


def _const_spec(shape):
    nd = len(shape)
    return pl.BlockSpec(shape, lambda *_: (0,) * nd, pipeline_mode=pl.Buffered(1))


def _row_tiles(B, L):
    if L >= 512:
        return 1, 512
    return B, L


def _ada_kernel(c_ref, w_ref, b_ref, o_ref):
    c = c_ref[...]
    a = _silu(c).astype(BF16)
    o_ref[0] = _dot(a, w_ref[0].astype(BF16)) + b_ref[0]


def ada_mod(c, ada_w, ada_b):
    R, D = c.shape
    depth, _, N = ada_w.shape
    tn = 1024
    return pl.pallas_call(
        _ada_kernel,
        grid=(depth, N // tn),
        in_specs=[
            pl.BlockSpec((R, D), lambda l, n: (0, 0)),
            pl.BlockSpec((1, D, tn), lambda l, n: (l, 0, n)),
            pl.BlockSpec((1, 1, tn), lambda l, n: (l, 0, n)),
        ],
        out_specs=pl.BlockSpec((1, R, tn), lambda l, n: (l, 0, n)),
        out_shape=jax.ShapeDtypeStruct((depth, R, N), F32),
        compiler_params=_cparams(("parallel", "parallel")),
        name="ada_mod",
    )(c, ada_w, ada_b.reshape(depth, 1, N))


def _ffn_kernel(x_ref, sh_ref, sc_ref, gt_ref, g_ref, win_ref, wout_ref, o_ref, *, d_ff, tf):
    x = x_ref[...]
    tb, tl, D = x.shape
    h = _modulate(x, g_ref[...], sh_ref[...], sc_ref[...])
    hb = h.reshape(tb * tl, D).astype(BF16)
    acc = jnp.zeros((tb * tl, D), F32)
    for j in range(d_ff // tf):
        a = _dot(hb, win_ref[:, j * tf:(j + 1) * tf])
        b = _dot(hb, win_ref[:, d_ff + j * tf:d_ff + (j + 1) * tf])
        u = (_silu(a) * b).astype(BF16)
        acc = acc + _dot(u, wout_ref[j * tf:(j + 1) * tf, :])
    o_ref[...] = x + 0.5 * gt_ref[...] * acc.reshape(tb, tl, D)


def ffn(x, shift, scale, gate, g, w_in, w_out):
    B, L, D = x.shape
    d_ff = w_out.shape[0]
    tb, tl = _row_tiles(B, L)
    xs = pl.BlockSpec((tb, tl, D), lambda b, l: (b, l, 0))
    ms = pl.BlockSpec((tb, 1, D), lambda b, l: (b, 0, 0))
    return pl.pallas_call(
        functools.partial(_ffn_kernel, d_ff=d_ff, tf=256),
        grid=(B // tb, L // tl),
        in_specs=[xs, ms, ms, ms, _const_spec((1, D)), _const_spec(w_in.shape), _const_spec(w_out.shape)],
        out_specs=xs,
        out_shape=jax.ShapeDtypeStruct(x.shape, F32),
        compiler_params=_cparams(("parallel", "parallel")),
        name="ffn",
    )(x, shift, scale, gate, g, w_in, w_out)


def _even_proj_kernel(x_ref, sh_ref, sc_ref, g_ref, wqa, wka, wva, wqb, wkb, wvb, wrb, wa1, wa2, ba2,
                      qa, ka, va, qb, kb, vb, rb, la):
    x = x_ref[...]
    tb, tl, D = x.shape
    h = _modulate(x, g_ref[...], sh_ref[...], sc_ref[...])
    hb = h.reshape(tb * tl, D).astype(BF16)
    for w, o in ((wqa, qa), (wka, ka), (wva, va), (wqb, qb), (wkb, kb), (wvb, vb), (wrb, rb)):
        o[...] = _dot(hb, w[...]).reshape(o.shape)
    a1 = _dot(hb, wa1[...]).astype(BF16)
    z = _dot(a1, wa2[...]) + ba2[...]
    la[...] = (-_softplus(-z) * (1.0 / GLA_TAU)).reshape(la.shape)


def even_proj(x, shift, scale, g, ws, wa1, wa2, ba2):
    B, L, D = x.shape
    tb, tl = _row_tiles(B, L)
    xs = pl.BlockSpec((tb, tl, D), lambda b, l: (b, l, 0))
    ms = pl.BlockSpec((tb, 1, D), lambda b, l: (b, 0, 0))
    widths = [w.shape[1] for w in ws] + [wa2.shape[1]]
    return pl.pallas_call(
        _even_proj_kernel,
        grid=(B // tb, L // tl),
        in_specs=[xs, ms, ms, _const_spec((1, D))] + [_const_spec(w.shape) for w in ws]
        + [_const_spec(wa1.shape), _const_spec(wa2.shape), _const_spec(ba2.shape)],
        out_specs=[pl.BlockSpec((tb, tl, n), lambda b, l: (b, l, 0)) for n in widths],
        out_shape=[jax.ShapeDtypeStruct((B, L, n), F32) for n in widths],
        compiler_params=_cparams(("parallel", "parallel")),
        name="even_proj",
    )(x, shift, scale, g, *ws, wa1, wa2, ba2)


def _odd_proj_kernel(x_ref, sh_ref, sc_ref, g_ref, wq, wkc, wvc, wks, wvs, wkw, wvw, wg0, wg1,
                     bdq, bdk, gq, gks, gkw,
                     q, kc, vc, ks, vs, kw, vw, gates):
    x = x_ref[...]
    tb, tl, D = x.shape
    h = _modulate(x, g_ref[...], sh_ref[...], sc_ref[...])
    hb = h.reshape(tb * tl, D).astype(BF16)
    q[...] = _head_rms(_dot(hb, wq[...]), bdq[...], gq[...]).reshape(q.shape)
    kc[...] = _dot(hb, wkc[...]).reshape(kc.shape)
    vc[...] = _dot(hb, wvc[...]).reshape(vc.shape)
    ks[...] = _head_rms(_dot(hb, wks[...]), bdk[...], gks[...]).reshape(ks.shape)
    vs[...] = _dot(hb, wvs[...]).reshape(vs.shape)
    kw[...] = _head_rms(_dot(hb, wkw[...]), bdk[...], gkw[...]).reshape(kw.shape)
    vw[...] = _dot(hb, wvw[...]).reshape(vw.shape)
    gates[0] = jax.nn.sigmoid(_dot(hb, wg0[...])).reshape(gates.shape[1:])
    gates[1] = jax.nn.sigmoid(_dot(hb, wg1[...])).reshape(gates.shape[1:])


def odd_proj(x, shift, scale, g, ws, consts):
    B, L, D = x.shape
    tb, tl = _row_tiles(B, L)
    xs = pl.BlockSpec((tb, tl, D), lambda b, l: (b, l, 0))
    ms = pl.BlockSpec((tb, 1, D), lambda b, l: (b, 0, 0))
    widths = [w.shape[1] for w in ws[:7]]
    out_specs = [pl.BlockSpec((tb, tl, n), lambda b, l: (b, l, 0)) for n in widths]
    out_specs.append(pl.BlockSpec((2, tb, tl, LANES), lambda b, l: (0, b, l, 0)))
    out_shape = [jax.ShapeDtypeStruct((B, L, n), F32) for n in widths]
    out_shape.append(jax.ShapeDtypeStruct((2, B, L, LANES), F32))
    return pl.pallas_call(
        _odd_proj_kernel,
        grid=(B // tb, L // tl),
        in_specs=[xs, ms, ms, _const_spec((1, D))] + [_const_spec(w.shape) for w in ws]
        + [_const_spec(c.shape) for c in consts],
        out_specs=out_specs,
        out_shape=out_shape,
        compiler_params=_cparams(("parallel", "parallel")),
        name="odd_proj",
    )(x, shift, scale, g, *ws, *consts)


def _out_proj_kernel(*refs, n_in):
    x_ref, gt_ref = refs[0], refs[1]
    a_refs = refs[2:2 + n_in]
    w_refs = refs[2 + n_in:2 + 2 * n_in]
    o_ref = refs[2 + 2 * n_in]
    x = x_ref[...]
    tb, tl, D = x.shape
    acc = jnp.zeros((tb * tl, D), F32)
    for a, w in zip(a_refs, w_refs):
        av = a[...]
        acc = acc + _dot(av.reshape(tb * tl, av.shape[-1]).astype(BF16), w[...])
    o_ref[...] = x + gt_ref[...] * acc.reshape(tb, tl, D)


def out_proj(x, gate, acts, ws):
    B, L, D = x.shape
    tb, tl = _row_tiles(B, L)
    xs = pl.BlockSpec((tb, tl, D), lambda b, l: (b, l, 0))
    ms = pl.BlockSpec((tb, 1, D), lambda b, l: (b, 0, 0))
    return pl.pallas_call(
        functools.partial(_out_proj_kernel, n_in=len(acts)),
        grid=(B // tb, L // tl),
        in_specs=[xs, ms] + [pl.BlockSpec((tb, tl, a.shape[-1]), lambda b, l: (b, l, 0)) for a in acts]
        + [_const_spec(w.shape) for w in ws],
        out_specs=xs,
        out_shape=jax.ShapeDtypeStruct(x.shape, F32),
        compiler_params=_cparams(("parallel", "parallel")),
        name="out_proj",
    )(x, gate, *acts, *ws)


def _log2(n):
    assert n > 0 and n & (n - 1) == 0, n
    return n.bit_length() - 1


def _later_keys_01(tk):
    return jnp.where(_iota((tk, tk), 0) > _iota((tk, tk), 1), 1.0, 0.0).astype(BF16)


def _sb_tile(q2, kt, vt, run, acc, u01, before):
    z = _dot_nt(q2, kt)
    sp = _softplus(z)
    lk = -sp
    ls = z - sp
    if before is not None:
        lk = jnp.where(before, lk, 0.0)
    cs = _dot_f32_by01(lk, u01)
    w = jnp.exp(ls + cs + run)
    if before is not None:
        w = jnp.where(before, w, 0.0)
    acc = acc + _dot(w.astype(BF16), vt)
    run = run + cs[:, 0:1] + lk[:, 0:1]
    return run, acc


def _sb_prompt_kernel(q_ref, k_ref, v_ref, o_ref, *, tq):
    i = pl.program_id(2)
    q = q_ref[0] * (HEAD_DIM ** -0.5)
    lane = _iota((tq, LANES), 1)
    q2 = jnp.concatenate([jnp.where(lane < HEAD_DIM, q, 0.0), jnp.where(lane >= HEAD_DIM, q, 0.0)],
                         axis=0).astype(BF16)
    u01 = _later_keys_01(tq)
    before = _iota((2 * tq, tq), 1) < (_iota((2 * tq, tq), 0) & (tq - 1))

    def tile(j, run, acc, mask):
        start = pl.multiple_of(j * tq, tq)
        kt = k_ref[0, pl.ds(start, tq), :].astype(BF16)
        vt = v_ref[0, pl.ds(start, tq), :].astype(BF16)
        return _sb_tile(q2, kt, vt, run, acc, u01, mask)

    run, acc = tile(i, jnp.zeros((2 * tq, 1), F32), jnp.zeros((2 * tq, LANES), F32), before)
    run, acc = lax.fori_loop(1, i + 1, lambda jj, c: tile(i - jj, c[0], c[1], None), (run, acc))
    o_ref[0] = jnp.where(lane < HEAD_DIM, acc[:tq], acc[tq:])


def sb_prompt(q, k, v):
    B, L, W = q.shape
    tq = 128
    qs = pl.BlockSpec((1, tq, LANES), lambda b, p, i: (b, i, p))
    ks = pl.BlockSpec((1, L, LANES), lambda b, p, i: (b, 0, p))
    return pl.pallas_call(
        functools.partial(_sb_prompt_kernel, tq=tq),
        grid=(B, W // LANES, L // tq),
        in_specs=[qs, ks, ks],
        out_specs=qs,
        out_shape=jax.ShapeDtypeStruct(q.shape, F32),
        compiler_params=_cparams(("parallel", "parallel", "arbitrary")),
        name="sb_prompt",
    )(q, k, v)


def _sb_sample_kernel(pt_ref, q_ref, kn_ref, vn_ref, *rest, n_tok, group):
    k_refs, v_refs = rest[:group], rest[group:2 * group]
    o_ref, run_ref, acc_ref = rest[2 * group:]
    s = pl.program_id(1)
    q = q_ref[0] * (HEAD_DIM ** -0.5)
    W = q.shape[1]
    n_heads = W // HEAD_DIM
    R = n_heads * n_tok
    tk = kn_ref.shape[1]
    own = (_iota((R, W), 0) >> _log2(n_tok)) == (_iota((R, W), 1) >> _log2(HEAD_DIM))
    q2 = jnp.where(own, jnp.concatenate([q] * n_heads, axis=0), 0.0).astype(BF16)
    u01 = _later_keys_01(tk)

    @pl.when(s == 0)
    def _():
        before = _iota((R, tk), 1) < (_iota((R, tk), 0) & (n_tok - 1))
        run, acc = _sb_tile(q2, kn_ref[0].astype(BF16), vn_ref[0].astype(BF16),
                            jnp.zeros((R, 1), F32), jnp.zeros((R, W), F32), u01, before)
        run_ref[...] = run
        acc_ref[...] = acc

    @pl.when(s > 0)
    def _():
        run, acc = run_ref[...], acc_ref[...]
        for kr, vr in zip(k_refs, v_refs):
            run, acc = _sb_tile(q2, kr[0].astype(BF16), vr[0].astype(BF16), run, acc, u01, None)
        run_ref[...] = run
        acc_ref[...] = acc

    @pl.when(s == pl.num_programs(1) - 1)
    def _():
        acc = jnp.where(own, acc_ref[...], 0.0)
        out = acc[0:n_tok]
        for h in range(1, n_heads):
            out = out + acc[h * n_tok:(h + 1) * n_tok]
        o_ref[0] = out


def sb_sample(q, k_new, v_new, pool_k, pool_v, page_table):
    B, n_tok, W = q.shape
    _, page, _ = pool_k.shape
    n_pages = page_table.shape[1]
    group = PAGE_GROUP
    n_steps = 1 + n_pages // group
    pad = ((0, 0), (0, page - n_tok), (0, 0))
    kn, vn = jnp.pad(k_new, pad), jnp.pad(v_new, pad)

    def page_spec(kk):
        def imap(b, s, pt):
            return (pt[b, n_pages - 1 - (jnp.maximum(s, 1) - 1) * group - kk], 0, 0)
        return pl.BlockSpec((1, page, W), imap)

    qs = pl.BlockSpec((1, n_tok, W), lambda b, s, pt: (b, 0, 0))
    ns = pl.BlockSpec((1, page, W), lambda b, s, pt: (b, 0, 0))
    R = (W // HEAD_DIM) * n_tok
    grid_spec = pltpu.PrefetchScalarGridSpec(
        num_scalar_prefetch=1,
        grid=(B, n_steps),
        in_specs=[qs, ns, ns] + [page_spec(kk) for kk in range(group)] * 2,
        out_specs=qs,
        scratch_shapes=[pltpu.VMEM((R, 1), F32), pltpu.VMEM((R, W), F32)],
    )
    return pl.pallas_call(
        functools.partial(_sb_sample_kernel, n_tok=n_tok, group=group),
        grid_spec=grid_spec,
        out_shape=jax.ShapeDtypeStruct(q.shape, F32),
        compiler_params=_cparams(("parallel", "arbitrary")),
        name="sb_sample",
    )(page_table, q, kn, vn, *([pool_k] * group), *([pool_v] * group))


def _gla_kernel(q_ref, k_ref, v_ref, r_ref, la_ref, s0_ref, g_ref, o_ref, sfin_ref, st, *, dk, dv):
    c = pl.program_id(1)

    @pl.when(c == 0)
    def _():
        st[...] = s0_ref[0]

    q = q_ref[0] * (dk ** -0.5)
    k = k_ref[0]
    la = la_ref[0]
    C, HK = q.shape
    n_heads = HK // dk
    causal = _iota((C, C), 0) >= _iota((C, C), 1)
    ltri = jnp.where(causal, 1.0, 0.0).astype(BF16)
    hi = la.astype(BF16)
    r1 = la - hi.astype(F32)
    mid = r1.astype(BF16)
    lo = (r1 - mid.astype(F32)).astype(BF16)
    b = _dot(ltri, hi) + _dot(ltri, mid) + _dot(ltri, lo)
    b_last = b[C - 1:C, :]
    q_in = q * jnp.exp(b)
    k_in = (k * jnp.exp(-b)).astype(BF16)
    k_dec = k * jnp.exp(b_last - b)
    lane_h = _iota((C, HK), 1) >> _log2(dk)
    s_old = st[...]
    s_old_b = s_old.astype(BF16)
    upd = jnp.zeros(s_old.shape, F32)
    for h in range(n_heads):
        own = lane_h == h
        qh = jnp.where(own, q_in, 0.0).astype(BF16)
        att = jnp.where(causal, _dot_nt(qh, k_in), 0.0)
        vh = v_ref[0, :, h * dv:(h + 1) * dv].astype(BF16)
        oh = _dot(att.astype(BF16), vh) + _dot_nt(qh, s_old_b)
        y = oh * lax.rsqrt(jnp.mean(oh * oh, axis=-1, keepdims=True) + EPS) * g_ref[...]
        o_ref[0, :, h * dv:(h + 1) * dv] = y * _silu(r_ref[0, :, h * dv:(h + 1) * dv])
        kd = jnp.where(own, k_dec, 0.0).astype(BF16)
        upd = upd + _dot_tn(vh, kd)
    st[...] = s_old * jnp.exp(b_last) + upd

    @pl.when(c == pl.num_programs(1) - 1)
    def _():
        sfin_ref[0] = st[...]


def gla(q, k, v, r, la, s0t, g):
    B, L, HK = q.shape
    HV = v.shape[2]
    dv = g.shape[1]
    dk = HK // (HV // dv)
    C = min(GLA_CHUNK, L)
    ks = pl.BlockSpec((1, C, HK), lambda b, c: (b, c, 0))
    vs = pl.BlockSpec((1, C, HV), lambda b, c: (b, c, 0))
    ss = pl.BlockSpec((1, dv, HK), lambda b, c: (b, 0, 0))
    return pl.pallas_call(
        functools.partial(_gla_kernel, dk=dk, dv=dv),
        grid=(B, L // C),
        in_specs=[ks, ks, vs, vs, ks, ss, pl.BlockSpec((1, dv), lambda b, c: (0, 0))],
        out_specs=[vs, ss],
        out_shape=[jax.ShapeDtypeStruct(v.shape, F32), jax.ShapeDtypeStruct(s0t.shape, F32)],
        scratch_shapes=[pltpu.VMEM((dv, HK), F32)],
        compiler_params=_cparams(("parallel", "arbitrary")),
        name="gla",
    )(q, k, v, r, la, s0t, g)


def _gather_kernel(pt_ref, *refs, group):
    o_ref = refs[group]
    page = refs[0].shape[1]
    for kk in range(group):
        o_ref[0, kk * page:(kk + 1) * page, :] = refs[kk][0]


def gather_pages(pool, page_table):
    _, page, W = pool.shape
    B, n_pages = page_table.shape
    group = PAGE_GROUP

    def page_spec(kk):
        return pl.BlockSpec((1, page, W), lambda b, s, pt: (pt[b, s * group + kk], 0, 0))

    grid_spec = pltpu.PrefetchScalarGridSpec(
        num_scalar_prefetch=1,
        grid=(B, n_pages // group),
        in_specs=[page_spec(kk) for kk in range(group)],
        out_specs=pl.BlockSpec((1, group * page, W), lambda b, s, pt: (b, s, 0)),
    )
    return pl.pallas_call(
        functools.partial(_gather_kernel, group=group),
        grid_spec=grid_spec,
        out_shape=jax.ShapeDtypeStruct((B, n_pages * page, W), F32),
        compiler_params=_cparams(("parallel", "parallel")),
        name="gather_pages",
    )(page_table, *([pool] * group))


def _gelu_tanh(x):
    return x * (0.5 * (1.0 + jnp.tanh(math.sqrt(2.0 / math.pi) * (x + 0.044715 * (x * x * x)))))


def _compress_kernel(*refs, n_pairs, nseg, norm):
    x_refs = refs[:n_pairs]
    w1_ref, b1_ref, w2_ref, b2_ref, bd_ref, g_ref, o_ref = refs[n_pairs:]
    wh = w1_ref.shape[3]
    R = n_pairs * nseg
    p0 = jnp.zeros((R, wh), F32)
    p1 = jnp.zeros((R, wh), F32)
    for r in range(CMP_STRIDE):
        rows = jnp.concatenate([x[0, pl.ds(r, nseg, stride=CMP_STRIDE), :] for x in x_refs], axis=0).astype(BF16)
        p0 = p0 + _dot(rows, w1_ref[0, r])
        p1 = p1 + _dot(rows, w1_ref[1, r])
    hid = b1_ref[...] + p0 + pltpu.roll(p1, R - 1, 0)
    y = _dot(_gelu_tanh(hid).astype(BF16), w2_ref[...]) + b2_ref[...]
    if norm:
        y = _head_rms(y, bd_ref[...], g_ref[...])
    y = jnp.where((_iota(y.shape, 0) & (nseg - 1)) < nseg - 1, y, 0.0)
    for p in range(n_pairs):
        o_ref[0, :, p * LANES:(p + 1) * LANES] = y[p * nseg:(p + 1) * nseg]


def compress(x, n_rows, w1e, b1e, w2e, b2e, bd, g, norm):
    B, _, W = x.shape
    nseg = n_rows // CMP_STRIDE
    _log2(nseg)
    n_pairs = W // LANES
    return pl.pallas_call(
        functools.partial(_compress_kernel, n_pairs=n_pairs, nseg=nseg, norm=norm),
        grid=(B,),
        in_specs=[pl.BlockSpec((1, n_rows, LANES), functools.partial(lambda b, p: (b, 0, p), p=p))
                  for p in range(n_pairs)]
        + [_const_spec(a.shape) for a in (w1e, b1e, w2e, b2e, bd, g)],
        out_specs=pl.BlockSpec((1, nseg, W), lambda b: (b, 0, 0)),
        out_shape=jax.ShapeDtypeStruct((B, nseg, W), F32),
        compiler_params=_cparams(("parallel",)),
        name="compress",
    )(*([x] * n_pairs), w1e, b1e, w2e, b2e, bd, g)


_REL_EXACT = REL_BUCKETS // 2
_REL_THRESHOLDS = tuple(
    math.ceil(_REL_EXACT * (REL_MAX_DIST / _REL_EXACT) ** (k / (REL_BUCKETS - _REL_EXACT)) - 1e-9)
    for k in range(1, REL_BUCKETS - _REL_EXACT))


def _bias_kernel(base_ref, tbl_ref, o_ref, *, cstep):
    t = pl.program_id(0)
    _, H, R, C = o_ref.shape
    dist = base_ref[t] + _iota((R, C), 0) - cstep * _iota((R, C), 1)
    d = jnp.maximum(dist, 0)
    big = jnp.full((R, C), _REL_EXACT, jnp.int32)
    for th in _REL_THRESHOLDS:
        big = big + jnp.where(d >= th, 1, 0)
    bucket = jnp.where(d < _REL_EXACT, d, big)
    for h in range(H):
        acc = jnp.full((R, C), tbl_ref[h, REL_BUCKETS - 1], F32)
        for u in range(REL_BUCKETS - 1):
            acc = jnp.where(bucket == u, tbl_ref[h, u], acc)
        o_ref[0, h] = acc


def bias_tiles(rel_table, bases, R, C, cstep):
    H = rel_table.shape[1]
    T = bases.shape[0]
    grid_spec = pltpu.PrefetchScalarGridSpec(
        num_scalar_prefetch=1,
        grid=(T,),
        in_specs=[pl.BlockSpec(memory_space=pltpu.SMEM)],
        out_specs=pl.BlockSpec((1, H, R, C), lambda t, base: (t, 0, 0, 0)),
    )
    return pl.pallas_call(
        functools.partial(_bias_kernel, cstep=cstep),
        grid_spec=grid_spec,
        out_shape=jax.ShapeDtypeStruct((T, H, R, C), F32),
        compiler_params=_cparams(("arbitrary",)),
        name="bias_tiles",
    )(bases, rel_table.T)


def _masked_softmax(s, mask):
    s = jnp.where(mask, s, NEG)
    m = jnp.max(s, axis=-1, keepdims=True)
    p = jnp.where(mask, jnp.exp(s - m), 0.0)
    return p / jnp.maximum(jnp.sum(p, axis=-1, keepdims=True), 1e-30)


def _softmax_update(s, mask, m, l, acc, v):
    s = jnp.where(mask, s, NEG)
    m_new = jnp.maximum(m, jnp.max(s, axis=-1, keepdims=True))
    alpha = jnp.exp(m - m_new)
    p = jnp.where(mask, jnp.exp(s - m_new), 0.0)
    l = alpha * l + jnp.sum(p, axis=-1, keepdims=True)
    acc = alpha * acc + _dot(p.astype(BF16), v)
    return m_new, l, acc


def _softmax_init(R, W):
    return jnp.full((R, 1), NEG, F32), jnp.zeros((R, 1), F32), jnp.zeros((R, W), F32)


def _softmax_finish(l, acc):
    return acc / jnp.maximum(l, 1e-30)


def _top_blocks(score, n_real, n_top):
    jidx = _iota(score.shape, 1)
    rank = jnp.zeros(score.shape, F32)
    for i in range(n_real):
        si = score[:, i:i + 1]
        beats = (si > score) | ((si == score) & (jidx > i))
        rank = rank + jnp.where(beats, 1.0, 0.0)
    return jnp.where(rank < n_top, 1.0, 0.0)


def _block_scores(imp, q_pos, n_sel):
    j = _iota(imp.shape, 1)
    cur = q_pos >> _log2(SLC_BLOCK)
    valid = j * SLC_BLOCK <= q_pos
    forced = valid & ((j == 0) | (j == cur) | (j == cur - 1))
    score = jnp.where(forced, FORCE, jnp.where(valid, imp, -FORCE))
    return jnp.where(j < n_sel, score, -3e38)


def _block_of_key_01(n_blocks, tk, pos0):
    j = _iota((n_blocks, tk), 0)
    c = _iota((n_blocks, tk), 1)
    return jnp.where(((pos0 + c) >> _log2(SLC_BLOCK)) == j, 1.0, 0.0).astype(BF16)


def _half_to(x, src_high, dst_high):
    return x if src_high == dst_high else pltpu.roll(x, HEAD_DIM, 1)


def _nsa_prompt_kernel(q_ref, g_ref, kc_ref, vc_ref, ks_ref, vs_ref, kw_ref, vw_ref, bd_ref, bc_ref, cov_ref,
                       o_ref, *, tq, n_cmp, n_sel, n_top, group):
    i = pl.program_id(2)
    G = group
    R = G * tq
    low = _iota((tq, LANES), 1) < HEAD_DIM
    row = _iota((R, tq), 0) & (tq - 1)
    col = _iota((R, tq), 1)
    n_win_tiles = WINDOW // tq
    for hh in range(2):
        own = low if hh == 0 else jnp.logical_not(low)
        pieces = []
        for g in range(G):
            x = q_ref[0, :, (hh * 2 + g // 2) * LANES:(hh * 2 + g // 2 + 1) * LANES] * (HEAD_DIM ** -0.5)
            pieces.append(jnp.where(own, _half_to(x, g % 2 == 1, hh == 1), 0.0))
        qs = jnp.concatenate(pieces, axis=0).astype(BF16)

        def head_bias(ref, t):
            return ref[t, hh * G:(hh + 1) * G].reshape(R, tq)

        s = _dot_nt(qs, kc_ref[0].astype(BF16)) + head_bias(bc_ref, 0)
        dist_c = i * tq + row - (col * CMP_STRIDE + (CMP_BLOCK - 1))
        p_c = _masked_softmax(s, (dist_c >= 0) & (col < n_cmp)).astype(BF16)
        o_c = _dot(p_c, vc_ref[0].astype(BF16))
        imp = _dot(p_c[0:tq], cov_ref[...])
        for g in range(1, G):
            imp = imp + _dot(p_c[g * tq:(g + 1) * tq], cov_ref[...])
        q_pos = i * tq + _iota((tq, LANES), 0)
        sel = _top_blocks(_block_scores(imp, q_pos, n_sel), n_sel, n_top).astype(BF16)

        def slc_tile(j, carry, diag):
            start = pl.multiple_of(j * tq, tq)
            kt = ks_ref[0, pl.ds(start, tq), :].astype(BF16)
            vt = vs_ref[0, pl.ds(start, tq), :].astype(BF16)
            s = _dot_nt(qs, kt) + head_bias(bd_ref, i - j)
            chosen = _dot(sel, _block_of_key_01(LANES, tq, j * tq))
            mask = jnp.concatenate([chosen] * G, axis=0) > 0.5
            if diag:
                mask = mask & (col <= row)
            return _softmax_update(s, mask, *carry, vt)

        carry = lax.fori_loop(0, i, lambda j, c: slc_tile(j, c, False), _softmax_init(R, LANES))
        _, l, acc = slc_tile(i, carry, True)
        o_s = _softmax_finish(l, acc)

        def win_tile(dd, carry):
            start = pl.multiple_of((i - dd) * tq, tq)
            kt = kw_ref[0, pl.ds(start, tq), :].astype(BF16)
            vt = vw_ref[0, pl.ds(start, tq), :].astype(BF16)
            s = _dot_nt(qs, kt) + head_bias(bd_ref, dd)
            dist = dd * tq + row - col
            return _softmax_update(s, (dist >= 0) & (dist < WINDOW), *carry, vt)

        _, l, acc = lax.fori_loop(0, jnp.minimum(i, n_win_tiles) + 1, win_tile, _softmax_init(R, LANES))
        o_w = _softmax_finish(l, acc)

        def gate_col(br):
            return jnp.concatenate([g_ref[0, 0, :, hh * 3 * G + g * 3 + br:hh * 3 * G + g * 3 + br + 1]
                                    for g in range(G)], axis=0)

        o = gate_col(0) * o_c + gate_col(1) * o_s + gate_col(2) * o_w
        for vc in range(G // 2):
            lo_piece = _half_to(o[(2 * vc) * tq:(2 * vc + 1) * tq], hh == 1, False)
            hi_piece = _half_to(o[(2 * vc + 1) * tq:(2 * vc + 2) * tq], hh == 1, True)
            o_ref[0, :, (hh * 2 + vc) * LANES:(hh * 2 + vc + 1) * LANES] = jnp.where(low, lo_piece, hi_piece)


def nsa_prompt(q, gates, kcmp, vcmp, ks, vs, kw, vw, bias_dense, bias_cmp, cover, n_cmp, n_sel):
    B, L, WQ = q.shape
    WK = ks.shape[2]
    group = WQ // WK
    tq = 128
    n_pairs = WK // LANES
    n_top = min(SLC_TOPN, n_sel)
    qw = WQ // n_pairs
    hp = bias_dense.shape[1] // n_pairs
    kv = pl.BlockSpec((1, L, LANES), lambda b, p, i: (b, 0, p))
    cm = pl.BlockSpec((1, kcmp.shape[1], LANES), lambda b, p, i: (b, 0, p))
    return pl.pallas_call(
        functools.partial(_nsa_prompt_kernel, tq=tq, n_cmp=n_cmp, n_sel=n_sel, n_top=n_top, group=group),
        grid=(B, n_pairs, L // tq),
        in_specs=[
            pl.BlockSpec((1, tq, qw), lambda b, p, i: (b, i, p)),
            pl.BlockSpec((1, 1, tq, LANES), lambda b, p, i: (p, b, i, 0)),
            cm, cm, kv, kv, kv, kv,
            pl.BlockSpec((bias_dense.shape[0], hp, tq, tq), lambda b, p, i: (0, p, 0, 0)),
            pl.BlockSpec((1, hp, tq, tq), lambda b, p, i: (i, p, 0, 0)),
            _const_spec(cover.shape),
        ],
        out_specs=pl.BlockSpec((1, tq, qw), lambda b, p, i: (b, i, p)),
        out_shape=jax.ShapeDtypeStruct(q.shape, F32),
        compiler_params=_cparams(("parallel", "parallel", "arbitrary")),
        name="nsa_prompt",
    )(q, gates, kcmp, vcmp, ks, vs, kw, vw, bias_dense, bias_cmp, cover)


def _nsa_sample_kernel(q_ref, g_ref, kc_ref, vc_ref, ks_ref, vs_ref, ksn_ref, vsn_ref, kwp_ref, vwp_ref,
                       kwn_ref, vwn_ref, bs_ref, bn_ref, bc_ref, bw_ref, cov_ref, o_ref,
                       qs_sc, sel_sc, oc_sc, m_sc, l_sc, acc_sc,
                       *, n_tok, q_base, n_cmp, n_sel, n_top, group, tk):
    ci = pl.program_id(1)
    R, W = qs_sc.shape
    n_heads = R // n_tok
    J = sel_sc.shape[1]
    chunk = ks_ref.shape[1]

    def tok(shape):
        return _iota(shape, 0) & (n_tok - 1)

    @pl.when(ci == 0)
    def _():
        blocks = []
        for h in range(n_heads):
            hkv = h // group
            x = q_ref[0, :, (h // 2) * LANES:(h // 2 + 1) * LANES] * (HEAD_DIM ** -0.5)
            x = _half_to(x, h % 2 == 1, hkv % 2 == 1)
            keep = (_iota(x.shape, 1) >= HEAD_DIM) if hkv % 2 == 1 else (_iota(x.shape, 1) < HEAD_DIM)
            x = jnp.where(keep, x, 0.0)
            blocks.append(jnp.concatenate(
                [x if cc == hkv // 2 else jnp.zeros_like(x) for cc in range(W // LANES)], axis=1))
        qs = jnp.concatenate(blocks, axis=0).astype(BF16)
        qs_sc[...] = qs
        s = _dot_nt(qs, kc_ref[0].astype(BF16)) + bc_ref[...]
        n = _iota(s.shape, 1)
        dist_c = q_base + tok(s.shape) - (n * CMP_STRIDE + (CMP_BLOCK - 1))
        p_c = _masked_softmax(s, (dist_c >= 0) & (n < n_cmp)).astype(BF16)
        oc_sc[...] = _dot(p_c, vc_ref[0].astype(BF16))
        ic = _dot(p_c, cov_ref[...])
        rows = group * n_tok
        imp_blocks = []
        for hkv in range(n_heads // group):
            a = ic[hkv * rows:hkv * rows + n_tok]
            for g in range(1, group):
                a = a + ic[hkv * rows + g * n_tok:hkv * rows + (g + 1) * n_tok]
            imp_blocks += [a] * group
        imp = jnp.concatenate(imp_blocks, axis=0)
        sel_sc[...] = _top_blocks(_block_scores(imp, q_base + tok(imp.shape), n_sel), n_sel, n_top)
        m0, l0, a0 = _softmax_init(R, W)
        m_sc[...] = m0
        l_sc[...] = l0
        acc_sc[...] = a0

    qs = qs_sc[...]
    sel = sel_sc[...].astype(BF16)

    def tile(u, carry):
        start = pl.multiple_of(u * tk, tk)
        kt = ks_ref[0, pl.ds(start, tk), :].astype(BF16)
        vt = vs_ref[0, pl.ds(start, tk), :].astype(BF16)
        s = _dot_nt(qs, kt) + bs_ref[:, pl.ds(start, tk)]
        chosen = _dot(sel, _block_of_key_01(J, tk, ci * chunk + u * tk))
        return _softmax_update(s, chosen > 0.5, *carry, vt)

    m, l, acc = lax.fori_loop(0, chunk // tk, tile, (m_sc[...], l_sc[...], acc_sc[...]))
    m_sc[...] = m
    l_sc[...] = l
    acc_sc[...] = acc

    @pl.when(ci == pl.num_programs(1) - 1)
    def _():
        tn = ksn_ref.shape[1]
        c = _iota((R, tn), 1)
        causal_new = c <= tok((R, tn))
        s = _dot_nt(qs, ksn_ref[0].astype(BF16)) + bn_ref[...]
        chosen = _dot(sel, _block_of_key_01(J, tn, q_base))
        _, l1, a1 = _softmax_update(s, (chosen > 0.5) & causal_new, m, l, acc, vsn_ref[0].astype(BF16))
        o_s = _softmax_finish(l1, a1)
        s = _dot_nt(qs, kwp_ref[0].astype(BF16)) + bw_ref[...]
        dist = WINDOW + tok(s.shape) - _iota(s.shape, 1)
        cw = _softmax_update(s, (dist >= 0) & (dist < WINDOW), *_softmax_init(R, W), vwp_ref[0].astype(BF16))
        s = _dot_nt(qs, kwn_ref[0].astype(BF16)) + bn_ref[...]
        _, l2, a2 = _softmax_update(s, causal_new, *cw, vwn_ref[0].astype(BF16))
        o_w = _softmax_finish(l2, a2)

        def gate_col(br):
            cols = []
            for h in range(n_heads):
                hkv, g = divmod(h, group)
                cidx = (hkv % 2) * 3 * group + g * 3 + br
                cols.append(g_ref[hkv // 2, 0, :, cidx:cidx + 1])
            return jnp.concatenate(cols, axis=0)

        o = gate_col(0) * oc_sc[...] + gate_col(1) * o_s + gate_col(2) * o_w
        low = _iota((n_tok, LANES), 1) < HEAD_DIM
        for oc in range(n_heads // 2):
            hkv = (2 * oc) // group
            src = slice((hkv // 2) * LANES, (hkv // 2 + 1) * LANES)
            lo_piece = _half_to(o[(2 * oc) * n_tok:(2 * oc + 1) * n_tok, src], hkv % 2 == 1, False)
            hi_piece = _half_to(o[(2 * oc + 1) * n_tok:(2 * oc + 2) * n_tok, src], hkv % 2 == 1, True)
            o_ref[0, :, oc * LANES:(oc + 1) * LANES] = jnp.where(low, lo_piece, hi_piece)


def nsa_sample(q, gates, kcmp, vcmp, ks_past, vs_past, ks_new, vs_new, kw_past, vw_past, kw_new, vw_new,
               bias_slc, bias_new, bias_cmp, bias_win, cover, n_cmp, n_sel):
    B, n_tok, WQ = q.shape
    WK = ks_past.shape[2]
    past = ks_past.shape[1]
    group = WQ // WK
    n_heads = WQ // HEAD_DIM
    R = n_heads * n_tok
    J = cover.shape[1]
    chunk = 2048
    tk = LANES

    def bspec(a):
        return pl.BlockSpec((1,) + a.shape[1:], lambda b, c: (b,) + (0,) * (a.ndim - 1))

    kvs = pl.BlockSpec((1, chunk, WK), lambda b, c: (b, c, 0))
    return pl.pallas_call(
        functools.partial(_nsa_sample_kernel, n_tok=n_tok, q_base=past, n_cmp=n_cmp, n_sel=n_sel,
                          n_top=min(SLC_TOPN, n_sel), group=group, tk=tk),
        grid=(B, past // chunk),
        in_specs=[
            bspec(q),
            pl.BlockSpec((gates.shape[0], 1, n_tok, LANES), lambda b, c: (0, b, 0, 0)),
            bspec(kcmp), bspec(vcmp), kvs, kvs, bspec(ks_new), bspec(vs_new),
            bspec(kw_past), bspec(vw_past), bspec(kw_new), bspec(vw_new),
            pl.BlockSpec((R, chunk), lambda b, c: (0, c)),
            _const_spec(bias_new.shape), _const_spec(bias_cmp.shape), _const_spec(bias_win.shape),
            _const_spec(cover.shape),
        ],
        out_specs=bspec(q),
        out_shape=jax.ShapeDtypeStruct(q.shape, F32),
        scratch_shapes=[pltpu.VMEM((R, WK), BF16), pltpu.VMEM((R, J), F32), pltpu.VMEM((R, WK), F32),
                        pltpu.VMEM((R, 1), F32), pltpu.VMEM((R, 1), F32), pltpu.VMEM((R, WK), F32)],
        compiler_params=_cparams(("parallel", "arbitrary")),
        name="nsa_sample",
    )(q, gates, kcmp, vcmp, ks_past, vs_past, ks_new, vs_new, kw_past, vw_past, kw_new, vw_new,
      bias_slc, bias_new, bias_cmp, bias_win, cover)


def _block_diag_ones(width):
    h = jnp.arange(width) // HEAD_DIM
    return (h[:, None] == h[None, :]).astype(BF16)


def _cover_01(n_cmp, n_sel, rows, cols):
    ci = jnp.arange(rows) * CMP_STRIDE
    sj = jnp.arange(cols) * SLC_BLOCK
    hit = (ci[:, None] < sj[None, :] + SLC_BLOCK) & (ci[:, None] + CMP_BLOCK > sj[None, :])
    hit = hit & (jnp.arange(rows)[:, None] < n_cmp) & (jnp.arange(cols)[None, :] < n_sel)
    return hit.astype(BF16)


def _round_up(n, m):
    return -(-n // m) * m


def _compress_weights(w1, b1, w2, b2, n_kv):
    hidden = w1.shape[1]
    eye = jnp.eye(n_kv, dtype=F32)
    w1r = w1.reshape(CMP_BLOCK // CMP_STRIDE, CMP_STRIDE, HEAD_DIM, hidden)
    w1e = jnp.einsum('mrde,hk->mrhdke', w1r, eye).reshape(
        CMP_BLOCK // CMP_STRIDE, CMP_STRIDE, n_kv * HEAD_DIM, n_kv * hidden).astype(BF16)
    w2e = jnp.einsum('ed,hk->hekd', w2, eye).reshape(n_kv * hidden, n_kv * HEAD_DIM).astype(BF16)
    return w1e, jnp.tile(b1, n_kv)[None, :], w2e, jnp.tile(b2, n_kv)[None, :]


def _trunk(x, mods, past, P):
    B, L, D = x.shape
    depth = P['norm_g'].shape[0]
    even_states, odd_states = [], []
    for li in range(depth):
        def m(k):
            return mods[li, :, k][:, None, :]

        def g(k):
            return P['norm_g'][li, k][None, :]

        x = ffn(x, m(0), m(1), m(2), g(0), P['ffn_w_in'][li, 0], P['ffn_w_out'][li, 0])
        if li % 2 == 0:
            e = li // 2
            w = P['even_w_in'][e]
            n_sb = P['sb_width']
            HK, HV = P['gla_hk'], P['gla_hv']
            offs = [0, n_sb, 2 * n_sb, 3 * n_sb, 3 * n_sb + HK, 3 * n_sb + 2 * HK, 3 * n_sb + 2 * HK + HV,
                    3 * n_sb + 2 * HK + 2 * HV]
            ws = [w[:, offs[k]:offs[k + 1]] for k in range(7)]
            rank = w.shape[1] - offs[7]
            wa1 = jnp.pad(w[:, offs[7]:], ((0, 0), (0, LANES - rank)))
            wa2 = jnp.pad(P['gla_w_a2'][e], ((0, LANES - rank), (0, 0)))
            qa, ka, va, qb, kb, vb, rb, la = even_proj(x, m(3), m(4), g(1), ws, wa1, wa2, P['gla_b_a2'][e][None, :])
            dv = P['gla_onorm_g'].shape[1]
            n_gh = HV // dv
            dk = HK // n_gh
            if past is None:
                o_a = sb_prompt(qa, ka, va)
                s0t = jnp.zeros((B, dv, HK), F32)
            else:
                o_a = sb_sample(qa, ka, va, past['sb_k'][e], past['sb_v'][e], past['page_table'])
                s0t = past['gla'][e].transpose(0, 3, 1, 2).reshape(B, dv, HK)
            o_b, st = gla(qb, kb, vb, rb, la, s0t, P['gla_onorm_g'][e][None, :])
            wo = P['even_w_out'][e]
            x = out_proj(x, m(5), [o_a, o_b], [wo[:n_sb], wo[n_sb:]])
            s_fin = st.reshape(B, dv, n_gh, dk).transpose(0, 2, 3, 1)
            even_states.append((ka.reshape(B, L, -1, HEAD_DIM), va.reshape(B, L, -1, HEAD_DIM), s_fin))
        else:
            o_i = li // 2
            w = P['odd_w_in'][o_i]
            WQ, WK = P['nsa_wq'], P['nsa_wk']
            n_kv = WK // HEAD_DIM
            group = WQ // WK
            offs = [0, WQ] + [WQ + (k + 1) * WK for k in range(6)]
            ws = [w[:, offs[k]:offs[k + 1]] for k in range(7)]
            wg = w[:, offs[7]:]
            half = wg.shape[1] // 2
            ws += [jnp.pad(wg[:, :half], ((0, 0), (0, LANES - half))),
                   jnp.pad(wg[:, half:], ((0, 0), (0, LANES - half)))]
            qk_g = P['nsa_qk_g'][o_i]
            bdq, bdk = _block_diag_ones(WQ), _block_diag_ones(WK)
            consts = [bdq, bdk, jnp.tile(qk_g[0], WQ // HEAD_DIM)[None, :],
                      jnp.tile(qk_g[2], n_kv)[None, :], jnp.tile(qk_g[3], n_kv)[None, :]]
            q, kc, vc, ks, vs, kw, vw, gates = odd_proj(x, m(3), m(4), g(1), ws, consts)
            pair = LANES // HEAD_DIM
            gk = jnp.tile(qk_g[1], pair)[None, :]
            bdp = _block_diag_ones(LANES)
            cw = [_compress_weights(P['cmp_w1'][o_i, t], P['cmp_b1'][o_i, t], P['cmp_w2'][o_i, t],
                                    P['cmp_b2'][o_i, t], pair) for t in range(2)]
            rel = P['rel_table']
            if past is None:
                n_rows = (L // CMP_STRIDE) * CMP_STRIDE
                assert n_rows == L and L % LANES == 0
                n_cmp = L // CMP_STRIDE - CMP_BLOCK // CMP_STRIDE + 1
                n_sel = -(-L // SLC_BLOCK)
                kcmp = compress(kc, n_rows, *cw[0], bdp, gk, True)
                vcmp = compress(vc, n_rows, *cw[1], bdp, gk, False)
                assert kcmp.shape[1] == LANES and n_sel <= LANES
                nq = L // LANES
                bias_dense = bias_tiles(rel, jnp.arange(nq, dtype=jnp.int32) * LANES, LANES, LANES, 1)
                bias_cmp = bias_tiles(rel, jnp.arange(nq, dtype=jnp.int32) * LANES - (CMP_BLOCK - 1),
                                      LANES, LANES, CMP_STRIDE)
                cover = _cover_01(n_cmp, n_sel, LANES, LANES)
                o = nsa_prompt(q, gates, kcmp, vcmp, ks, vs, kw, vw, bias_dense, bias_cmp, cover, n_cmp, n_sel)
                w_keep = min(WINDOW, L)
                win_k, win_v = kw[:, L - w_keep:], vw[:, L - w_keep:]
            else:
                pt = past['page_table']
                past_len = pt.shape[1] * past['cmp_k'].shape[2]
                T = past_len + L
                n_rows = (T // CMP_STRIDE) * CMP_STRIDE
                assert n_rows == past_len and past['win_k'].shape[2] == WINDOW and L <= LANES
                n_cmp = n_rows // CMP_STRIDE - CMP_BLOCK // CMP_STRIDE + 1
                n_sel = -(-T // SLC_BLOCK)
                kc_all = gather_pages(past['cmp_k'][o_i], pt)
                vc_all = gather_pages(past['cmp_v'][o_i], pt)
                ks_all = gather_pages(past['slc_k'][o_i], pt)
                vs_all = gather_pages(past['slc_v'][o_i], pt)
                kcmp = compress(kc_all, n_rows, *cw[0], bdp, gk, True)
                vcmp = compress(vc_all, n_rows, *cw[1], bdp, gk, False)
                J = _round_up(n_sel, LANES)
                n_heads = WQ // HEAD_DIM
                R = n_heads * L

                def sample_bias(base, C, cstep):
                    return bias_tiles(rel, jnp.array([base], jnp.int32), L, C, cstep).reshape(R, C)

                bias_slc = sample_bias(past_len, past_len, 1)
                bias_new = sample_bias(0, LANES, 1)
                bias_cmp = sample_bias(past_len - (CMP_BLOCK - 1), kcmp.shape[1], CMP_STRIDE)
                bias_win = sample_bias(WINDOW, WINDOW, 1)
                cover = _cover_01(n_cmp, n_sel, kcmp.shape[1], J)
                pad = ((0, 0), (0, LANES - L), (0, 0))
                o = nsa_sample(q, gates, kcmp, vcmp, ks_all, vs_all, jnp.pad(ks, pad), jnp.pad(vs, pad),
                               past['win_k'][o_i], past['win_v'][o_i], jnp.pad(kw, pad), jnp.pad(vw, pad),
                               bias_slc, bias_new, bias_cmp, bias_win, cover, n_cmp, n_sel)
                win_k = jnp.concatenate([past['win_k'][o_i], kw], axis=1)[:, L:]
                win_v = jnp.concatenate([past['win_v'][o_i], vw], axis=1)[:, L:]
            x = out_proj(x, m(5), [o], [P['odd_w_out'][o_i]])

            def rows(t):
                return t.reshape(B, t.shape[1], n_kv, HEAD_DIM)

            odd_states.append(tuple(rows(t) for t in (kc, vc, ks, vs, win_k, win_v)))
        x = ffn(x, m(6), m(7), m(8), g(2), P['ffn_w_in'][li, 1], P['ffn_w_out'][li, 1])
    even_new = [jnp.stack([s[i] for s in even_states]) for i in range(3)]
    odd_new = [jnp.stack([s[i] for s in odd_states]) for i in range(6)]
    return x, even_new, odd_new


def kernel(x_prompt, x_sample, cache_sb_k, cache_sb_v, state_gla, cache_cmp_k, cache_cmp_v, cache_slc_k,
           cache_slc_v, cache_win_k, cache_win_v, page_table, c_prompt, c_sample, norm_g, ada_w, ada_b,
           ffn_w_in, ffn_w_out, even_w_in, gla_w_a2, gla_b_a2, gla_onorm_g, even_w_out, odd_w_in, nsa_qk_g,
           cmp_w1, cmp_b1, cmp_w2, cmp_b2, rel_table, odd_w_out):
    D = x_prompt.shape[-1]
    depth = norm_g.shape[0]
    Bp, Bs = x_prompt.shape[0], x_sample.shape[0]
    mods = ada_mod(jnp.concatenate([c_prompt, c_sample], axis=0), ada_w, ada_b).reshape(depth, Bp + Bs, 9, D)
    n_sb = cache_sb_k.shape[3] * cache_sb_k.shape[4]
    dv = gla_onorm_g.shape[1]
    HK = gla_w_a2.shape[2]
    n_gh = state_gla.shape[2]
    WK = cache_cmp_k.shape[3] * cache_cmp_k.shape[4]
    P = {
        'norm_g': norm_g, 'ffn_w_in': ffn_w_in.astype(BF16), 'ffn_w_out': ffn_w_out.astype(BF16),
        'even_w_in': even_w_in.astype(BF16), 'gla_w_a2': gla_w_a2.astype(BF16), 'gla_b_a2': gla_b_a2,
        'gla_onorm_g': gla_onorm_g, 'even_w_out': even_w_out.astype(BF16), 'odd_w_in': odd_w_in.astype(BF16),
        'nsa_qk_g': nsa_qk_g, 'cmp_w1': cmp_w1, 'cmp_b1': cmp_b1, 'cmp_w2': cmp_w2, 'cmp_b2': cmp_b2,
        'rel_table': rel_table, 'odd_w_out': odd_w_out.astype(BF16),
        'sb_width': n_sb, 'gla_hk': HK, 'gla_hv': n_gh * dv, 'nsa_wq': odd_w_out.shape[1], 'nsa_wk': WK,
    }

    def pool(c):
        return c.reshape(c.shape[0], c.shape[1], c.shape[2], c.shape[3] * c.shape[4])

    past = {
        'page_table': page_table, 'sb_k': pool(cache_sb_k), 'sb_v': pool(cache_sb_v), 'gla': state_gla,
        'cmp_k': pool(cache_cmp_k), 'cmp_v': pool(cache_cmp_v), 'slc_k': pool(cache_slc_k),
        'slc_v': pool(cache_slc_v), 'win_k': pool(cache_win_k), 'win_v': pool(cache_win_v),
    }
    y_p, p_even, p_odd = _trunk(x_prompt, mods[:, :Bp], None, P)
    y_s, s_even, s_odd = _trunk(x_sample, mods[:, Bp:], past, P)
    return (y_p, y_s, *p_even, *p_odd, *s_even, *s_odd)
```

```python
import functools
import math

import jax
import jax.numpy as jnp
from jax import lax
from jax.experimental import pallas as pl
from jax.experimental.pallas import tpu as pltpu

F32 = jnp.float32
BF16 = jnp.bfloat16

EPS = 1e-6
NEG = -1e30
FORCE = 1e6
HEAD_DIM = 64
GLA_TAU = 16.0
GLA_CHUNK = 64
CMP_BLOCK = 32
CMP_STRIDE = 16
SLC_BLOCK = 64
SLC_TOPN = 16
WINDOW = 512
REL_BUCKETS = 32
REL_MAX_DIST = 1024
PAGE_GROUP = 16
LANES = 128
VMEM_LIMIT = 56 * 1024 * 1024


def _cparams(sem, vmem=VMEM_LIMIT):
    return pltpu.CompilerParams(dimension_semantics=sem, vmem_limit_bytes=vmem)


def _dot(a, b):
    return jnp.dot(a, b, preferred_element_type=F32)


def _dot_nt(a, b):
    return lax.dot_general(a, b, (((1,), (1,)), ((), ())), preferred_element_type=F32)


def _dot_tn(a, b):
    return lax.dot_general(a, b, (((0,), (0,)), ((), ())), preferred_element_type=F32)


def _split2(x):
    hi = x.astype(BF16)
    lo = (x - hi.astype(F32)).astype(BF16)
    return hi, lo


def _dot_f32_by01(x, m01):
    hi, lo = _split2(x)
    return _dot(hi, m01) + _dot(lo, m01)


def _dot_01_by_f32(m01, x):
    hi, lo = _split2(x)
    return _dot(m01, hi) + _dot(m01, lo)


def _softplus(z):
    return jnp.maximum(z, 0.0) + jnp.log1p(jnp.exp(-jnp.abs(z)))


def _silu(a):
    return a * jax.nn.sigmoid(a)


def _iota(shape, dim):
    return lax.broadcasted_iota(jnp.int32, shape, dim)


def _modulate(x, g, shift, scale):
    ms = jnp.mean(x * x, axis=-1, keepdims=True)
    y = x * lax.rsqrt(ms + EPS) * g
    return y * (1.0 + scale) + shift


def _head_rms(x, bd, g):
    ms = _dot_f32_by01(x * x, bd) * (1.0 / HEAD_DIM)
    return x * lax.rsqrt(ms + EPS) * g


def _const_spec(shape):
    nd = len(shape)
    return pl.BlockSpec(shape, lambda *_: (0,) * nd, pipeline_mode=pl.Buffered(1))


def _row_tiles(B, L):
    if L >= 512:
        return 1, 512
    return B, L


def _ada_kernel(c_ref, w_ref, b_ref, o_ref):
    c = c_ref[...]
    a = _silu(c).astype(BF16)
    o_ref[0] = _dot(a, w_ref[0].astype(BF16)) + b_ref[0]


def ada_mod(c, ada_w, ada_b):
    R, D = c.shape
    depth, _, N = ada_w.shape
    tn = 1024
    return pl.pallas_call(
        _ada_kernel,
        grid=(depth, N // tn),
        in_specs=[
            pl.BlockSpec((R, D), lambda l, n: (0, 0)),
            pl.BlockSpec((1, D, tn), lambda l, n: (l, 0, n)),
            pl.BlockSpec((1, 1, tn), lambda l, n: (l, 0, n)),
        ],
        out_specs=pl.BlockSpec((1, R, tn), lambda l, n: (l, 0, n)),
        out_shape=jax.ShapeDtypeStruct((depth, R, N), F32),
        compiler_params=_cparams(("parallel", "parallel")),
        name="ada_mod",
    )(c, ada_w, ada_b.reshape(depth, 1, N))


def _ffn_kernel(x_ref, sh_ref, sc_ref, gt_ref, g_ref, win_ref, wout_ref, o_ref, *, d_ff, tf):
    x = x_ref[...]
    tb, tl, D = x.shape
    h = _modulate(x, g_ref[...], sh_ref[...], sc_ref[...])
    hb = h.reshape(tb * tl, D).astype(BF16)
    acc = jnp.zeros((tb * tl, D), F32)
    for j in range(d_ff // tf):
        a = _dot(hb, win_ref[:, j * tf:(j + 1) * tf])
        b = _dot(hb, win_ref[:, d_ff + j * tf:d_ff + (j + 1) * tf])
        u = (_silu(a) * b).astype(BF16)
        acc = acc + _dot(u, wout_ref[j * tf:(j + 1) * tf, :])
    o_ref[...] = x + 0.5 * gt_ref[...] * acc.reshape(tb, tl, D)


def ffn(x, shift, scale, gate, g, w_in, w_out):
    B, L, D = x.shape
    d_ff = w_out.shape[0]
    tb, tl = _row_tiles(B, L)
    xs = pl.BlockSpec((tb, tl, D), lambda b, l: (b, l, 0))
    ms = pl.BlockSpec((tb, 1, D), lambda b, l: (b, 0, 0))
    return pl.pallas_call(
        functools.partial(_ffn_kernel, d_ff=d_ff, tf=256),
        grid=(B // tb, L // tl),
        in_specs=[xs, ms, ms, ms, _const_spec((1, D)), _const_spec(w_in.shape), _const_spec(w_out.shape)],
        out_specs=xs,
        out_shape=jax.ShapeDtypeStruct(x.shape, F32),
        compiler_params=_cparams(("parallel", "parallel")),
        name="ffn",
    )(x, shift, scale, gate, g, w_in, w_out)


def _even_proj_kernel(x_ref, sh_ref, sc_ref, g_ref, wqa, wka, wva, wqb, wkb, wvb, wrb, wa1, wa2, ba2,
                      qa, ka, va, qb, kb, vb, rb, la):
    x = x_ref[...]
    tb, tl, D = x.shape
    h = _modulate(x, g_ref[...], sh_ref[...], sc_ref[...])
    hb = h.reshape(tb * tl, D).astype(BF16)
    for w, o in ((wqa, qa), (wka, ka), (wva, va), (wqb, qb), (wkb, kb), (wvb, vb), (wrb, rb)):
        o[...] = _dot(hb, w[...]).reshape(o.shape)
    a1 = _dot(hb, wa1[...]).astype(BF16)
    z = _dot(a1, wa2[...]) + ba2[...]
    la[...] = (-_softplus(-z) * (1.0 / GLA_TAU)).reshape(la.shape)


def even_proj(x, shift, scale, g, ws, wa1, wa2, ba2):
    B, L, D = x.shape
    tb, tl = _row_tiles(B, L)
    xs = pl.BlockSpec((tb, tl, D), lambda b, l: (b, l, 0))
    ms = pl.BlockSpec((tb, 1, D), lambda b, l: (b, 0, 0))
    widths = [w.shape[1] for w in ws] + [wa2.shape[1]]
    return pl.pallas_call(
        _even_proj_kernel,
        grid=(B // tb, L // tl),
        in_specs=[xs, ms, ms, _const_spec((1, D))] + [_const_spec(w.shape) for w in ws]
        + [_const_spec(wa1.shape), _const_spec(wa2.shape), _const_spec(ba2.shape)],
        out_specs=[pl.BlockSpec((tb, tl, n), lambda b, l: (b, l, 0)) for n in widths],
        out_shape=[jax.ShapeDtypeStruct((B, L, n), F32) for n in widths],
        compiler_params=_cparams(("parallel", "parallel")),
        name="even_proj",
    )(x, shift, scale, g, *ws, wa1, wa2, ba2)


def _odd_proj_kernel(x_ref, sh_ref, sc_ref, g_ref, wq, wkc, wvc, wks, wvs, wkw, wvw, wg0, wg1,
                     bdq, bdk, gq, gks, gkw,
                     q, kc, vc, ks, vs, kw, vw, gates):
    x = x_ref[...]
    tb, tl, D = x.shape
    h = _modulate(x, g_ref[...], sh_ref[...], sc_ref[...])
    hb = h.reshape(tb * tl, D).astype(BF16)
    q[...] = _head_rms(_dot(hb, wq[...]), bdq[...], gq[...]).reshape(q.shape)
    kc[...] = _dot(hb, wkc[...]).reshape(kc.shape)
    vc[...] = _dot(hb, wvc[...]).reshape(vc.shape)
    ks[...] = _head_rms(_dot(hb, wks[...]), bdk[...], gks[...]).reshape(ks.shape)
    vs[...] = _dot(hb, wvs[...]).reshape(vs.shape)
    kw[...] = _head_rms(_dot(hb, wkw[...]), bdk[...], gkw[...]).reshape(kw.shape)
    vw[...] = _dot(hb, wvw[...]).reshape(vw.shape)
    gates[0] = jax.nn.sigmoid(_dot(hb, wg0[...])).reshape(gates.shape[1:])
    gates[1] = jax.nn.sigmoid(_dot(hb, wg1[...])).reshape(gates.shape[1:])


def odd_proj(x, shift, scale, g, ws, consts):
    B, L, D = x.shape
    tb, tl = _row_tiles(B, L)
    xs = pl.BlockSpec((tb, tl, D), lambda b, l: (b, l, 0))
    ms = pl.BlockSpec((tb, 1, D), lambda b, l: (b, 0, 0))
    widths = [w.shape[1] for w in ws[:7]]
    out_specs = [pl.BlockSpec((tb, tl, n), lambda b, l: (b, l, 0)) for n in widths]
    out_specs.append(pl.BlockSpec((2, tb, tl, LANES), lambda b, l: (0, b, l, 0)))
    out_shape = [jax.ShapeDtypeStruct((B, L, n), F32) for n in widths]
    out_shape.append(jax.ShapeDtypeStruct((2, B, L, LANES), F32))
    return pl.pallas_call(
        _odd_proj_kernel,
        grid=(B // tb, L // tl),
        in_specs=[xs, ms, ms, _const_spec((1, D))] + [_const_spec(w.shape) for w in ws]
        + [_const_spec(c.shape) for c in consts],
        out_specs=out_specs,
        out_shape=out_shape,
        compiler_params=_cparams(("parallel", "parallel")),
        name="odd_proj",
    )(x, shift, scale, g, *ws, *consts)


def _out_proj_kernel(*refs, n_in):
    x_ref, gt_ref = refs[0], refs[1]
    a_refs = refs[2:2 + n_in]
    w_refs = refs[2 + n_in:2 + 2 * n_in]
    o_ref = refs[2 + 2 * n_in]
    x = x_ref[...]
    tb, tl, D = x.shape
    acc = jnp.zeros((tb * tl, D), F32)
    for a, w in zip(a_refs, w_refs):
        av = a[...]
        acc = acc + _dot(av.reshape(tb * tl, av.shape[-1]).astype(BF16), w[...])
    o_ref[...] = x + gt_ref[...] * acc.reshape(tb, tl, D)


def out_proj(x, gate, acts, ws):
    B, L, D = x.shape
    tb, tl = _row_tiles(B, L)
    xs = pl.BlockSpec((tb, tl, D), lambda b, l: (b, l, 0))
    ms = pl.BlockSpec((tb, 1, D), lambda b, l: (b, 0, 0))
    return pl.pallas_call(
        functools.partial(_out_proj_kernel, n_in=len(acts)),
        grid=(B // tb, L // tl),
        in_specs=[xs, ms] + [pl.BlockSpec((tb, tl, a.shape[-1]), lambda b, l: (b, l, 0)) for a in acts]
        + [_const_spec(w.shape) for w in ws],
        out_specs=xs,
        out_shape=jax.ShapeDtypeStruct(x.shape, F32),
        compiler_params=_cparams(("parallel", "parallel")),
        name="out_proj",
    )(x, gate, *acts, *ws)


def _log2(n):
    assert n > 0 and n & (n - 1) == 0, n
    return n.bit_length() - 1


def _later_keys_01(tk):
    return jnp.where(_iota((tk, tk), 0) > _iota((tk, tk), 1), 1.0, 0.0).astype(BF16)


def _sb_tile(q2, kt, vt, run, acc, u01, before, feature_major=False):
    z = _dot(q2, kt) if feature_major else _dot_nt(q2, kt)
    sp = _softplus(z)
    lk = -sp
    ls = z - sp
    if before is not None:
        lk = jnp.where(before, lk, 0.0)
    cs = _dot_f32_by01(lk, u01)
    w = jnp.exp(ls + cs + run)
    if before is not None:
        w = jnp.where(before, w, 0.0)
    wb = w.astype(BF16)
    acc = acc + (_dot_nt(wb, vt) if feature_major else _dot(wb, vt))
    run = run + cs[:, 0:1] + lk[:, 0:1]
    return run, acc


def _sb_prompt_kernel(q_ref, k_ref, v_ref, o_ref, *, tq):
    i = pl.program_id(2)
    q = q_ref[0] * (HEAD_DIM ** -0.5)
    lane = _iota((tq, LANES), 1)
    q2 = jnp.concatenate([jnp.where(lane < HEAD_DIM, q, 0.0), jnp.where(lane >= HEAD_DIM, q, 0.0)],
                         axis=0).astype(BF16)
    u01 = _later_keys_01(tq)
    before = _iota((2 * tq, tq), 1) < (_iota((2 * tq, tq), 0) & (tq - 1))

    def tile(j, run, acc, mask):
        start = pl.multiple_of(j * tq, tq)
        kt = k_ref[0, pl.ds(start, tq), :].astype(BF16)
        vt = v_ref[0, pl.ds(start, tq), :].astype(BF16)
        return _sb_tile(q2, kt, vt, run, acc, u01, mask)

    run, acc = tile(i, jnp.zeros((2 * tq, 1), F32), jnp.zeros((2 * tq, LANES), F32), before)
    run, acc = lax.fori_loop(1, i + 1, lambda jj, c: tile(i - jj, c[0], c[1], None), (run, acc))
    o_ref[0] = jnp.where(lane < HEAD_DIM, acc[:tq], acc[tq:])


def sb_prompt(q, k, v):
    B, L, W = q.shape
    tq = 256
    qs = pl.BlockSpec((1, tq, LANES), lambda b, p, i: (b, i, p))
    ks = pl.BlockSpec((1, L, LANES), lambda b, p, i: (b, 0, p))
    return pl.pallas_call(
        functools.partial(_sb_prompt_kernel, tq=tq),
        grid=(B, W // LANES, L // tq),
        in_specs=[qs, ks, ks],
        out_specs=qs,
        out_shape=jax.ShapeDtypeStruct(q.shape, F32),
        compiler_params=_cparams(("parallel", "parallel", "arbitrary")),
        name="sb_prompt",
    )(q, k, v)


def _sb_sample_kernel(pt_ref, q_ref, kn_ref, vn_ref, *rest, n_tok, group):
    k_refs, v_refs = rest[:group], rest[group:2 * group]
    o_ref, run_ref, acc_ref = rest[2 * group:]
    s = pl.program_id(1)
    q = q_ref[0] * (HEAD_DIM ** -0.5)
    W = q.shape[1]
    n_heads = W // HEAD_DIM
    R = n_heads * n_tok
    tk = kn_ref.shape[1]
    own = (_iota((R, W), 0) >> _log2(n_tok)) == (_iota((R, W), 1) >> _log2(HEAD_DIM))
    q2 = jnp.where(own, jnp.concatenate([q] * n_heads, axis=0), 0.0).astype(BF16)
    u01 = _later_keys_01(tk)

    @pl.when(s == 0)
    def _():
        before = _iota((R, tk), 1) < (_iota((R, tk), 0) & (n_tok - 1))
        run, acc = _sb_tile(q2, kn_ref[0].astype(BF16), vn_ref[0].astype(BF16),
                            jnp.zeros((R, 1), F32), jnp.zeros((R, W), F32), u01, before)
        run_ref[...] = run
        acc_ref[...] = acc

    @pl.when(s > 0)
    def _():
        run, acc = run_ref[...], acc_ref[...]
        for kr, vr in zip(k_refs, v_refs):
            run, acc = _sb_tile(q2, kr[0, 0].astype(BF16), vr[0, 0].astype(BF16), run, acc, u01, None, True)
        run_ref[...] = run
        acc_ref[...] = acc

    @pl.when(s == pl.num_programs(1) - 1)
    def _():
        acc = jnp.where(own, acc_ref[...], 0.0)
        out = acc[0:n_tok]
        for h in range(1, n_heads):
            out = out + acc[h * n_tok:(h + 1) * n_tok]
        o_ref[0] = out


def sb_sample(q, k_new, v_new, pool_k, pool_v, layer, page_table):
    B, n_tok, W = q.shape
    page = pool_k.shape[3]
    n_pages = page_table.shape[1]
    group = PAGE_GROUP
    n_steps = 1 + n_pages // group
    pad = ((0, 0), (0, page - n_tok), (0, 0))
    kn, vn = jnp.pad(k_new, pad), jnp.pad(v_new, pad)

    def page_spec(kk):
        def imap(b, s, pt):
            return (layer, pt[b, n_pages - 1 - (jnp.maximum(s, 1) - 1) * group - kk], 0, 0)
        return pl.BlockSpec((1, 1, W, page), imap)

    qs = pl.BlockSpec((1, n_tok, W), lambda b, s, pt: (b, 0, 0))
    ns = pl.BlockSpec((1, page, W), lambda b, s, pt: (b, 0, 0))
    R = (W // HEAD_DIM) * n_tok
    grid_spec = pltpu.PrefetchScalarGridSpec(
        num_scalar_prefetch=1,
        grid=(B, n_steps),
        in_specs=[qs, ns, ns] + [page_spec(kk) for kk in range(group)] * 2,
        out_specs=qs,
        scratch_shapes=[pltpu.VMEM((R, 1), F32), pltpu.VMEM((R, W), F32)],
    )
    return pl.pallas_call(
        functools.partial(_sb_sample_kernel, n_tok=n_tok, group=group),
        grid_spec=grid_spec,
        out_shape=jax.ShapeDtypeStruct(q.shape, F32),
        compiler_params=_cparams(("parallel", "arbitrary")),
        name="sb_sample",
    )(page_table, q, kn, vn, *([pool_k] * group), *([pool_v] * group))


def _gla_kernel(q_ref, k_ref, v_ref, r_ref, la_ref, s0_ref, g_ref, o_ref, sfin_ref, st, *, dk, dv):
    c = pl.program_id(1)

    @pl.when(c == 0)
    def _():
        st[...] = s0_ref[0]

    q = q_ref[0] * (dk ** -0.5)
    k = k_ref[0]
    la = la_ref[0]
    C, HK = q.shape
    n_heads = HK // dk
    causal = _iota((C, C), 0) >= _iota((C, C), 1)
    ltri = jnp.where(causal, 1.0, 0.0).astype(BF16)
    hi = la.astype(BF16)
    r1 = la - hi.astype(F32)
    mid = r1.astype(BF16)
    lo = (r1 - mid.astype(F32)).astype(BF16)
    b = _dot(ltri, hi) + _dot(ltri, mid) + _dot(ltri, lo)
    b_last = b[C - 1:C, :]
    q_in = q * jnp.exp(b)
    k_in = (k * jnp.exp(-b)).astype(BF16)
    k_dec = k * jnp.exp(b_last - b)
    lane_h = _iota((C, HK), 1) >> _log2(dk)
    s_old = st[...]
    s_old_b = s_old.astype(BF16)
    upd = jnp.zeros(s_old.shape, F32)
    for h in range(n_heads):
        own = lane_h == h
        qh = jnp.where(own, q_in, 0.0).astype(BF16)
        att = jnp.where(causal, _dot_nt(qh, k_in), 0.0)
        vh = v_ref[0, :, h * dv:(h + 1) * dv].astype(BF16)
        oh = _dot(att.astype(BF16), vh) + _dot_nt(qh, s_old_b)
        y = oh * lax.rsqrt(jnp.mean(oh * oh, axis=-1, keepdims=True) + EPS) * g_ref[...]
        o_ref[0, :, h * dv:(h + 1) * dv] = y * _silu(r_ref[0, :, h * dv:(h + 1) * dv])
        kd = jnp.where(own, k_dec, 0.0).astype(BF16)
        upd = upd + _dot_tn(vh, kd)
    st[...] = s_old * jnp.exp(b_last) + upd

    @pl.when(c == pl.num_programs(1) - 1)
    def _():
        sfin_ref[0] = st[...]


def gla(q, k, v, r, la, s0t, g):
    B, L, HK = q.shape
    HV = v.shape[2]
    dv = g.shape[1]
    dk = HK // (HV // dv)
    C = min(GLA_CHUNK, L)
    ks = pl.BlockSpec((1, C, HK), lambda b, c: (b, c, 0))
    vs = pl.BlockSpec((1, C, HV), lambda b, c: (b, c, 0))
    ss = pl.BlockSpec((1, dv, HK), lambda b, c: (b, 0, 0))
    return pl.pallas_call(
        functools.partial(_gla_kernel, dk=dk, dv=dv),
        grid=(B, L // C),
        in_specs=[ks, ks, vs, vs, ks, ss, pl.BlockSpec((1, dv), lambda b, c: (0, 0))],
        out_specs=[vs, ss],
        out_shape=[jax.ShapeDtypeStruct(v.shape, F32), jax.ShapeDtypeStruct(s0t.shape, F32)],
        scratch_shapes=[pltpu.VMEM((dv, HK), F32)],
        compiler_params=_cparams(("parallel", "arbitrary")),
        name="gla",
    )(q, k, v, r, la, s0t, g)


def _gelu_tanh(x):
    return x * (0.5 * (1.0 + jnp.tanh(math.sqrt(2.0 / math.pi) * (x + 0.044715 * (x * x * x)))))


def _compress_mlp(seg_rows, w1_ref, b1_ref, w2_ref, b2_ref, bd_ref, g_ref, o_ref, *, n_pairs, nseg, norm):
    wh = w1_ref.shape[3]
    R = n_pairs * nseg
    p0 = jnp.zeros((R, wh), F32)
    p1 = jnp.zeros((R, wh), F32)
    for r in range(CMP_STRIDE):
        rows = jnp.concatenate([seg_rows(p, r) for p in range(n_pairs)], axis=0).astype(BF16)
        p0 = p0 + _dot(rows, w1_ref[0, r])
        p1 = p1 + _dot(rows, w1_ref[1, r])
    hid = b1_ref[...] + p0 + pltpu.roll(p1, R - 1, 0)
    y = _dot(_gelu_tanh(hid).astype(BF16), w2_ref[...]) + b2_ref[...]
    if norm:
        y = _head_rms(y, bd_ref[...], g_ref[...])
    y = jnp.where((_iota(y.shape, 0) & (nseg - 1)) < nseg - 1, y, 0.0)
    for p in range(n_pairs):
        o_ref[0, :, p * LANES:(p + 1) * LANES] = y[p * nseg:(p + 1) * nseg]


def _compress_kernel(*refs, n_pairs, nseg, norm):
    x_refs = refs[:n_pairs]
    _compress_mlp(lambda p, r: x_refs[p][0, pl.ds(r, nseg, stride=CMP_STRIDE), :], *refs[n_pairs:],
                  n_pairs=n_pairs, nseg=nseg, norm=norm)


def compress(x, n_rows, w1e, b1e, w2e, b2e, bd, g, norm):
    B, _, W = x.shape
    nseg = n_rows // CMP_STRIDE
    _log2(nseg)
    n_pairs = W // LANES
    return pl.pallas_call(
        functools.partial(_compress_kernel, n_pairs=n_pairs, nseg=nseg, norm=norm),
        grid=(B,),
        in_specs=[pl.BlockSpec((1, n_rows, LANES), functools.partial(lambda b, p: (b, 0, p), p=p))
                  for p in range(n_pairs)]
        + [_const_spec(a.shape) for a in (w1e, b1e, w2e, b2e, bd, g)],
        out_specs=pl.BlockSpec((1, nseg, W), lambda b: (b, 0, 0)),
        out_shape=jax.ShapeDtypeStruct((B, nseg, W), F32),
        compiler_params=_cparams(("parallel",)),
        name="compress",
    )(*([x] * n_pairs), w1e, b1e, w2e, b2e, bd, g)


def _compress_paged_kernel(pt_ref, *refs, group, n_pairs, nseg, norm):
    page_refs = refs[:group]
    xs = refs[-1]
    s = pl.program_id(1)
    page = page_refs[0].shape[3]
    for kk, pr in enumerate(page_refs):
        start = pl.multiple_of((s * group + kk) * page, page)
        for p in range(n_pairs):
            xs[p, pl.ds(start, page), :] = pr[0, 0, p * LANES:(p + 1) * LANES, :].T

    @pl.when(s == pl.num_programs(1) - 1)
    def _():
        _compress_mlp(lambda p, r: xs[p, pl.ds(r, nseg, stride=CMP_STRIDE), :], *refs[group:-1],
                      n_pairs=n_pairs, nseg=nseg, norm=norm)


def compress_paged(pool, layer, page_table, w1e, b1e, w2e, b2e, bd, g, norm):
    _, _, W, page = pool.shape
    B, n_pages = page_table.shape
    group = PAGE_GROUP
    n_rows = n_pages * page
    nseg = n_rows // CMP_STRIDE
    _log2(nseg)
    n_pairs = W // LANES

    def page_spec(kk):
        return pl.BlockSpec((1, 1, W, page), lambda b, s, pt: (layer, pt[b, s * group + kk], 0, 0))

    consts = (w1e, b1e, w2e, b2e, bd, g)
    grid_spec = pltpu.PrefetchScalarGridSpec(
        num_scalar_prefetch=1,
        grid=(B, n_pages // group),
        in_specs=[page_spec(kk) for kk in range(group)]
        + [pl.BlockSpec(a.shape, functools.partial(lambda b, s, pt, nd: (0,) * nd, nd=a.ndim),
                        pipeline_mode=pl.Buffered(1)) for a in consts],
        out_specs=pl.BlockSpec((1, nseg, W), lambda b, s, pt: (b, 0, 0)),
        scratch_shapes=[pltpu.VMEM((n_pairs, n_rows, LANES), F32)],
    )
    return pl.pallas_call(
        functools.partial(_compress_paged_kernel, group=group, n_pairs=n_pairs, nseg=nseg, norm=norm),
        grid_spec=grid_spec,
        out_shape=jax.ShapeDtypeStruct((B, nseg, W), F32),
        compiler_params=_cparams(("parallel", "arbitrary")),
        name="compress_paged",
    )(page_table, *([pool] * group), *consts)


_REL_EXACT = REL_BUCKETS // 2
_REL_THRESHOLDS = tuple(
    math.ceil(_REL_EXACT * (REL_MAX_DIST / _REL_EXACT) ** (k / (REL_BUCKETS - _REL_EXACT)) - 1e-9)
    for k in range(1, REL_BUCKETS - _REL_EXACT))


def _bias_kernel(base_ref, tbl_ref, o_ref, *, cstep):
    t = pl.program_id(0)
    _, H, R, C = o_ref.shape
    dist = base_ref[t] + _iota((R, C), 0) - cstep * _iota((R, C), 1)
    d = jnp.maximum(dist, 0)
    big = jnp.full((R, C), _REL_EXACT, jnp.int32)
    for th in _REL_THRESHOLDS:
        big = big + jnp.where(d >= th, 1, 0)
    bucket = jnp.where(d < _REL_EXACT, d, big)
    for h in range(H):
        acc = jnp.full((R, C), tbl_ref[h, REL_BUCKETS - 1], F32)
        for u in range(REL_BUCKETS - 1):
            acc = jnp.where(bucket == u, tbl_ref[h, u], acc)
        o_ref[0, h] = acc


def bias_tiles(rel_table, bases, R, C, cstep):
    H = rel_table.shape[1]
    T = bases.shape[0]
    grid_spec = pltpu.PrefetchScalarGridSpec(
        num_scalar_prefetch=1,
        grid=(T,),
        in_specs=[pl.BlockSpec(memory_space=pltpu.SMEM)],
        out_specs=pl.BlockSpec((1, H, R, C), lambda t, base: (t, 0, 0, 0)),
    )
    return pl.pallas_call(
        functools.partial(_bias_kernel, cstep=cstep),
        grid_spec=grid_spec,
        out_shape=jax.ShapeDtypeStruct((T, H, R, C), F32),
        compiler_params=_cparams(("arbitrary",)),
        name="bias_tiles",
    )(bases, rel_table.T)


def _masked_softmax(s, mask):
    s = jnp.where(mask, s, NEG)
    m = jnp.max(s, axis=-1, keepdims=True)
    p = jnp.where(mask, jnp.exp(s - m), 0.0)
    return p / jnp.maximum(jnp.sum(p, axis=-1, keepdims=True), 1e-30)


def _softmax_update(s, mask, m, l, acc, v):
    s = jnp.where(mask, s, NEG)
    m_new = jnp.maximum(m, jnp.max(s, axis=-1, keepdims=True))
    alpha = jnp.exp(m - m_new)
    p = jnp.where(mask, jnp.exp(s - m_new), 0.0)
    l = alpha * l + jnp.sum(p, axis=-1, keepdims=True)
    acc = alpha * acc + _dot(p.astype(BF16), v)
    return m_new, l, acc


def _off_mask(keep):
    return jnp.where(keep, 0.0, NEG)


def _softmax_update_add(s, m, l, acc, v, feature_major=False):
    m_new = jnp.maximum(m, jnp.max(s, axis=-1, keepdims=True))
    alpha = jnp.exp(m - m_new)
    p = jnp.exp(s - m_new)
    l = alpha * l + jnp.sum(p, axis=-1, keepdims=True)
    pb = p.astype(BF16)
    acc = alpha * acc + (_dot_nt(pb, v) if feature_major else _dot(pb, v))
    return m_new, l, acc


def _softmax_init(R, W):
    return jnp.full((R, 1), NEG, F32), jnp.zeros((R, 1), F32), jnp.zeros((R, W), F32)


def _softmax_finish(l, acc):
    return acc / jnp.maximum(l, 1e-30)


def _top_blocks(score, n_real, n_top, axis=1):
    jidx = _iota(score.shape, axis)
    rank = jnp.zeros(score.shape, F32)
    for i in range(n_real):
        si = score[:, i:i + 1] if axis == 1 else score[i:i + 1, :]
        beats = (si > score) | ((si == score) & (jidx > i))
        rank = rank + jnp.where(beats, 1.0, 0.0)
    return jnp.where(rank < n_top, 1.0, 0.0)


def _block_scores(imp, q_pos, n_sel, axis=1):
    j = _iota(imp.shape, axis)
    cur = q_pos >> _log2(SLC_BLOCK)
    valid = j * SLC_BLOCK <= q_pos
    forced = valid & ((j == 0) | (j == cur) | (j == cur - 1))
    score = jnp.where(forced, FORCE, jnp.where(valid, imp, -FORCE))
    return jnp.where(j < n_sel, score, -3e38)


def _block_of_key_01(n_blocks, tk, pos0):
    j = _iota((n_blocks, tk), 0)
    c = _iota((n_blocks, tk), 1)
    return jnp.where(((pos0 + c) >> _log2(SLC_BLOCK)) == j, 1.0, 0.0).astype(BF16)


def _half_to(x, src_high, dst_high):
    return x if src_high == dst_high else pltpu.roll(x, HEAD_DIM, 1)


def _nsa_prompt_kernel(q_ref, g_ref, kc_ref, vc_ref, ks_ref, vs_ref, kw_ref, vw_ref, bd_ref, bc_ref, covt_ref,
                       o_ref, *, tq, n_cmp, n_sel, n_top, group):
    i = pl.program_id(2)
    G = group
    R = G * tq
    nt = tq // LANES
    JP = _round_up(n_sel, 8)
    low = _iota((tq, LANES), 1) < HEAD_DIM
    row = _iota((R, tq), 0) & (tq - 1)
    col = _iota((R, tq), 1)
    causal_off = _off_mask(col <= row)
    for hh in range(2):
        own = low if hh == 0 else jnp.logical_not(low)
        pieces = []
        for g in range(G):
            x = q_ref[0, :, (hh * 2 + g // 2) * LANES:(hh * 2 + g // 2 + 1) * LANES] * (HEAD_DIM ** -0.5)
            pieces.append(jnp.where(own, _half_to(x, g % 2 == 1, hh == 1), 0.0))
        qs = jnp.concatenate(pieces, axis=0).astype(BF16)

        def dense_bias(d):
            blocks = []
            for g in range(G):
                for a in range(nt):
                    blocks.append(jnp.concatenate(
                        [bd_ref[jnp.maximum(nt * d + a - b, 0), hh * G + g] for b in range(nt)], axis=1))
            return jnp.concatenate(blocks, axis=0)

        bias_c = jnp.concatenate([bc_ref[a, hh * G + g] for g in range(G) for a in range(nt)], axis=0)
        s = _dot_nt(qs, kc_ref[0].astype(BF16)) + bias_c
        rowc = _iota(s.shape, 0) & (tq - 1)
        colc = _iota(s.shape, 1)
        dist_c = i * tq + rowc - (colc * CMP_STRIDE + (CMP_BLOCK - 1))
        p_c = _masked_softmax(s, (dist_c >= 0) & (colc < n_cmp)).astype(BF16)
        o_c = _dot(p_c, vc_ref[0].astype(BF16))
        covt = covt_ref[0:JP, :]
        impt = _dot_nt(covt, p_c[0:tq])
        for g in range(1, G):
            impt = impt + _dot_nt(covt, p_c[g * tq:(g + 1) * tq])
        q_pos = i * tq + _iota(impt.shape, 1)
        selt = _top_blocks(_block_scores(impt, q_pos, n_sel, 0), n_sel, n_top, 0)
        selt = jnp.concatenate([selt, jnp.zeros((LANES - JP, tq), F32)], axis=0)
        sel = selt.T.astype(BF16)

        def slc_tile(j, carry, diag):
            start = pl.multiple_of(j * tq, tq)
            kt = ks_ref[0, pl.ds(start, tq), :].astype(BF16)
            vt = vs_ref[0, pl.ds(start, tq), :].astype(BF16)
            off = _off_mask(_dot(sel, _block_of_key_01(LANES, tq, j * tq)) > 0.5)
            s = _dot_nt(qs, kt) + dense_bias(i - j) + jnp.concatenate([off] * G, axis=0)
            if diag:
                s = s + causal_off
            return _softmax_update_add(s, *carry, vt)

        carry = lax.fori_loop(0, i, lambda j, c: slc_tile(j, c, False), _softmax_init(R, LANES))
        _, l, acc = slc_tile(i, carry, True)
        o_s = _softmax_finish(l, acc)

        carry = _softmax_init(R, LANES)
        for dd in range(WINDOW // tq + 1):
            start = pl.multiple_of(jnp.maximum(i - dd, 0) * tq, tq)
            kt = kw_ref[0, pl.ds(start, tq), :].astype(BF16)
            vt = vw_ref[0, pl.ds(start, tq), :].astype(BF16)
            dist = dd * tq + row - col
            keep = (dist >= 0) & (dist < WINDOW)
            if dd > 0:
                keep = keep & (i >= dd)
            s = _dot_nt(qs, kt) + dense_bias(dd) + _off_mask(keep)
            carry = _softmax_update_add(s, *carry, vt)
        o_w = _softmax_finish(carry[1], carry[2])

        def gate_col(br):
            return jnp.concatenate([g_ref[0, 0, :, hh * 3 * G + g * 3 + br:hh * 3 * G + g * 3 + br + 1]
                                    for g in range(G)], axis=0)

        o = gate_col(0) * o_c + gate_col(1) * o_s + gate_col(2) * o_w
        for vc in range(G // 2):
            lo_piece = _half_to(o[(2 * vc) * tq:(2 * vc + 1) * tq], hh == 1, False)
            hi_piece = _half_to(o[(2 * vc + 1) * tq:(2 * vc + 2) * tq], hh == 1, True)
            o_ref[0, :, (hh * 2 + vc) * LANES:(hh * 2 + vc + 1) * LANES] = jnp.where(low, lo_piece, hi_piece)


def nsa_prompt(q, gates, kcmp, vcmp, ks, vs, kw, vw, bias_dense, bias_cmp, cover, n_cmp, n_sel):
    B, L, WQ = q.shape
    WK = ks.shape[2]
    group = WQ // WK
    tq = 256
    nt = tq // LANES
    n_pairs = WK // LANES
    n_top = min(SLC_TOPN, n_sel)
    qw = WQ // n_pairs
    hp = bias_dense.shape[1] // n_pairs
    kv = pl.BlockSpec((1, L, LANES), lambda b, p, i: (b, 0, p))
    cm = pl.BlockSpec((1, kcmp.shape[1], LANES), lambda b, p, i: (b, 0, p))
    return pl.pallas_call(
        functools.partial(_nsa_prompt_kernel, tq=tq, n_cmp=n_cmp, n_sel=n_sel, n_top=n_top, group=group),
        grid=(B, n_pairs, L // tq),
        in_specs=[
            pl.BlockSpec((1, tq, qw), lambda b, p, i: (b, i, p)),
            pl.BlockSpec((1, 1, tq, LANES), lambda b, p, i: (p, b, i, 0)),
            cm, cm, kv, kv, kv, kv,
            pl.BlockSpec((bias_dense.shape[0], hp, LANES, LANES), lambda b, p, i: (0, p, 0, 0)),
            pl.BlockSpec((nt, hp, LANES, LANES), lambda b, p, i: (i, p, 0, 0)),
            _const_spec(cover.shape),
        ],
        out_specs=pl.BlockSpec((1, tq, qw), lambda b, p, i: (b, i, p)),
        out_shape=jax.ShapeDtypeStruct(q.shape, F32),
        compiler_params=_cparams(("parallel", "parallel", "arbitrary")),
        name="nsa_prompt",
    )(q, gates, kcmp, vcmp, ks, vs, kw, vw, bias_dense, bias_cmp, cover)


def _nsa_sample_kernel(pt_ref, q_ref, g_ref, kc_ref, vc_ref, ksn_ref, vsn_ref, kwp_ref, vwp_ref,
                       kwn_ref, vwn_ref, bs_ref, bn_ref, bc_ref, bw_ref, cov_ref, e_ref, *rest,
                       n_tok, q_base, n_cmp, n_sel, n_top, group, pages):
    k_refs, v_refs = rest[:pages], rest[pages:2 * pages]
    o_ref, qs_sc, sel_sc, oc_sc, m_sc, l_sc, acc_sc = rest[2 * pages:]
    ci = pl.program_id(1)
    R, W = qs_sc.shape
    n_heads = R // n_tok
    J = sel_sc.shape[1]

    def tok(shape):
        return _iota(shape, 0) & (n_tok - 1)

    @pl.when(ci == 0)
    def _():
        blocks = []
        for h in range(n_heads):
            hkv = h // group
            x = q_ref[0, :, (h // 2) * LANES:(h // 2 + 1) * LANES] * (HEAD_DIM ** -0.5)
            x = _half_to(x, h % 2 == 1, hkv % 2 == 1)
            keep = (_iota(x.shape, 1) >= HEAD_DIM) if hkv % 2 == 1 else (_iota(x.shape, 1) < HEAD_DIM)
            x = jnp.where(keep, x, 0.0)
            blocks.append(jnp.concatenate(
                [x if cc == hkv // 2 else jnp.zeros_like(x) for cc in range(W // LANES)], axis=1))
        qs = jnp.concatenate(blocks, axis=0).astype(BF16)
        qs_sc[...] = qs
        s = _dot_nt(qs, kc_ref[0].astype(BF16)) + bc_ref[...]
        n = _iota(s.shape, 1)
        dist_c = q_base + tok(s.shape) - (n * CMP_STRIDE + (CMP_BLOCK - 1))
        p_c = _masked_softmax(s, (dist_c >= 0) & (n < n_cmp)).astype(BF16)
        oc_sc[...] = _dot(p_c, vc_ref[0].astype(BF16))
        ic = _dot(p_c, cov_ref[...])
        rows = group * n_tok
        imp_blocks = []
        for hkv in range(n_heads // group):
            a = ic[hkv * rows:hkv * rows + n_tok]
            for g in range(1, group):
                a = a + ic[hkv * rows + g * n_tok:hkv * rows + (g + 1) * n_tok]
            imp_blocks += [a] * group
        imp = jnp.concatenate(imp_blocks, axis=0)
        sel_sc[...] = _top_blocks(_block_scores(imp, q_base + tok(imp.shape), n_sel), n_sel, n_top)
        m0, l0, a0 = _softmax_init(R, W)
        m_sc[...] = m0
        l_sc[...] = l0
        acc_sc[...] = a0

    qs = qs_sc[...]
    sel = sel_sc[...].astype(BF16)

    kt = jnp.concatenate([r[0, 0] for r in k_refs], axis=1).astype(BF16)
    vt = jnp.concatenate([r[0, 0] for r in v_refs], axis=1).astype(BF16)
    s = _dot(qs, kt) + bs_ref[...] + _off_mask(_dot(sel, e_ref[...]) > 0.5)
    m, l, acc = _softmax_update_add(s, m_sc[...], l_sc[...], acc_sc[...], vt, True)
    m_sc[...] = m
    l_sc[...] = l
    acc_sc[...] = acc

    @pl.when(ci == pl.num_programs(1) - 1)
    def _():
        tn = ksn_ref.shape[1]
        c = _iota((R, tn), 1)
        causal_new = c <= tok((R, tn))
        s = _dot_nt(qs, ksn_ref[0].astype(BF16)) + bn_ref[...]
        chosen = _dot(sel, _block_of_key_01(J, tn, q_base))
        _, l1, a1 = _softmax_update_add(s + _off_mask((chosen > 0.5) & causal_new), m, l, acc,
                                        vsn_ref[0].astype(BF16))
        o_s = _softmax_finish(l1, a1)
        s = _dot(qs, kwp_ref[0, 0].astype(BF16)) + bw_ref[...]
        dist = WINDOW + tok(s.shape) - _iota(s.shape, 1)
        cw = _softmax_update_add(s + _off_mask((dist >= 0) & (dist < WINDOW)), *_softmax_init(R, W),
                                 vwp_ref[0, 0].astype(BF16), True)
        s = _dot_nt(qs, kwn_ref[0].astype(BF16)) + bn_ref[...]
        _, l2, a2 = _softmax_update_add(s + _off_mask(causal_new), *cw, vwn_ref[0].astype(BF16))
        o_w = _softmax_finish(l2, a2)

        def gate_col(br):
            cols = []
            for h in range(n_heads):
                hkv, g = divmod(h, group)
                cidx = (hkv % 2) * 3 * group + g * 3 + br
                cols.append(g_ref[hkv // 2, 0, :, cidx:cidx + 1])
            return jnp.concatenate(cols, axis=0)

        o = gate_col(0) * oc_sc[...] + gate_col(1) * o_s + gate_col(2) * o_w
        low = _iota((n_tok, LANES), 1) < HEAD_DIM
        for oc in range(n_heads // 2):
            hkv = (2 * oc) // group
            src = slice((hkv // 2) * LANES, (hkv // 2 + 1) * LANES)
            lo_piece = _half_to(o[(2 * oc) * n_tok:(2 * oc + 1) * n_tok, src], hkv % 2 == 1, False)
            hi_piece = _half_to(o[(2 * oc + 1) * n_tok:(2 * oc + 2) * n_tok, src], hkv % 2 == 1, True)
            o_ref[0, :, oc * LANES:(oc + 1) * LANES] = jnp.where(low, lo_piece, hi_piece)


def nsa_sample(q, gates, kcmp, vcmp, pool_k, pool_v, win_k, win_v, layer, page_table, ks_new, vs_new,
               kw_new, vw_new, bias_slc, bias_new, bias_cmp, bias_win, cover, n_cmp, n_sel):
    B, n_tok, WQ = q.shape
    WK, page = pool_k.shape[2], pool_k.shape[3]
    n_pages = page_table.shape[1]
    past = n_pages * page
    group = WQ // WK
    n_heads = WQ // HEAD_DIM
    R = n_heads * n_tok
    J = cover.shape[1]
    pages = PAGE_GROUP
    chunk = pages * page
    block_of_key = (jnp.arange(past)[None, :] // SLC_BLOCK == jnp.arange(J)[:, None]).astype(BF16)

    def bspec(a):
        return pl.BlockSpec((1,) + a.shape[1:], lambda b, c, pt: (b,) + (0,) * (a.ndim - 1))

    def cspec(a):
        return pl.BlockSpec(a.shape, lambda b, c, pt: (0,) * a.ndim, pipeline_mode=pl.Buffered(1))

    def page_spec(kk):
        return pl.BlockSpec((1, 1, WK, page), lambda b, c, pt: (layer, pt[b, c * pages + kk], 0, 0))

    wspec = pl.BlockSpec((1, 1, WK, win_k.shape[3]), lambda b, c, pt: (layer, b, 0, 0))
    grid_spec = pltpu.PrefetchScalarGridSpec(
        num_scalar_prefetch=1,
        grid=(B, n_pages // pages),
        in_specs=[
            bspec(q),
            pl.BlockSpec((gates.shape[0], 1, n_tok, LANES), lambda b, c, pt: (0, b, 0, 0)),
            bspec(kcmp), bspec(vcmp), bspec(ks_new), bspec(vs_new), wspec, wspec, bspec(kw_new), bspec(vw_new),
            pl.BlockSpec((R, chunk), lambda b, c, pt: (0, c)),
            cspec(bias_new), cspec(bias_cmp), cspec(bias_win), cspec(cover),
            pl.BlockSpec((J, chunk), lambda b, c, pt: (0, c)),
        ] + [page_spec(kk) for kk in range(pages)] * 2,
        out_specs=bspec(q),
        scratch_shapes=[pltpu.VMEM((R, WK), BF16), pltpu.VMEM((R, J), F32), pltpu.VMEM((R, WK), F32),
                        pltpu.VMEM((R, 1), F32), pltpu.VMEM((R, 1), F32), pltpu.VMEM((R, WK), F32)],
    )
    return pl.pallas_call(
        functools.partial(_nsa_sample_kernel, n_tok=n_tok, q_base=past, n_cmp=n_cmp, n_sel=n_sel,
                          n_top=min(SLC_TOPN, n_sel), group=group, pages=pages),
        grid_spec=grid_spec,
        out_shape=jax.ShapeDtypeStruct(q.shape, F32),
        compiler_params=_cparams(("parallel", "arbitrary")),
        name="nsa_sample",
    )(page_table, q, gates, kcmp, vcmp, ks_new, vs_new, win_k, win_v, kw_new, vw_new,
      bias_slc, bias_new, bias_cmp, bias_win, cover, block_of_key, *([pool_k] * pages), *([pool_v] * pages))


def _block_diag_ones(width):
    h = jnp.arange(width) // HEAD_DIM
    return (h[:, None] == h[None, :]).astype(BF16)


def _cover_01(n_cmp, n_sel, rows, cols):
    ci = jnp.arange(rows) * CMP_STRIDE
    sj = jnp.arange(cols) * SLC_BLOCK
    hit = (ci[:, None] < sj[None, :] + SLC_BLOCK) & (ci[:, None] + CMP_BLOCK > sj[None, :])
    hit = hit & (jnp.arange(rows)[:, None] < n_cmp) & (jnp.arange(cols)[None, :] < n_sel)
    return hit.astype(BF16)


def _round_up(n, m):
    return -(-n // m) * m


def _compress_weights(w1, b1, w2, b2, n_kv):
    hidden = w1.shape[1]
    eye = jnp.eye(n_kv, dtype=F32)
    w1r = w1.reshape(CMP_BLOCK // CMP_STRIDE, CMP_STRIDE, HEAD_DIM, hidden)
    w1e = jnp.einsum('mrde,hk->mrhdke', w1r, eye).reshape(
        CMP_BLOCK // CMP_STRIDE, CMP_STRIDE, n_kv * HEAD_DIM, n_kv * hidden).astype(BF16)
    w2e = jnp.einsum('ed,hk->hekd', w2, eye).reshape(n_kv * hidden, n_kv * HEAD_DIM).astype(BF16)
    return w1e, jnp.tile(b1, n_kv)[None, :], w2e, jnp.tile(b2, n_kv)[None, :]


def _trunk(x, mods, past, P):
    B, L, D = x.shape
    depth = P['norm_g'].shape[0]
    even_states, odd_states = [], []
    for li in range(depth):
        def m(k):
            return mods[li, :, k][:, None, :]

        def g(k):
            return P['norm_g'][li, k][None, :]

        x = ffn(x, m(0), m(1), m(2), g(0), P['ffn_w_in'][li, 0], P['ffn_w_out'][li, 0])
        if li % 2 == 0:
            e = li // 2
            w = P['even_w_in'][e]
            n_sb = P['sb_width']
            HK, HV = P['gla_hk'], P['gla_hv']
            offs = [0, n_sb, 2 * n_sb, 3 * n_sb, 3 * n_sb + HK, 3 * n_sb + 2 * HK, 3 * n_sb + 2 * HK + HV,
                    3 * n_sb + 2 * HK + 2 * HV]
            ws = [w[:, offs[k]:offs[k + 1]] for k in range(7)]
            rank = w.shape[1] - offs[7]
            wa1 = jnp.pad(w[:, offs[7]:], ((0, 0), (0, LANES - rank)))
            wa2 = jnp.pad(P['gla_w_a2'][e], ((0, LANES - rank), (0, 0)))
            qa, ka, va, qb, kb, vb, rb, la = even_proj(x, m(3), m(4), g(1), ws, wa1, wa2, P['gla_b_a2'][e][None, :])
            dv = P['gla_onorm_g'].shape[1]
            n_gh = HV // dv
            dk = HK // n_gh
            if past is None:
                o_a = sb_prompt(qa, ka, va)
                s0t = jnp.zeros((B, dv, HK), F32)
            else:
                o_a = sb_sample(qa, ka, va, past['sb_k'], past['sb_v'], e, past['page_table'])
                s0t = past['gla'][e].transpose(0, 3, 1, 2).reshape(B, dv, HK)
            o_b, st = gla(qb, kb, vb, rb, la, s0t, P['gla_onorm_g'][e][None, :])
            wo = P['even_w_out'][e]
            x = out_proj(x, m(5), [o_a, o_b], [wo[:n_sb], wo[n_sb:]])
            s_fin = st.reshape(B, dv, n_gh, dk).transpose(0, 2, 3, 1)
            even_states.append((ka.reshape(B, L, -1, HEAD_DIM), va.reshape(B, L, -1, HEAD_DIM), s_fin))
        else:
            o_i = li // 2
            w = P['odd_w_in'][o_i]
            WQ, WK = P['nsa_wq'], P['nsa_wk']
            n_kv = WK // HEAD_DIM
            group = WQ // WK
            offs = [0, WQ] + [WQ + (k + 1) * WK for k in range(6)]
            ws = [w[:, offs[k]:offs[k + 1]] for k in range(7)]
            wg = w[:, offs[7]:]
            half = wg.shape[1] // 2
            ws += [jnp.pad(wg[:, :half], ((0, 0), (0, LANES - half))),
                   jnp.pad(wg[:, half:], ((0, 0), (0, LANES - half)))]
            qk_g = P['nsa_qk_g'][o_i]
            bdq, bdk = _block_diag_ones(WQ), _block_diag_ones(WK)
            consts = [bdq, bdk, jnp.tile(qk_g[0], WQ // HEAD_DIM)[None, :],
                      jnp.tile(qk_g[2], n_kv)[None, :], jnp.tile(qk_g[3], n_kv)[None, :]]
            q, kc, vc, ks, vs, kw, vw, gates = odd_proj(x, m(3), m(4), g(1), ws, consts)
            pair = LANES // HEAD_DIM
            gk = jnp.tile(qk_g[1], pair)[None, :]
            bdp = _block_diag_ones(LANES)
            cw = [_compress_weights(P['cmp_w1'][o_i, t], P['cmp_b1'][o_i, t], P['cmp_w2'][o_i, t],
                                    P['cmp_b2'][o_i, t], pair) for t in range(2)]
            rel = P['rel_table']
            if past is None:
                n_rows = (L // CMP_STRIDE) * CMP_STRIDE
                assert n_rows == L and L % LANES == 0
                n_cmp = L // CMP_STRIDE - CMP_BLOCK // CMP_STRIDE + 1
                n_sel = -(-L // SLC_BLOCK)
                kcmp = compress(kc, n_rows, *cw[0], bdp, gk, True)
                vcmp = compress(vc, n_rows, *cw[1], bdp, gk, False)
                assert kcmp.shape[1] == LANES and n_sel <= LANES
                nq = L // LANES
                bias_dense = bias_tiles(rel, jnp.arange(nq, dtype=jnp.int32) * LANES, LANES, LANES, 1)
                bias_cmp = bias_tiles(rel, jnp.arange(nq, dtype=jnp.int32) * LANES - (CMP_BLOCK - 1),
                                      LANES, LANES, CMP_STRIDE)
                cover = _cover_01(n_cmp, n_sel, LANES, LANES).T
                o = nsa_prompt(q, gates, kcmp, vcmp, ks, vs, kw, vw, bias_dense, bias_cmp, cover, n_cmp, n_sel)
                w_keep = min(WINDOW, L)
                win_k, win_v = kw[:, L - w_keep:], vw[:, L - w_keep:]
            else:
                pt = past['page_table']
                past_len = pt.shape[1] * past['cmp_k'].shape[3]
                T = past_len + L
                n_rows = (T // CMP_STRIDE) * CMP_STRIDE
                assert n_rows == past_len and past['win_k'].shape[3] == WINDOW and L <= LANES
                n_cmp = n_rows // CMP_STRIDE - CMP_BLOCK // CMP_STRIDE + 1
                n_sel = -(-T // SLC_BLOCK)
                kcmp = compress_paged(past['cmp_k'], o_i, pt, *cw[0], bdp, gk, True)
                vcmp = compress_paged(past['cmp_v'], o_i, pt, *cw[1], bdp, gk, False)
                J = _round_up(n_sel, LANES)
                n_heads = WQ // HEAD_DIM
                R = n_heads * L

                def sample_bias(base, C, cstep):
                    return bias_tiles(rel, jnp.array([base], jnp.int32), L, C, cstep).reshape(R, C)

                bias_slc = sample_bias(past_len, past_len, 1)
                bias_new = sample_bias(0, LANES, 1)
                bias_cmp = sample_bias(past_len - (CMP_BLOCK - 1), kcmp.shape[1], CMP_STRIDE)
                bias_win = sample_bias(WINDOW, WINDOW, 1)
                cover = _cover_01(n_cmp, n_sel, kcmp.shape[1], J)
                pad = ((0, 0), (0, LANES - L), (0, 0))
                o = nsa_sample(q, gates, kcmp, vcmp, past['slc_k'], past['slc_v'], past['win_k'], past['win_v'],
                               o_i, pt, jnp.pad(ks, pad), jnp.pad(vs, pad), jnp.pad(kw, pad), jnp.pad(vw, pad),
                               bias_slc, bias_new, bias_cmp, bias_win, cover, n_cmp, n_sel)
                win_k = jnp.concatenate([past['win_k_rows'][o_i], kw], axis=1)[:, L:]
                win_v = jnp.concatenate([past['win_v_rows'][o_i], vw], axis=1)[:, L:]
            x = out_proj(x, m(5), [o], [P['odd_w_out'][o_i]])

            def rows(t):
                return t.reshape(B, t.shape[1], n_kv, HEAD_DIM)

            odd_states.append(tuple(rows(t) for t in (kc, vc, ks, vs, win_k, win_v)))
        x = ffn(x, m(6), m(7), m(8), g(2), P['ffn_w_in'][li, 1], P['ffn_w_out'][li, 1])
    even_new = [jnp.stack([s[i] for s in even_states]) for i in range(3)]
    odd_new = [jnp.stack([s[i] for s in odd_states]) for i in range(6)]
    return x, even_new, odd_new


def kernel(x_prompt, x_sample, cache_sb_k, cache_sb_v, state_gla, cache_cmp_k, cache_cmp_v, cache_slc_k,
           cache_slc_v, cache_win_k, cache_win_v, page_table, c_prompt, c_sample, norm_g, ada_w, ada_b,
           ffn_w_in, ffn_w_out, even_w_in, gla_w_a2, gla_b_a2, gla_onorm_g, even_w_out, odd_w_in, nsa_qk_g,
           cmp_w1, cmp_b1, cmp_w2, cmp_b2, rel_table, odd_w_out):
    D = x_prompt.shape[-1]
    depth = norm_g.shape[0]
    Bp, Bs = x_prompt.shape[0], x_sample.shape[0]
    mods = ada_mod(jnp.concatenate([c_prompt, c_sample], axis=0), ada_w, ada_b).reshape(depth, Bp + Bs, 9, D)
    n_sb = cache_sb_k.shape[3] * cache_sb_k.shape[4]
    dv = gla_onorm_g.shape[1]
    HK = gla_w_a2.shape[2]
    n_gh = state_gla.shape[2]
    WK = cache_cmp_k.shape[3] * cache_cmp_k.shape[4]
    P = {
        'norm_g': norm_g, 'ffn_w_in': ffn_w_in.astype(BF16), 'ffn_w_out': ffn_w_out.astype(BF16),
        'even_w_in': even_w_in.astype(BF16), 'gla_w_a2': gla_w_a2.astype(BF16), 'gla_b_a2': gla_b_a2,
        'gla_onorm_g': gla_onorm_g, 'even_w_out': even_w_out.astype(BF16), 'odd_w_in': odd_w_in.astype(BF16),
        'nsa_qk_g': nsa_qk_g, 'cmp_w1': cmp_w1, 'cmp_b1': cmp_b1, 'cmp_w2': cmp_w2, 'cmp_b2': cmp_b2,
        'rel_table': rel_table, 'odd_w_out': odd_w_out.astype(BF16),
        'sb_width': n_sb, 'gla_hk': HK, 'gla_hv': n_gh * dv, 'nsa_wq': odd_w_out.shape[1], 'nsa_wk': WK,
    }

    def pool(c):
        return c.transpose(0, 1, 3, 4, 2).reshape(c.shape[0], c.shape[1], c.shape[3] * c.shape[4], c.shape[2])

    def rows(c):
        return c.reshape(c.shape[0], c.shape[1], c.shape[2], c.shape[3] * c.shape[4])

    past = {
        'page_table': page_table, 'sb_k': pool(cache_sb_k), 'sb_v': pool(cache_sb_v), 'gla': state_gla,
        'cmp_k': pool(cache_cmp_k), 'cmp_v': pool(cache_cmp_v), 'slc_k': pool(cache_slc_k),
        'slc_v': pool(cache_slc_v), 'win_k': pool(cache_win_k), 'win_v': pool(cache_win_v),
        'win_k_rows': rows(cache_win_k), 'win_v_rows': rows(cache_win_v),
    }
    y_p, p_even, p_odd = _trunk(x_prompt, mods[:, :Bp], None, P)
    y_s, s_even, s_odd = _trunk(x_sample, mods[:, Bp:], past, P)
    return (y_p, y_s, *p_even, *p_odd, *s_even, *s_odd)
```

```python
import functools
import math

import jax
import jax.numpy as jnp
from jax import lax
from jax.experimental import pallas as pl
from jax.experimental.pallas import tpu as pltpu

F32 = jnp.float32
BF16 = jnp.bfloat16

EPS = 1e-6
NEG = -1e30
FORCE = 1e6
HEAD_DIM = 64
GLA_TAU = 16.0
GLA_CHUNK = 64
CMP_BLOCK = 32
CMP_STRIDE = 16
SLC_BLOCK = 64
SLC_TOPN = 16
WINDOW = 512
REL_BUCKETS = 32
REL_MAX_DIST = 1024
PAGE_GROUP = 16
LANES = 128
VMEM_LIMIT = 56 * 1024 * 1024


def _cparams(sem, vmem=VMEM_LIMIT):
    return pltpu.CompilerParams(dimension_semantics=sem, vmem_limit_bytes=vmem)


def _dot(a, b):
    return jnp.dot(a, b, preferred_element_type=F32)


def _dot_nt(a, b):
    return lax.dot_general(a, b, (((1,), (1,)), ((), ())), preferred_element_type=F32)


def _dot_tn(a, b):
    return lax.dot_general(a, b, (((0,), (0,)), ((), ())), preferred_element_type=F32)


def _split2(x):
    hi = x.astype(BF16)
    lo = (x - hi.astype(F32)).astype(BF16)
    return hi, lo


def _dot_f32_by01(x, m01):
    hi, lo = _split2(x)
    return _dot(hi, m01) + _dot(lo, m01)


def _dot_01_by_f32(m01, x):
    hi, lo = _split2(x)
    return _dot(m01, hi) + _dot(m01, lo)


def _softplus(z):
    return jnp.maximum(z, 0.0) + jnp.log1p(jnp.exp(-jnp.abs(z)))


def _silu(a):
    return a * jax.nn.sigmoid(a)


def _iota(shape, dim):
    return lax.broadcasted_iota(jnp.int32, shape, dim)


def _modulate(x, g, shift, scale):
    ms = jnp.mean(x * x, axis=-1, keepdims=True)
    y = x * lax.rsqrt(ms + EPS) * g
    return y * (1.0 + scale) + shift


def _head_rms(x, bd, g):
    ms = _dot_f32_by01(x * x, bd) * (1.0 / HEAD_DIM)
    return x * lax.rsqrt(ms + EPS) * g


def _const_spec(shape):
    nd = len(shape)
    return pl.BlockSpec(shape, lambda *_: (0,) * nd, pipeline_mode=pl.Buffered(1))


def _row_tiles(B, L):
    if L >= 512:
        return 1, 512
    return B, L


def _ada_kernel(c_ref, w_ref, b_ref, o_ref):
    c = c_ref[...]
    a = _silu(c).astype(BF16)
    o_ref[0] = _dot(a, w_ref[0].astype(BF16)) + b_ref[0]


def ada_mod(c, ada_w, ada_b):
    R, D = c.shape
    depth, _, N = ada_w.shape
    tn = 1024
    return pl.pallas_call(
        _ada_kernel,
        grid=(depth, N // tn),
        in_specs=[
            pl.BlockSpec((R, D), lambda l, n: (0, 0)),
            pl.BlockSpec((1, D, tn), lambda l, n: (l, 0, n)),
            pl.BlockSpec((1, 1, tn), lambda l, n: (l, 0, n)),
        ],
        out_specs=pl.BlockSpec((1, R, tn), lambda l, n: (l, 0, n)),
        out_shape=jax.ShapeDtypeStruct((depth, R, N), F32),
        compiler_params=_cparams(("parallel", "parallel")),
        name="ada_mod",
    )(c, ada_w, ada_b.reshape(depth, 1, N))


def _ffn_kernel(x_ref, sh_ref, sc_ref, gt_ref, g_ref, win_ref, wout_ref, o_ref, *, d_ff, tf):
    x = x_ref[...]
    tb, tl, D = x.shape
    h = _modulate(x, g_ref[...], sh_ref[...], sc_ref[...])
    hb = h.reshape(tb * tl, D).astype(BF16)
    acc = jnp.zeros((tb * tl, D), F32)
    for j in range(d_ff // tf):
        a = _dot(hb, win_ref[:, j * tf:(j + 1) * tf])
        b = _dot(hb, win_ref[:, d_ff + j * tf:d_ff + (j + 1) * tf])
        u = (_silu(a) * b).astype(BF16)
        acc = acc + _dot(u, wout_ref[j * tf:(j + 1) * tf, :])
    o_ref[...] = x + 0.5 * gt_ref[...] * acc.reshape(tb, tl, D)


def ffn(x, shift, scale, gate, g, w_in, w_out):
    B, L, D = x.shape
    d_ff = w_out.shape[0]
    tb, tl = _row_tiles(B, L)
    xs = pl.BlockSpec((tb, tl, D), lambda b, l: (b, l, 0))
    ms = pl.BlockSpec((tb, 1, D), lambda b, l: (b, 0, 0))
    return pl.pallas_call(
        functools.partial(_ffn_kernel, d_ff=d_ff, tf=256),
        grid=(B // tb, L // tl),
        in_specs=[xs, ms, ms, ms, _const_spec((1, D)), _const_spec(w_in.shape), _const_spec(w_out.shape)],
        out_specs=xs,
        out_shape=jax.ShapeDtypeStruct(x.shape, F32),
        compiler_params=_cparams(("parallel", "parallel")),
        name="ffn",
    )(x, shift, scale, gate, g, w_in, w_out)


def _even_proj_kernel(x_ref, sh_ref, sc_ref, g_ref, wqa, wka, wva, wqb, wkb, wvb, wrb, wa1, wa2, ba2,
                      qa, ka, va, qb, kb, vb, rb, la):
    x = x_ref[...]
    tb, tl, D = x.shape
    h = _modulate(x, g_ref[...], sh_ref[...], sc_ref[...])
    hb = h.reshape(tb * tl, D).astype(BF16)
    for w, o in ((wqa, qa), (wka, ka), (wva, va), (wqb, qb), (wkb, kb), (wvb, vb), (wrb, rb)):
        o[...] = _dot(hb, w[...]).reshape(o.shape)
    a1 = _dot(hb, wa1[...]).astype(BF16)
    z = _dot(a1, wa2[...]) + ba2[...]
    la[...] = (-_softplus(-z) * (1.0 / GLA_TAU)).reshape(la.shape)


def even_proj(x, shift, scale, g, ws, wa1, wa2, ba2):
    B, L, D = x.shape
    tb, tl = _row_tiles(B, L)
    xs = pl.BlockSpec((tb, tl, D), lambda b, l: (b, l, 0))
    ms = pl.BlockSpec((tb, 1, D), lambda b, l: (b, 0, 0))
    widths = [w.shape[1] for w in ws] + [wa2.shape[1]]
    return pl.pallas_call(
        _even_proj_kernel,
        grid=(B // tb, L // tl),
        in_specs=[xs, ms, ms, _const_spec((1, D))] + [_const_spec(w.shape) for w in ws]
        + [_const_spec(wa1.shape), _const_spec(wa2.shape), _const_spec(ba2.shape)],
        out_specs=[pl.BlockSpec((tb, tl, n), lambda b, l: (b, l, 0)) for n in widths],
        out_shape=[jax.ShapeDtypeStruct((B, L, n), F32) for n in widths],
        compiler_params=_cparams(("parallel", "parallel")),
        name="even_proj",
    )(x, shift, scale, g, *ws, wa1, wa2, ba2)


def _odd_proj_kernel(x_ref, sh_ref, sc_ref, g_ref, wq, wkc, wvc, wks, wvs, wkw, wvw, wg0, wg1,
                     bdq, bdk, gq, gks, gkw,
                     q, kc, vc, ks, vs, kw, vw, gates):
    x = x_ref[...]
    tb, tl, D = x.shape
    h = _modulate(x, g_ref[...], sh_ref[...], sc_ref[...])
    hb = h.reshape(tb * tl, D).astype(BF16)
    q[...] = _head_rms(_dot(hb, wq[...]), bdq[...], gq[...]).reshape(q.shape)
    kc[...] = _dot(hb, wkc[...]).reshape(kc.shape)
    vc[...] = _dot(hb, wvc[...]).reshape(vc.shape)
    ks[...] = _head_rms(_dot(hb, wks[...]), bdk[...], gks[...]).reshape(ks.shape)
    vs[...] = _dot(hb, wvs[...]).reshape(vs.shape)
    kw[...] = _head_rms(_dot(hb, wkw[...]), bdk[...], gkw[...]).reshape(kw.shape)
    vw[...] = _dot(hb, wvw[...]).reshape(vw.shape)
    gates[0] = jax.nn.sigmoid(_dot(hb, wg0[...])).reshape(gates.shape[1:])
    gates[1] = jax.nn.sigmoid(_dot(hb, wg1[...])).reshape(gates.shape[1:])


def odd_proj(x, shift, scale, g, ws, consts):
    B, L, D = x.shape
    tb, tl = _row_tiles(B, L)
    xs = pl.BlockSpec((tb, tl, D), lambda b, l: (b, l, 0))
    ms = pl.BlockSpec((tb, 1, D), lambda b, l: (b, 0, 0))
    widths = [w.shape[1] for w in ws[:7]]
    out_specs = [pl.BlockSpec((tb, tl, n), lambda b, l: (b, l, 0)) for n in widths]
    out_specs.append(pl.BlockSpec((2, tb, tl, LANES), lambda b, l: (0, b, l, 0)))
    out_shape = [jax.ShapeDtypeStruct((B, L, n), F32) for n in widths]
    out_shape.append(jax.ShapeDtypeStruct((2, B, L, LANES), F32))
    return pl.pallas_call(
        _odd_proj_kernel,
        grid=(B // tb, L // tl),
        in_specs=[xs, ms, ms, _const_spec((1, D))] + [_const_spec(w.shape) for w in ws]
        + [_const_spec(c.shape) for c in consts],
        out_specs=out_specs,
        out_shape=out_shape,
        compiler_params=_cparams(("parallel", "parallel")),
        name="odd_proj",
    )(x, shift, scale, g, *ws, *consts)


def _out_proj_kernel(*refs, n_in):
    x_ref, gt_ref = refs[0], refs[1]
    a_refs = refs[2:2 + n_in]
    w_refs = refs[2 + n_in:2 + 2 * n_in]
    o_ref = refs[2 + 2 * n_in]
    x = x_ref[...]
    tb, tl, D = x.shape
    acc = jnp.zeros((tb * tl, D), F32)
    for a, w in zip(a_refs, w_refs):
        av = a[...]
        acc = acc + _dot(av.reshape(tb * tl, av.shape[-1]).astype(BF16), w[...])
    o_ref[...] = x + gt_ref[...] * acc.reshape(tb, tl, D)


def out_proj(x, gate, acts, ws):
    B, L, D = x.shape
    tb, tl = _row_tiles(B, L)
    xs = pl.BlockSpec((tb, tl, D), lambda b, l: (b, l, 0))
    ms = pl.BlockSpec((tb, 1, D), lambda b, l: (b, 0, 0))
    return pl.pallas_call(
        functools.partial(_out_proj_kernel, n_in=len(acts)),
        grid=(B // tb, L // tl),
        in_specs=[xs, ms] + [pl.BlockSpec((tb, tl, a.shape[-1]), lambda b, l: (b, l, 0)) for a in acts]
        + [_const_spec(w.shape) for w in ws],
        out_specs=xs,
        out_shape=jax.ShapeDtypeStruct(x.shape, F32),
        compiler_params=_cparams(("parallel", "parallel")),
        name="out_proj",
    )(x, gate, *acts, *ws)


def _log2(n):
    assert n > 0 and n & (n - 1) == 0, n
    return n.bit_length() - 1


def _later_keys_01(tk):
    return jnp.where(_iota((tk, tk), 0) > _iota((tk, tk), 1), 1.0, 0.0).astype(BF16)


def _sb_tile(q2, kt, vt, run, acc, u01, before, feature_major=False):
    z = _dot(q2, kt) if feature_major else _dot_nt(q2, kt)
    sp = _softplus(z)
    lk = -sp
    ls = z - sp
    if before is not None:
        lk = jnp.where(before, lk, 0.0)
    cs = _dot_f32_by01(lk, u01)
    w = jnp.exp(ls + cs + run)
    if before is not None:
        w = jnp.where(before, w, 0.0)
    wb = w.astype(BF16)
    acc = acc + (_dot_nt(wb, vt) if feature_major else _dot(wb, vt))
    run = run + cs[:, 0:1] + lk[:, 0:1]
    return run, acc


def _sb_prompt_kernel(q_ref, k_ref, v_ref, o_ref, *, tq):
    i = pl.program_id(2)
    q = q_ref[0] * (HEAD_DIM ** -0.5)
    lane = _iota((tq, LANES), 1)
    q2 = jnp.concatenate([jnp.where(lane < HEAD_DIM, q, 0.0), jnp.where(lane >= HEAD_DIM, q, 0.0)],
                         axis=0).astype(BF16)
    u01 = _later_keys_01(tq)
    before = _iota((2 * tq, tq), 1) < (_iota((2 * tq, tq), 0) & (tq - 1))

    def tile(j, run, acc, mask):
        start = pl.multiple_of(j * tq, tq)
        kt = k_ref[0, pl.ds(start, tq), :].astype(BF16)
        vt = v_ref[0, pl.ds(start, tq), :].astype(BF16)
        return _sb_tile(q2, kt, vt, run, acc, u01, mask)

    run, acc = tile(i, jnp.zeros((2 * tq, 1), F32), jnp.zeros((2 * tq, LANES), F32), before)
    run, acc = lax.fori_loop(1, i + 1, lambda jj, c: tile(i - jj, c[0], c[1], None), (run, acc))
    o_ref[0] = jnp.where(lane < HEAD_DIM, acc[:tq], acc[tq:])


def sb_prompt(q, k, v):
    B, L, W = q.shape
    tq = 256
    qs = pl.BlockSpec((1, tq, LANES), lambda b, p, i: (b, i, p))
    ks = pl.BlockSpec((1, L, LANES), lambda b, p, i: (b, 0, p))
    return pl.pallas_call(
        functools.partial(_sb_prompt_kernel, tq=tq),
        grid=(B, W // LANES, L // tq),
        in_specs=[qs, ks, ks],
        out_specs=qs,
        out_shape=jax.ShapeDtypeStruct(q.shape, F32),
        compiler_params=_cparams(("parallel", "parallel", "arbitrary")),
        name="sb_prompt",
    )(q, k, v)


def _sb_sample_kernel(pt_ref, q_ref, kn_ref, vn_ref, *rest, n_tok, group):
    k_refs, v_refs = rest[:group], rest[group:2 * group]
    o_ref, run_ref, acc_ref = rest[2 * group:]
    s = pl.program_id(1)
    q = q_ref[0] * (HEAD_DIM ** -0.5)
    W = q.shape[1]
    n_heads = W // HEAD_DIM
    R = n_heads * n_tok
    tk = kn_ref.shape[1]
    own = (_iota((R, W), 0) >> _log2(n_tok)) == (_iota((R, W), 1) >> _log2(HEAD_DIM))
    q2 = jnp.where(own, jnp.concatenate([q] * n_heads, axis=0), 0.0).astype(BF16)
    u01 = _later_keys_01(tk)

    @pl.when(s == 0)
    def _():
        before = _iota((R, tk), 1) < (_iota((R, tk), 0) & (n_tok - 1))
        run, acc = _sb_tile(q2, kn_ref[0].astype(BF16), vn_ref[0].astype(BF16),
                            jnp.zeros((R, 1), F32), jnp.zeros((R, W), F32), u01, before)
        run_ref[...] = run
        acc_ref[...] = acc

    @pl.when(s > 0)
    def _():
        kt = jnp.concatenate([r[0, 0] for r in k_refs], axis=1).astype(BF16)
        vt = jnp.concatenate([r[0, 0] for r in v_refs], axis=1).astype(BF16)

        def stack(x):
            return jnp.concatenate([x[:, kk * tk:(kk + 1) * tk] for kk in range(group)], axis=0)

        z = _dot(q2, kt)
        sp = _softplus(z)
        lk = stack(-sp)
        cs = _dot_f32_by01(lk, u01)
        tot = cs[:, 0:1] + lk[:, 0:1]
        run = run_ref[...]
        runs = []
        for kk in range(group):
            runs.append(run)
            run = run + tot[kk * R:(kk + 1) * R]
        w = jnp.exp(stack(z - sp) + cs + jnp.concatenate(runs, axis=0))
        w = jnp.concatenate([w[kk * R:(kk + 1) * R] for kk in range(group)], axis=1).astype(BF16)
        acc_ref[...] = acc_ref[...] + _dot_nt(w, vt)
        run_ref[...] = run

    @pl.when(s == pl.num_programs(1) - 1)
    def _():
        acc = jnp.where(own, acc_ref[...], 0.0)
        out = acc[0:n_tok]
        for h in range(1, n_heads):
            out = out + acc[h * n_tok:(h + 1) * n_tok]
        o_ref[0] = out


def sb_sample(q, k_new, v_new, pool_k, pool_v, layer, page_table):
    B, n_tok, W = q.shape
    page = pool_k.shape[3]
    n_pages = page_table.shape[1]
    group = PAGE_GROUP
    n_steps = 1 + n_pages // group
    pad = ((0, 0), (0, page - n_tok), (0, 0))
    kn, vn = jnp.pad(k_new, pad), jnp.pad(v_new, pad)

    def page_spec(kk):
        def imap(b, s, pt):
            return (layer, pt[b, n_pages - 1 - (jnp.maximum(s, 1) - 1) * group - kk], 0, 0)
        return pl.BlockSpec((1, 1, W, page), imap)

    qs = pl.BlockSpec((1, n_tok, W), lambda b, s, pt: (b, 0, 0))
    ns = pl.BlockSpec((1, page, W), lambda b, s, pt: (b, 0, 0))
    R = (W // HEAD_DIM) * n_tok
    grid_spec = pltpu.PrefetchScalarGridSpec(
        num_scalar_prefetch=1,
        grid=(B, n_steps),
        in_specs=[qs, ns, ns] + [page_spec(kk) for kk in range(group)] * 2,
        out_specs=qs,
        scratch_shapes=[pltpu.VMEM((R, 1), F32), pltpu.VMEM((R, W), F32)],
    )
    return pl.pallas_call(
        functools.partial(_sb_sample_kernel, n_tok=n_tok, group=group),
        grid_spec=grid_spec,
        out_shape=jax.ShapeDtypeStruct(q.shape, F32),
        compiler_params=_cparams(("parallel", "arbitrary")),
        name="sb_sample",
    )(page_table, q, kn, vn, *([pool_k] * group), *([pool_v] * group))


def _gla_kernel(q_ref, k_ref, v_ref, r_ref, la_ref, s0_ref, g_ref, o_ref, sfin_ref, st, *, dk, dv):
    c = pl.program_id(1)

    @pl.when(c == 0)
    def _():
        st[...] = s0_ref[...]

    for ib in range(q_ref.shape[0]):
        _gla_chunk(q_ref.at[ib], k_ref.at[ib], v_ref.at[ib], r_ref.at[ib], la_ref.at[ib], g_ref, o_ref.at[ib],
                   st.at[ib], dk=dk, dv=dv)

    @pl.when(c == pl.num_programs(1) - 1)
    def _():
        sfin_ref[...] = st[...]


def _gla_chunk(q_ref, k_ref, v_ref, r_ref, la_ref, g_ref, o_ref, st, *, dk, dv):
    q = q_ref[...] * (dk ** -0.5)
    k = k_ref[...]
    la = la_ref[...]
    C, HK = q.shape
    n_heads = HK // dk
    causal = _iota((C, C), 0) >= _iota((C, C), 1)
    ltri = jnp.where(causal, 1.0, 0.0).astype(BF16)
    hi = la.astype(BF16)
    r1 = la - hi.astype(F32)
    mid = r1.astype(BF16)
    lo = (r1 - mid.astype(F32)).astype(BF16)
    b = _dot(ltri, hi) + _dot(ltri, mid) + _dot(ltri, lo)
    b_last = b[C - 1:C, :]
    q_in = q * jnp.exp(b)
    k_in = (k * jnp.exp(-b)).astype(BF16)
    k_dec = k * jnp.exp(b_last - b)
    lane_h = _iota((C, HK), 1) >> _log2(dk)
    s_old = st[...]
    s_old_b = s_old.astype(BF16)
    upd = jnp.zeros(s_old.shape, F32)
    for h in range(n_heads):
        own = lane_h == h
        qh = jnp.where(own, q_in, 0.0).astype(BF16)
        att = jnp.where(causal, _dot_nt(qh, k_in), 0.0)
        vh = v_ref[:, h * dv:(h + 1) * dv].astype(BF16)
        oh = _dot(att.astype(BF16), vh) + _dot_nt(qh, s_old_b)
        y = oh * lax.rsqrt(jnp.mean(oh * oh, axis=-1, keepdims=True) + EPS) * g_ref[...]
        o_ref[:, h * dv:(h + 1) * dv] = y * _silu(r_ref[:, h * dv:(h + 1) * dv])
        kd = jnp.where(own, k_dec, 0.0).astype(BF16)
        upd = upd + _dot_tn(vh, kd)
    st[...] = s_old * jnp.exp(b_last) + upd


def gla(q, k, v, r, la, s0t, g):
    B, L, HK = q.shape
    HV = v.shape[2]
    dv = g.shape[1]
    dk = HK // (HV // dv)
    C = min(GLA_CHUNK, L)
    bb = 2 if B % 2 == 0 else 1
    ks = pl.BlockSpec((bb, C, HK), lambda b, c: (b, c, 0))
    vs = pl.BlockSpec((bb, C, HV), lambda b, c: (b, c, 0))
    ss = pl.BlockSpec((bb, dv, HK), lambda b, c: (b, 0, 0))
    return pl.pallas_call(
        functools.partial(_gla_kernel, dk=dk, dv=dv),
        grid=(B // bb, L // C),
        in_specs=[ks, ks, vs, vs, ks, ss, pl.BlockSpec((1, dv), lambda b, c: (0, 0))],
        out_specs=[vs, ss],
        out_shape=[jax.ShapeDtypeStruct(v.shape, F32), jax.ShapeDtypeStruct(s0t.shape, F32)],
        scratch_shapes=[pltpu.VMEM((bb, dv, HK), F32)],
        compiler_params=_cparams(("parallel", "arbitrary")),
        name="gla",
    )(q, k, v, r, la, s0t, g)


def _gelu_tanh(x):
    return x * (0.5 * (1.0 + jnp.tanh(math.sqrt(2.0 / math.pi) * (x + 0.044715 * (x * x * x)))))


def _compress_mlp(seg_rows, w1_ref, b1_ref, w2_ref, b2_ref, bd_ref, g_ref, o_ref, *, n_pairs, nseg, norm):
    rk = w1_ref.shape[2] // LANES
    wh = w1_ref.shape[3]
    R = n_pairs * nseg
    p0 = jnp.zeros((R, wh), F32)
    p1 = jnp.zeros((R, wh), F32)
    for rr in range(CMP_STRIDE // rk):
        rows = jnp.concatenate(
            [jnp.concatenate([seg_rows(p, rr * rk + u) for u in range(rk)], axis=1) for p in range(n_pairs)],
            axis=0).astype(BF16)
        p0 = p0 + _dot(rows, w1_ref[0, rr])
        p1 = p1 + _dot(rows, w1_ref[1, rr])
    hid = b1_ref[...] + p0 + pltpu.roll(p1, R - 1, 0)
    y = _dot(_gelu_tanh(hid).astype(BF16), w2_ref[...]) + b2_ref[...]
    if norm:
        y = _head_rms(y, bd_ref[...], g_ref[...])
    y = jnp.where((_iota(y.shape, 0) & (nseg - 1)) < nseg - 1, y, 0.0)
    for p in range(n_pairs):
        o_ref[0, :, p * LANES:(p + 1) * LANES] = y[p * nseg:(p + 1) * nseg]


def _compress_kernel(*refs, n_pairs, nseg, norm):
    x_refs = refs[:n_pairs]
    _compress_mlp(lambda p, r: x_refs[p][0, pl.ds(r, nseg, stride=CMP_STRIDE), :], *refs[n_pairs:],
                  n_pairs=n_pairs, nseg=nseg, norm=norm)


def compress(x, n_rows, w1e, b1e, w2e, b2e, bd, g, norm):
    B, _, W = x.shape
    nseg = n_rows // CMP_STRIDE
    _log2(nseg)
    n_pairs = W // LANES
    return pl.pallas_call(
        functools.partial(_compress_kernel, n_pairs=n_pairs, nseg=nseg, norm=norm),
        grid=(B,),
        in_specs=[pl.BlockSpec((1, n_rows, LANES), functools.partial(lambda b, p: (b, 0, p), p=p))
                  for p in range(n_pairs)]
        + [_const_spec(a.shape) for a in (w1e, b1e, w2e, b2e, bd, g)],
        out_specs=pl.BlockSpec((1, nseg, W), lambda b: (b, 0, 0)),
        out_shape=jax.ShapeDtypeStruct((B, nseg, W), F32),
        compiler_params=_cparams(("parallel",)),
        name="compress",
    )(*([x] * n_pairs), w1e, b1e, w2e, b2e, bd, g)


def _compress_paged_kernel(pt_ref, *refs, group, n_pairs, nseg, norm):
    page_refs = refs[:group]
    xs = refs[-1]
    s = pl.program_id(1)
    page = page_refs[0].shape[3]
    for kk, pr in enumerate(page_refs):
        start = pl.multiple_of((s * group + kk) * page, page)
        for p in range(n_pairs):
            xs[p, pl.ds(start, page), :] = pr[0, 0, p * LANES:(p + 1) * LANES, :].T

    @pl.when(s == pl.num_programs(1) - 1)
    def _():
        _compress_mlp(lambda p, r: xs[p, pl.ds(r, nseg, stride=CMP_STRIDE), :], *refs[group:-1],
                      n_pairs=n_pairs, nseg=nseg, norm=norm)


def compress_paged(pool, layer, page_table, w1e, b1e, w2e, b2e, bd, g, norm):
    _, _, W, page = pool.shape
    B, n_pages = page_table.shape
    group = PAGE_GROUP
    n_rows = n_pages * page
    nseg = n_rows // CMP_STRIDE
    _log2(nseg)
    n_pairs = W // LANES

    def page_spec(kk):
        return pl.BlockSpec((1, 1, W, page), lambda b, s, pt: (layer, pt[b, s * group + kk], 0, 0))

    consts = (w1e, b1e, w2e, b2e, bd, g)
    grid_spec = pltpu.PrefetchScalarGridSpec(
        num_scalar_prefetch=1,
        grid=(B, n_pages // group),
        in_specs=[page_spec(kk) for kk in range(group)]
        + [pl.BlockSpec(a.shape, functools.partial(lambda b, s, pt, nd: (0,) * nd, nd=a.ndim),
                        pipeline_mode=pl.Buffered(1)) for a in consts],
        out_specs=pl.BlockSpec((1, nseg, W), lambda b, s, pt: (b, 0, 0)),
        scratch_shapes=[pltpu.VMEM((n_pairs, n_rows, LANES), F32)],
    )
    return pl.pallas_call(
        functools.partial(_compress_paged_kernel, group=group, n_pairs=n_pairs, nseg=nseg, norm=norm),
        grid_spec=grid_spec,
        out_shape=jax.ShapeDtypeStruct((B, nseg, W), F32),
        compiler_params=_cparams(("parallel", "arbitrary")),
        name="compress_paged",
    )(page_table, *([pool] * group), *consts)


_REL_EXACT = REL_BUCKETS // 2
_REL_THRESHOLDS = tuple(
    math.ceil(_REL_EXACT * (REL_MAX_DIST / _REL_EXACT) ** (k / (REL_BUCKETS - _REL_EXACT)) - 1e-9)
    for k in range(1, REL_BUCKETS - _REL_EXACT))


def _bias_kernel(base_ref, tbl_ref, o_ref, *, cstep):
    t = pl.program_id(0)
    _, H, R, C = o_ref.shape
    dist = base_ref[t] + _iota((R, C), 0) - cstep * _iota((R, C), 1)
    d = jnp.maximum(dist, 0)
    big = jnp.full((R, C), _REL_EXACT, jnp.int32)
    for th in _REL_THRESHOLDS:
        big = big + jnp.where(d >= th, 1, 0)
    bucket = jnp.where(d < _REL_EXACT, d, big)
    for h in range(H):
        acc = jnp.full((R, C), tbl_ref[h, REL_BUCKETS - 1], F32)
        for u in range(REL_BUCKETS - 1):
            acc = jnp.where(bucket == u, tbl_ref[h, u], acc)
        o_ref[0, h] = acc


def bias_tiles(rel_table, bases, R, C, cstep):
    H = rel_table.shape[1]
    T = bases.shape[0]
    grid_spec = pltpu.PrefetchScalarGridSpec(
        num_scalar_prefetch=1,
        grid=(T,),
        in_specs=[pl.BlockSpec(memory_space=pltpu.SMEM)],
        out_specs=pl.BlockSpec((1, H, R, C), lambda t, base: (t, 0, 0, 0)),
    )
    return pl.pallas_call(
        functools.partial(_bias_kernel, cstep=cstep),
        grid_spec=grid_spec,
        out_shape=jax.ShapeDtypeStruct((T, H, R, C), F32),
        compiler_params=_cparams(("arbitrary",)),
        name="bias_tiles",
    )(bases, rel_table.T)


def _masked_softmax(s, mask):
    s = jnp.where(mask, s, NEG)
    m = jnp.max(s, axis=-1, keepdims=True)
    p = jnp.where(mask, jnp.exp(s - m), 0.0)
    return p / jnp.maximum(jnp.sum(p, axis=-1, keepdims=True), 1e-30)


def _softmax_update(s, mask, m, l, acc, v):
    s = jnp.where(mask, s, NEG)
    m_new = jnp.maximum(m, jnp.max(s, axis=-1, keepdims=True))
    alpha = jnp.exp(m - m_new)
    p = jnp.where(mask, jnp.exp(s - m_new), 0.0)
    l = alpha * l + jnp.sum(p, axis=-1, keepdims=True)
    acc = alpha * acc + _dot(p.astype(BF16), v)
    return m_new, l, acc


def _off_mask(keep):
    return jnp.where(keep, 0.0, NEG)


def _softmax_update_add(s, m, l, acc, v, feature_major=False):
    m_new = jnp.maximum(m, jnp.max(s, axis=-1, keepdims=True))
    alpha = jnp.exp(m - m_new)
    p = jnp.exp(s - m_new)
    l = alpha * l + jnp.sum(p, axis=-1, keepdims=True)
    pb = p.astype(BF16)
    acc = alpha * acc + (_dot_nt(pb, v) if feature_major else _dot(pb, v))
    return m_new, l, acc


def _softmax_step(s, m, acc, v1):
    m_new = jnp.maximum(m, jnp.max(s, axis=-1, keepdims=True))
    p = jnp.exp(s - m_new)
    return m_new, jnp.exp(m - m_new) * acc + _dot(p.astype(BF16), v1)


def _softmax_ratio(acc, sum_high):
    c = HEAD_DIM if sum_high else 0
    return acc / jnp.maximum(acc[:, c:c + 1], 1e-30)


def _softmax_init(R, W):
    return jnp.full((R, 1), NEG, F32), jnp.zeros((R, 1), F32), jnp.zeros((R, W), F32)


def _softmax_finish(l, acc):
    return acc / jnp.maximum(l, 1e-30)


def _top_blocks(score, n_real, n_top, axis=1):
    jidx = _iota(score.shape, axis)
    rank = jnp.zeros(score.shape, F32)
    for i in range(n_real):
        si = score[:, i:i + 1] if axis == 1 else score[i:i + 1, :]
        beats = (si > score) | ((si == score) & (jidx > i))
        rank = rank + jnp.where(beats, 1.0, 0.0)
    return jnp.where(rank < n_top, 1.0, 0.0)


def _block_scores(imp, q_pos, n_sel, axis=1):
    j = _iota(imp.shape, axis)
    cur = q_pos >> _log2(SLC_BLOCK)
    valid = j * SLC_BLOCK <= q_pos
    forced = valid & ((j == 0) | (j == cur) | (j == cur - 1))
    score = jnp.where(forced, FORCE, jnp.where(valid, imp, -FORCE))
    return jnp.where(j < n_sel, score, -3e38)


def _block_of_key_01(n_blocks, tk, pos0):
    j = _iota((n_blocks, tk), 0)
    c = _iota((n_blocks, tk), 1)
    return jnp.where(((pos0 + c) >> _log2(SLC_BLOCK)) == j, 1.0, 0.0).astype(BF16)


def _half_to(x, src_high, dst_high):
    return x if src_high == dst_high else pltpu.roll(x, HEAD_DIM, 1)


def _nsa_prompt_kernel(q_ref, g_ref, kc_ref, vc_ref, ks_ref, vs_ref, kw_ref, vw_ref, bd_ref, bc_ref, covt_ref,
                       o_ref, woff_sc, *, tq, n_cmp, n_sel, n_top, group):
    i = pl.program_id(2)
    G = group
    R = G * tq
    nt = tq // LANES
    JP = _round_up(n_sel, 8)
    n_win = WINDOW // tq
    low = _iota((tq, LANES), 1) < HEAD_DIM

    @pl.when(i == 0)
    def _():
        dist0 = _iota((tq, tq), 0) - _iota((tq, tq), 1)
        for dd in range(n_win + 1):
            woff_sc[dd] = _off_mask((dist0 + dd * tq >= 0) & (dist0 + dd * tq < WINDOW))

    def tile_off(dd):
        return jnp.concatenate([woff_sc[dd]] * G, axis=0)

    for hh in range(2):
        own = low if hh == 0 else jnp.logical_not(low)
        pieces = []
        for g in range(G):
            x = q_ref[0, :, (hh * 2 + g // 2) * LANES:(hh * 2 + g // 2 + 1) * LANES] * (HEAD_DIM ** -0.5)
            pieces.append(_half_to(x, g % 2 == 1, hh == 1))
        qs = jnp.concatenate([jnp.where(own, x, 0.0) for x in pieces], axis=0).astype(BF16)

        def values(ref, start):
            return jnp.where(own, ref[0, pl.ds(start, tq), :], 1.0).astype(BF16)

        def dense_bias(d):
            blocks = []
            for g in range(G):
                for a in range(nt):
                    blocks.append(jnp.concatenate(
                        [bd_ref[jnp.maximum(nt * d + a - b, 0), hh * G + g] for b in range(nt)], axis=1))
            return jnp.concatenate(blocks, axis=0)

        bias_c = jnp.concatenate([bc_ref[a, hh * G + g] for g in range(G) for a in range(nt)], axis=0)
        s = _dot_nt(qs, kc_ref[0].astype(BF16)) + bias_c
        rowc = _iota(s.shape, 0) & (tq - 1)
        colc = _iota(s.shape, 1)
        dist_c = i * tq + rowc - (colc * CMP_STRIDE + (CMP_BLOCK - 1))
        p_c = _masked_softmax(s, (dist_c >= 0) & (colc < n_cmp)).astype(BF16)
        o_c = _dot(p_c, vc_ref[0].astype(BF16))
        covt = covt_ref[0:JP, :]
        impt = _dot_nt(covt, p_c[0:tq])
        for g in range(1, G):
            impt = impt + _dot_nt(covt, p_c[g * tq:(g + 1) * tq])
        q_pos = i * tq + _iota(impt.shape, 1)
        selt = _top_blocks(_block_scores(impt, q_pos, n_sel, 0), n_sel, n_top, 0)
        selt = jnp.concatenate([selt, jnp.zeros((LANES - JP, tq), F32)], axis=0)
        sel = selt.T

        sel_off = _half_to(_off_mask(sel > 0.5), False, hh == 0)
        qsel = jnp.concatenate([jnp.where(own, x, sel_off) for x in pieces], axis=0).astype(BF16)
        spare_block = _iota((tq, LANES), 1) - (HEAD_DIM if hh == 0 else 0)
        key_row = _iota((tq, LANES), 0)

        def slc_tile(j, carry, diag):
            start = pl.multiple_of(j * tq, tq)
            in_block = ((start + key_row) >> _log2(SLC_BLOCK)) == spare_block
            kt = jnp.where(own, ks_ref[0, pl.ds(start, tq), :], jnp.where(in_block, 1.0, 0.0)).astype(BF16)
            s = _dot_nt(qsel, kt) + dense_bias(i - j)
            if diag:
                s = s + tile_off(0)
            return _softmax_step(s, *carry, values(vs_ref, start))

        init = (jnp.full((R, 1), NEG, F32), jnp.zeros((R, LANES), F32))
        carry = lax.fori_loop(0, i, lambda j, c: slc_tile(j, c, False), init)
        carry = lax.fori_loop(i, i + 1, lambda j, c: slc_tile(j, c, True), carry)
        o_s = _softmax_ratio(carry[1], hh == 0)

        def win_tile(dd, carry):
            start = pl.multiple_of((i - dd) * tq, tq)
            kt = kw_ref[0, pl.ds(start, tq), :].astype(BF16)
            s = _dot_nt(qs, kt) + dense_bias(dd) + tile_off(dd)
            return _softmax_step(s, *carry, values(vw_ref, start))

        carry = lax.fori_loop(0, jnp.minimum(i, n_win) + 1, win_tile, init)
        o_w = _softmax_ratio(carry[1], hh == 0)

        def gate_col(br):
            return jnp.concatenate([g_ref[0, 0, :, hh * 3 * G + g * 3 + br:hh * 3 * G + g * 3 + br + 1]
                                    for g in range(G)], axis=0)

        o = gate_col(0) * o_c + gate_col(1) * o_s + gate_col(2) * o_w
        for vc in range(G // 2):
            lo_piece = _half_to(o[(2 * vc) * tq:(2 * vc + 1) * tq], hh == 1, False)
            hi_piece = _half_to(o[(2 * vc + 1) * tq:(2 * vc + 2) * tq], hh == 1, True)
            o_ref[0, :, (hh * 2 + vc) * LANES:(hh * 2 + vc + 1) * LANES] = jnp.where(low, lo_piece, hi_piece)


def nsa_prompt(q, gates, kcmp, vcmp, ks, vs, kw, vw, bias_dense, bias_cmp, cover, n_cmp, n_sel):
    B, L, WQ = q.shape
    WK = ks.shape[2]
    group = WQ // WK
    tq = 256
    nt = tq // LANES
    assert n_sel <= HEAD_DIM
    n_pairs = WK // LANES
    n_top = min(SLC_TOPN, n_sel)
    qw = WQ // n_pairs
    hp = bias_dense.shape[1] // n_pairs
    kv = pl.BlockSpec((1, L, LANES), lambda b, p, i: (b, 0, p))
    cm = pl.BlockSpec((1, kcmp.shape[1], LANES), lambda b, p, i: (b, 0, p))
    return pl.pallas_call(
        functools.partial(_nsa_prompt_kernel, tq=tq, n_cmp=n_cmp, n_sel=n_sel, n_top=n_top, group=group),
        grid=(B, n_pairs, L // tq),
        in_specs=[
            pl.BlockSpec((1, tq, qw), lambda b, p, i: (b, i, p)),
            pl.BlockSpec((1, 1, tq, LANES), lambda b, p, i: (p, b, i, 0)),
            cm, cm, kv, kv, kv, kv,
            pl.BlockSpec((bias_dense.shape[0], hp, LANES, LANES), lambda b, p, i: (0, p, 0, 0)),
            pl.BlockSpec((nt, hp, LANES, LANES), lambda b, p, i: (i, p, 0, 0)),
            _const_spec(cover.shape),
        ],
        out_specs=pl.BlockSpec((1, tq, qw), lambda b, p, i: (b, i, p)),
        out_shape=jax.ShapeDtypeStruct(q.shape, F32),
        scratch_shapes=[pltpu.VMEM((WINDOW // tq + 1, tq, tq), F32)],
        compiler_params=_cparams(("parallel", "parallel", "arbitrary")),
        name="nsa_prompt",
    )(q, gates, kcmp, vcmp, ks, vs, kw, vw, bias_dense, bias_cmp, cover)


def _nsa_sample_kernel(pt_ref, q_ref, g_ref, kc_ref, vc_ref, ksn_ref, vsn_ref, kwp_ref, vwp_ref,
                       kwn_ref, vwn_ref, bs_ref, bn_ref, bc_ref, bw_ref, cov_ref, e_ref, *rest,
                       n_tok, q_base, n_cmp, n_sel, n_top, group, pages):
    k_refs, v_refs = rest[:pages], rest[pages:2 * pages]
    o_ref, qs_sc, sel_sc, oc_sc, m_sc, l_sc, acc_sc = rest[2 * pages:]
    ci = pl.program_id(1)
    R, W = qs_sc.shape
    n_heads = R // n_tok
    J = sel_sc.shape[1]

    def tok(shape):
        return _iota(shape, 0) & (n_tok - 1)

    @pl.when(ci == 0)
    def _():
        blocks = []
        for h in range(n_heads):
            hkv = h // group
            x = q_ref[0, :, (h // 2) * LANES:(h // 2 + 1) * LANES] * (HEAD_DIM ** -0.5)
            x = _half_to(x, h % 2 == 1, hkv % 2 == 1)
            keep = (_iota(x.shape, 1) >= HEAD_DIM) if hkv % 2 == 1 else (_iota(x.shape, 1) < HEAD_DIM)
            x = jnp.where(keep, x, 0.0)
            blocks.append(jnp.concatenate(
                [x if cc == hkv // 2 else jnp.zeros_like(x) for cc in range(W // LANES)], axis=1))
        qs = jnp.concatenate(blocks, axis=0).astype(BF16)
        qs_sc[...] = qs
        s = _dot_nt(qs, kc_ref[0].astype(BF16)) + bc_ref[...]
        n = _iota(s.shape, 1)
        dist_c = q_base + tok(s.shape) - (n * CMP_STRIDE + (CMP_BLOCK - 1))
        p_c = _masked_softmax(s, (dist_c >= 0) & (n < n_cmp)).astype(BF16)
        oc_sc[...] = _dot(p_c, vc_ref[0].astype(BF16))
        ic = _dot(p_c, cov_ref[...])
        rows = group * n_tok
        imp_blocks = []
        for hkv in range(n_heads // group):
            a = ic[hkv * rows:hkv * rows + n_tok]
            for g in range(1, group):
                a = a + ic[hkv * rows + g * n_tok:hkv * rows + (g + 1) * n_tok]
            imp_blocks += [a] * group
        imp = jnp.concatenate(imp_blocks, axis=0)
        sel_sc[...] = _top_blocks(_block_scores(imp, q_base + tok(imp.shape), n_sel), n_sel, n_top)
        m0, l0, a0 = _softmax_init(R, W)
        m_sc[...] = m0
        l_sc[...] = l0
        acc_sc[...] = a0

    qs = qs_sc[...]
    sel = sel_sc[...].astype(BF16)

    kt = jnp.concatenate([r[0, 0] for r in k_refs], axis=1).astype(BF16)
    vt = jnp.concatenate([r[0, 0] for r in v_refs], axis=1).astype(BF16)
    s = _dot(qs, kt) + bs_ref[...] + _off_mask(_dot(sel, e_ref[...]) > 0.5)
    m, l, acc = _softmax_update_add(s, m_sc[...], l_sc[...], acc_sc[...], vt, True)
    m_sc[...] = m
    l_sc[...] = l
    acc_sc[...] = acc

    @pl.when(ci == pl.num_programs(1) - 1)
    def _():
        tn = ksn_ref.shape[1]
        c = _iota((R, tn), 1)
        causal_new = c <= tok((R, tn))
        s = _dot_nt(qs, ksn_ref[0].astype(BF16)) + bn_ref[...]
        chosen = _dot(sel, _block_of_key_01(J, tn, q_base))
        _, l1, a1 = _softmax_update_add(s + _off_mask((chosen > 0.5) & causal_new), m, l, acc,
                                        vsn_ref[0].astype(BF16))
        o_s = _softmax_finish(l1, a1)
        s = _dot(qs, kwp_ref[0, 0].astype(BF16)) + bw_ref[...]
        dist = WINDOW + tok(s.shape) - _iota(s.shape, 1)
        cw = _softmax_update_add(s + _off_mask((dist >= 0) & (dist < WINDOW)), *_softmax_init(R, W),
                                 vwp_ref[0, 0].astype(BF16), True)
        s = _dot_nt(qs, kwn_ref[0].astype(BF16)) + bn_ref[...]
        _, l2, a2 = _softmax_update_add(s + _off_mask(causal_new), *cw, vwn_ref[0].astype(BF16))
        o_w = _softmax_finish(l2, a2)

        def gate_col(br):
            cols = []
            for h in range(n_heads):
                hkv, g = divmod(h, group)
                cidx = (hkv % 2) * 3 * group + g * 3 + br
                cols.append(g_ref[hkv // 2, 0, :, cidx:cidx + 1])
            return jnp.concatenate(cols, axis=0)

        o = gate_col(0) * oc_sc[...] + gate_col(1) * o_s + gate_col(2) * o_w
        low = _iota((n_tok, LANES), 1) < HEAD_DIM
        for oc in range(n_heads // 2):
            hkv = (2 * oc) // group
            src = slice((hkv // 2) * LANES, (hkv // 2 + 1) * LANES)
            lo_piece = _half_to(o[(2 * oc) * n_tok:(2 * oc + 1) * n_tok, src], hkv % 2 == 1, False)
            hi_piece = _half_to(o[(2 * oc + 1) * n_tok:(2 * oc + 2) * n_tok, src], hkv % 2 == 1, True)
            o_ref[0, :, oc * LANES:(oc + 1) * LANES] = jnp.where(low, lo_piece, hi_piece)


def nsa_sample(q, gates, kcmp, vcmp, pool_k, pool_v, win_k, win_v, layer, page_table, ks_new, vs_new,
               kw_new, vw_new, bias_slc, bias_new, bias_cmp, bias_win, cover, n_cmp, n_sel):
    B, n_tok, WQ = q.shape
    WK, page = pool_k.shape[2], pool_k.shape[3]
    n_pages = page_table.shape[1]
    past = n_pages * page
    group = WQ // WK
    n_heads = WQ // HEAD_DIM
    R = n_heads * n_tok
    J = cover.shape[1]
    pages = PAGE_GROUP
    chunk = pages * page
    block_of_key = (jnp.arange(past)[None, :] // SLC_BLOCK == jnp.arange(J)[:, None]).astype(BF16)

    def bspec(a):
        return pl.BlockSpec((1,) + a.shape[1:], lambda b, c, pt: (b,) + (0,) * (a.ndim - 1))

    def cspec(a):
        return pl.BlockSpec(a.shape, lambda b, c, pt: (0,) * a.ndim, pipeline_mode=pl.Buffered(1))

    def page_spec(kk):
        return pl.BlockSpec((1, 1, WK, page), lambda b, c, pt: (layer, pt[b, c * pages + kk], 0, 0))

    wspec = pl.BlockSpec((1, 1, WK, win_k.shape[3]), lambda b, c, pt: (layer, b, 0, 0))
    grid_spec = pltpu.PrefetchScalarGridSpec(
        num_scalar_prefetch=1,
        grid=(B, n_pages // pages),
        in_specs=[
            bspec(q),
            pl.BlockSpec((gates.shape[0], 1, n_tok, LANES), lambda b, c, pt: (0, b, 0, 0)),
            bspec(kcmp), bspec(vcmp), bspec(ks_new), bspec(vs_new), wspec, wspec, bspec(kw_new), bspec(vw_new),
            pl.BlockSpec((R, chunk), lambda b, c, pt: (0, c)),
            cspec(bias_new), cspec(bias_cmp), cspec(bias_win), cspec(cover),
            pl.BlockSpec((J, chunk), lambda b, c, pt: (0, c)),
        ] + [page_spec(kk) for kk in range(pages)] * 2,
        out_specs=bspec(q),
        scratch_shapes=[pltpu.VMEM((R, WK), BF16), pltpu.VMEM((R, J), F32), pltpu.VMEM((R, WK), F32),
                        pltpu.VMEM((R, 1), F32), pltpu.VMEM((R, 1), F32), pltpu.VMEM((R, WK), F32)],
    )
    return pl.pallas_call(
        functools.partial(_nsa_sample_kernel, n_tok=n_tok, q_base=past, n_cmp=n_cmp, n_sel=n_sel,
                          n_top=min(SLC_TOPN, n_sel), group=group, pages=pages),
        grid_spec=grid_spec,
        out_shape=jax.ShapeDtypeStruct(q.shape, F32),
        compiler_params=_cparams(("parallel", "arbitrary")),
        name="nsa_sample",
    )(page_table, q, gates, kcmp, vcmp, ks_new, vs_new, win_k, win_v, kw_new, vw_new,
      bias_slc, bias_new, bias_cmp, bias_win, cover, block_of_key, *([pool_k] * pages), *([pool_v] * pages))


def _block_diag_ones(width):
    h = jnp.arange(width) // HEAD_DIM
    return (h[:, None] == h[None, :]).astype(BF16)


def _cover_01(n_cmp, n_sel, rows, cols):
    ci = jnp.arange(rows) * CMP_STRIDE
    sj = jnp.arange(cols) * SLC_BLOCK
    hit = (ci[:, None] < sj[None, :] + SLC_BLOCK) & (ci[:, None] + CMP_BLOCK > sj[None, :])
    hit = hit & (jnp.arange(rows)[:, None] < n_cmp) & (jnp.arange(cols)[None, :] < n_sel)
    return hit.astype(BF16)


def _round_up(n, m):
    return -(-n // m) * m


def _compress_weights(w1, b1, w2, b2, n_kv):
    hidden = w1.shape[1]
    eye = jnp.eye(n_kv, dtype=F32)
    w1r = w1.reshape(CMP_BLOCK // CMP_STRIDE, CMP_STRIDE, HEAD_DIM, hidden)
    rk = 2
    w1e = jnp.einsum('mrde,hk->mrhdke', w1r, eye).reshape(
        CMP_BLOCK // CMP_STRIDE, CMP_STRIDE // rk, rk * n_kv * HEAD_DIM, n_kv * hidden).astype(BF16)
    w2e = jnp.einsum('ed,hk->hekd', w2, eye).reshape(n_kv * hidden, n_kv * HEAD_DIM).astype(BF16)
    return w1e, jnp.tile(b1, n_kv)[None, :], w2e, jnp.tile(b2, n_kv)[None, :]


def _trunk(x, mods, past, P):
    B, L, D = x.shape
    depth = P['norm_g'].shape[0]
    even_states, odd_states = [], []
    for li in range(depth):
        def m(k):
            return mods[li, :, k][:, None, :]

        def g(k):
            return P['norm_g'][li, k][None, :]

        x = ffn(x, m(0), m(1), m(2), g(0), P['ffn_w_in'][li, 0], P['ffn_w_out'][li, 0])
        if li % 2 == 0:
            e = li // 2
            w = P['even_w_in'][e]
            n_sb = P['sb_width']
            HK, HV = P['gla_hk'], P['gla_hv']
            offs = [0, n_sb, 2 * n_sb, 3 * n_sb, 3 * n_sb + HK, 3 * n_sb + 2 * HK, 3 * n_sb + 2 * HK + HV,
                    3 * n_sb + 2 * HK + 2 * HV]
            ws = [w[:, offs[k]:offs[k + 1]] for k in range(7)]
            rank = w.shape[1] - offs[7]
            wa1 = jnp.pad(w[:, offs[7]:], ((0, 0), (0, LANES - rank)))
            wa2 = jnp.pad(P['gla_w_a2'][e], ((0, LANES - rank), (0, 0)))
            qa, ka, va, qb, kb, vb, rb, la = even_proj(x, m(3), m(4), g(1), ws, wa1, wa2, P['gla_b_a2'][e][None, :])
            dv = P['gla_onorm_g'].shape[1]
            n_gh = HV // dv
            dk = HK // n_gh
            if past is None:
                o_a = sb_prompt(qa, ka, va)
                s0t = jnp.zeros((B, dv, HK), F32)
            else:
                o_a = sb_sample(qa, ka, va, past['sb_k'], past['sb_v'], e, past['page_table'])
                s0t = past['gla'][e].transpose(0, 3, 1, 2).reshape(B, dv, HK)
            o_b, st = gla(qb, kb, vb, rb, la, s0t, P['gla_onorm_g'][e][None, :])
            wo = P['even_w_out'][e]
            x = out_proj(x, m(5), [o_a, o_b], [wo[:n_sb], wo[n_sb:]])
            s_fin = st.reshape(B, dv, n_gh, dk).transpose(0, 2, 3, 1)
            even_states.append((ka.reshape(B, L, -1, HEAD_DIM), va.reshape(B, L, -1, HEAD_DIM), s_fin))
        else:
            o_i = li // 2
            w = P['odd_w_in'][o_i]
            WQ, WK = P['nsa_wq'], P['nsa_wk']
            n_kv = WK // HEAD_DIM
            group = WQ // WK
            offs = [0, WQ] + [WQ + (k + 1) * WK for k in range(6)]
            ws = [w[:, offs[k]:offs[k + 1]] for k in range(7)]
            wg = w[:, offs[7]:]
            half = wg.shape[1] // 2
            ws += [jnp.pad(wg[:, :half], ((0, 0), (0, LANES - half))),
                   jnp.pad(wg[:, half:], ((0, 0), (0, LANES - half)))]
            qk_g = P['nsa_qk_g'][o_i]
            bdq, bdk = _block_diag_ones(WQ), _block_diag_ones(WK)
            consts = [bdq, bdk, jnp.tile(qk_g[0], WQ // HEAD_DIM)[None, :],
                      jnp.tile(qk_g[2], n_kv)[None, :], jnp.tile(qk_g[3], n_kv)[None, :]]
            q, kc, vc, ks, vs, kw, vw, gates = odd_proj(x, m(3), m(4), g(1), ws, consts)
            pair = LANES // HEAD_DIM
            gk = jnp.tile(qk_g[1], pair)[None, :]
            bdp = _block_diag_ones(LANES)
            cw = [_compress_weights(P['cmp_w1'][o_i, t], P['cmp_b1'][o_i, t], P['cmp_w2'][o_i, t],
                                    P['cmp_b2'][o_i, t], pair) for t in range(2)]
            rel = P['rel_table']
            if past is None:
                n_rows = (L // CMP_STRIDE) * CMP_STRIDE
                assert n_rows == L and L % LANES == 0
                n_cmp = L // CMP_STRIDE - CMP_BLOCK // CMP_STRIDE + 1
                n_sel = -(-L // SLC_BLOCK)
                kcmp = compress(kc, n_rows, *cw[0], bdp, gk, True)
                vcmp = compress(vc, n_rows, *cw[1], bdp, gk, False)
                assert kcmp.shape[1] == LANES and n_sel <= LANES
                nq = L // LANES
                bias_dense = bias_tiles(rel, jnp.arange(nq, dtype=jnp.int32) * LANES, LANES, LANES, 1)
                bias_cmp = bias_tiles(rel, jnp.arange(nq, dtype=jnp.int32) * LANES - (CMP_BLOCK - 1),
                                      LANES, LANES, CMP_STRIDE)
                cover = _cover_01(n_cmp, n_sel, LANES, LANES).T
                o = nsa_prompt(q, gates, kcmp, vcmp, ks, vs, kw, vw, bias_dense, bias_cmp, cover, n_cmp, n_sel)
                w_keep = min(WINDOW, L)
                win_k, win_v = kw[:, L - w_keep:], vw[:, L - w_keep:]
            else:
                pt = past['page_table']
                past_len = pt.shape[1] * past['cmp_k'].shape[3]
                T = past_len + L
                n_rows = (T // CMP_STRIDE) * CMP_STRIDE
                assert n_rows == past_len and past['win_k'].shape[3] == WINDOW and L <= LANES
                n_cmp = n_rows // CMP_STRIDE - CMP_BLOCK // CMP_STRIDE + 1
                n_sel = -(-T // SLC_BLOCK)
                kcmp = compress_paged(past['cmp_k'], o_i, pt, *cw[0], bdp, gk, True)
                vcmp = compress_paged(past['cmp_v'], o_i, pt, *cw[1], bdp, gk, False)
                J = _round_up(n_sel, LANES)
                n_heads = WQ // HEAD_DIM
                R = n_heads * L

                def sample_bias(base, C, cstep):
                    return bias_tiles(rel, jnp.array([base], jnp.int32), L, C, cstep).reshape(R, C)

                bias_slc = sample_bias(past_len, past_len, 1)
                bias_new = sample_bias(0, LANES, 1)
                bias_cmp = sample_bias(past_len - (CMP_BLOCK - 1), kcmp.shape[1], CMP_STRIDE)
                bias_win = sample_bias(WINDOW, WINDOW, 1)
                cover = _cover_01(n_cmp, n_sel, kcmp.shape[1], J)
                pad = ((0, 0), (0, LANES - L), (0, 0))
                o = nsa_sample(q, gates, kcmp, vcmp, past['slc_k'], past['slc_v'], past['win_k'], past['win_v'],
                               o_i, pt, jnp.pad(ks, pad), jnp.pad(vs, pad), jnp.pad(kw, pad), jnp.pad(vw, pad),
                               bias_slc, bias_new, bias_cmp, bias_win, cover, n_cmp, n_sel)
                win_k = jnp.concatenate([past['win_k_rows'][o_i], kw], axis=1)[:, L:]
                win_v = jnp.concatenate([past['win_v_rows'][o_i], vw], axis=1)[:, L:]
            x = out_proj(x, m(5), [o], [P['odd_w_out'][o_i]])

            def rows(t):
                return t.reshape(B, t.shape[1], n_kv, HEAD_DIM)

            odd_states.append(tuple(rows(t) for t in (kc, vc, ks, vs, win_k, win_v)))
        x = ffn(x, m(6), m(7), m(8), g(2), P['ffn_w_in'][li, 1], P['ffn_w_out'][li, 1])
    even_new = [jnp.stack([s[i] for s in even_states]) for i in range(3)]
    odd_new = [jnp.stack([s[i] for s in odd_states]) for i in range(6)]
    return x, even_new, odd_new


def kernel(x_prompt, x_sample, cache_sb_k, cache_sb_v, state_gla, cache_cmp_k, cache_cmp_v, cache_slc_k,
           cache_slc_v, cache_win_k, cache_win_v, page_table, c_prompt, c_sample, norm_g, ada_w, ada_b,
           ffn_w_in, ffn_w_out, even_w_in, gla_w_a2, gla_b_a2, gla_onorm_g, even_w_out, odd_w_in, nsa_qk_g,
           cmp_w1, cmp_b1, cmp_w2, cmp_b2, rel_table, odd_w_out):
    D = x_prompt.shape[-1]
    depth = norm_g.shape[0]
    Bp, Bs = x_prompt.shape[0], x_sample.shape[0]
    mods = ada_mod(jnp.concatenate([c_prompt, c_sample], axis=0), ada_w, ada_b).reshape(depth, Bp + Bs, 9, D)
    n_sb = cache_sb_k.shape[3] * cache_sb_k.shape[4]
    dv = gla_onorm_g.shape[1]
    HK = gla_w_a2.shape[2]
    n_gh = state_gla.shape[2]
    WK = cache_cmp_k.shape[3] * cache_cmp_k.shape[4]
    P = {
        'norm_g': norm_g, 'ffn_w_in': ffn_w_in.astype(BF16), 'ffn_w_out': ffn_w_out.astype(BF16),
        'even_w_in': even_w_in.astype(BF16), 'gla_w_a2': gla_w_a2.astype(BF16), 'gla_b_a2': gla_b_a2,
        'gla_onorm_g': gla_onorm_g, 'even_w_out': even_w_out.astype(BF16), 'odd_w_in': odd_w_in.astype(BF16),
        'nsa_qk_g': nsa_qk_g, 'cmp_w1': cmp_w1, 'cmp_b1': cmp_b1, 'cmp_w2': cmp_w2, 'cmp_b2': cmp_b2,
        'rel_table': rel_table, 'odd_w_out': odd_w_out.astype(BF16),
        'sb_width': n_sb, 'gla_hk': HK, 'gla_hv': n_gh * dv, 'nsa_wq': odd_w_out.shape[1], 'nsa_wk': WK,
    }

    def pool(c):
        return c.transpose(0, 1, 3, 4, 2).reshape(c.shape[0], c.shape[1], c.shape[3] * c.shape[4], c.shape[2])

    def rows(c):
        return c.reshape(c.shape[0], c.shape[1], c.shape[2], c.shape[3] * c.shape[4])

    past = {
        'page_table': page_table, 'sb_k': pool(cache_sb_k), 'sb_v': pool(cache_sb_v), 'gla': state_gla,
        'cmp_k': pool(cache_cmp_k), 'cmp_v': pool(cache_cmp_v), 'slc_k': pool(cache_slc_k),
        'slc_v': pool(cache_slc_v), 'win_k': pool(cache_win_k), 'win_v': pool(cache_win_v),
        'win_k_rows': rows(cache_win_k), 'win_v_rows': rows(cache_win_v),
    }
    y_p, p_even, p_odd = _trunk(x_prompt, mods[:, :Bp], None, P)
    y_s, s_even, s_odd = _trunk(x_sample, mods[:, Bp:], past, P)
    return (y_p, y_s, *p_even, *p_odd, *s_even, *s_odd)
```

```python
import functools
import math

import jax
import jax.numpy as jnp
from jax import lax
from jax.experimental import pallas as pl
from jax.experimental.pallas import tpu as pltpu

F32 = jnp.float32
BF16 = jnp.bfloat16

EPS = 1e-6
NEG = -1e30
FORCE = 1e6
HEAD_DIM = 64
GLA_TAU = 16.0
GLA_CHUNK = 64
CMP_BLOCK = 32
CMP_STRIDE = 16
SLC_BLOCK = 64
SLC_TOPN = 16
WINDOW = 512
REL_BUCKETS = 32
REL_MAX_DIST = 1024
PAGE_GROUP = 16
LANES = 128
VMEM_LIMIT = 56 * 1024 * 1024


def _cparams(sem, vmem=VMEM_LIMIT):
    return pltpu.CompilerParams(dimension_semantics=sem, vmem_limit_bytes=vmem)


def _dot(a, b):
    return jnp.dot(a, b, preferred_element_type=F32)


def _dot_nt(a, b):
    return lax.dot_general(a, b, (((1,), (1,)), ((), ())), preferred_element_type=F32)


def _dot_tn(a, b):
    return lax.dot_general(a, b, (((0,), (0,)), ((), ())), preferred_element_type=F32)


def _split2(x):
    hi = x.astype(BF16)
    lo = (x - hi.astype(F32)).astype(BF16)
    return hi, lo


def _dot_f32_by01(x, m01):
    hi, lo = _split2(x)
    return _dot(hi, m01) + _dot(lo, m01)


def _dot_01_by_f32(m01, x):
    hi, lo = _split2(x)
    return _dot(m01, hi) + _dot(m01, lo)


def _softplus(z):
    return jnp.maximum(z, 0.0) + jnp.log1p(jnp.exp(-jnp.abs(z)))


LOG2E = 1.4426950408889634
QSCALE = HEAD_DIM ** -0.5 * LOG2E


def _sb_logs2(z2):
    l2 = jnp.log(1.0 + jnp.exp2(-jnp.abs(z2))) * LOG2E
    lk = -(jnp.maximum(z2, 0.0) + l2)
    return lk, z2 + lk


def _silu(a):
    return a * jax.nn.sigmoid(a)


def _iota(shape, dim):
    return lax.broadcasted_iota(jnp.int32, shape, dim)


def _modulate(x, g, shift, scale):
    ms = jnp.mean(x * x, axis=-1, keepdims=True)
    y = x * lax.rsqrt(ms + EPS) * g
    return y * (1.0 + scale) + shift


def _head_rms(x, bd, g):
    ms = _dot_f32_by01(x * x, bd) * (1.0 / HEAD_DIM)
    return x * lax.rsqrt(ms + EPS) * g


def _head_rms_wide(x, head_of, g):
    ms = _dot_f32_by01(x * x, head_of) * (1.0 / HEAD_DIM)
    inv = lax.rsqrt(ms + EPS)
    hi = inv.astype(BF16)
    r1 = inv - hi.astype(F32)
    mid = r1.astype(BF16)
    lo = (r1 - mid.astype(F32)).astype(BF16)
    inv_lanes = _dot_nt(hi, head_of) + _dot_nt(mid, head_of) + _dot_nt(lo, head_of)
    return x * inv_lanes * g


def _const_spec(shape):
    nd = len(shape)
    return pl.BlockSpec(shape, lambda *_: (0,) * nd, pipeline_mode=pl.Buffered(1))


def _row_tiles(B, L):
    if L >= 512:
        return 1, 512
    return B, L


def _ada_kernel(c_ref, w_ref, b_ref, o_ref):
    c = c_ref[...]
    a = _silu(c).astype(BF16)
    o_ref[0] = _dot(a, w_ref[0].astype(BF16)) + b_ref[0]


def ada_mod(c, ada_w, ada_b):
    R, D = c.shape
    depth, _, N = ada_w.shape
    tn = 1024
    return pl.pallas_call(
        _ada_kernel,
        grid=(depth, N // tn),
        in_specs=[
            pl.BlockSpec((R, D), lambda l, n: (0, 0)),
            pl.BlockSpec((1, D, tn), lambda l, n: (l, 0, n)),
            pl.BlockSpec((1, 1, tn), lambda l, n: (l, 0, n)),
        ],
        out_specs=pl.BlockSpec((1, R, tn), lambda l, n: (l, 0, n)),
        out_shape=jax.ShapeDtypeStruct((depth, R, N), F32),
        compiler_params=_cparams(("parallel", "parallel")),
        name="ada_mod",
    )(c, ada_w, ada_b.reshape(depth, 1, N))


def _ffn_kernel(x_ref, sh_ref, sc_ref, gt_ref, g_ref, win_ref, wout_ref, o_ref, *, d_ff, tf):
    x = x_ref[...]
    tb, tl, D = x.shape
    h = _modulate(x, g_ref[...], sh_ref[...], sc_ref[...])
    hb = h.reshape(tb * tl, D).astype(BF16)
    acc = jnp.zeros((tb * tl, D), F32)
    for j in range(d_ff // tf):
        a = _dot(hb, win_ref[:, j * tf:(j + 1) * tf])
        b = _dot(hb, win_ref[:, d_ff + j * tf:d_ff + (j + 1) * tf])
        u = (_silu(a) * b).astype(BF16)
        acc = acc + _dot(u, wout_ref[j * tf:(j + 1) * tf, :])
    o_ref[...] = x + 0.5 * gt_ref[...] * acc.reshape(tb, tl, D)


def ffn(x, shift, scale, gate, g, w_in, w_out):
    B, L, D = x.shape
    d_ff = w_out.shape[0]
    tb, tl = _row_tiles(B, L)
    xs = pl.BlockSpec((tb, tl, D), lambda b, l: (b, l, 0))
    ms = pl.BlockSpec((tb, 1, D), lambda b, l: (b, 0, 0))
    return pl.pallas_call(
        functools.partial(_ffn_kernel, d_ff=d_ff, tf=256),
        grid=(B // tb, L // tl),
        in_specs=[xs, ms, ms, ms, _const_spec((1, D)), _const_spec(w_in.shape), _const_spec(w_out.shape)],
        out_specs=xs,
        out_shape=jax.ShapeDtypeStruct(x.shape, F32),
        compiler_params=_cparams(("parallel", "parallel")),
        name="ffn",
    )(x, shift, scale, gate, g, w_in, w_out)


def _even_proj_kernel(x_ref, sh_ref, sc_ref, g_ref, wqa, wka, wva, wqb, wkb, wvb, wrb, wa1, wa2, ba2,
                      qa, ka, va, qb, kb, vb, rb, la):
    x = x_ref[...]
    tb, tl, D = x.shape
    h = _modulate(x, g_ref[...], sh_ref[...], sc_ref[...])
    hb = h.reshape(tb * tl, D).astype(BF16)
    for w, o in ((wqa, qa), (wka, ka), (wva, va), (wqb, qb), (wkb, kb), (wvb, vb), (wrb, rb)):
        o[...] = _dot(hb, w[...]).reshape(o.shape)
    a1 = _dot(hb, wa1[...]).astype(BF16)
    z = _dot(a1, wa2[...]) + ba2[...]
    la[...] = (-_softplus(-z) * (1.0 / GLA_TAU)).reshape(la.shape)


def even_proj(x, shift, scale, g, ws, wa1, wa2, ba2):
    B, L, D = x.shape
    tb, tl = _row_tiles(B, L)
    xs = pl.BlockSpec((tb, tl, D), lambda b, l: (b, l, 0))
    ms = pl.BlockSpec((tb, 1, D), lambda b, l: (b, 0, 0))
    widths = [w.shape[1] for w in ws] + [wa2.shape[1]]
    return pl.pallas_call(
        _even_proj_kernel,
        grid=(B // tb, L // tl),
        in_specs=[xs, ms, ms, _const_spec((1, D))] + [_const_spec(w.shape) for w in ws]
        + [_const_spec(wa1.shape), _const_spec(wa2.shape), _const_spec(ba2.shape)],
        out_specs=[pl.BlockSpec((tb, tl, n), lambda b, l: (b, l, 0)) for n in widths],
        out_shape=[jax.ShapeDtypeStruct((B, L, n), F32) for n in widths],
        compiler_params=_cparams(("parallel", "parallel")),
        name="even_proj",
    )(x, shift, scale, g, *ws, wa1, wa2, ba2)


def _odd_proj_kernel(x_ref, sh_ref, sc_ref, g_ref, wq, wkc, wvc, wks, wvs, wkw, wvw, wg0, wg1,
                     hq, hk, gq, gks, gkw,
                     q, kc, vc, ks, vs, kw, vw, gates):
    x = x_ref[...]
    tb, tl, D = x.shape
    h = _modulate(x, g_ref[...], sh_ref[...], sc_ref[...])
    hb = h.reshape(tb * tl, D).astype(BF16)
    q[...] = _head_rms_wide(_dot(hb, wq[...]), hq[...], gq[...]).reshape(q.shape)
    kc[...] = _dot(hb, wkc[...]).reshape(kc.shape)
    vc[...] = _dot(hb, wvc[...]).reshape(vc.shape)
    ks[...] = _head_rms_wide(_dot(hb, wks[...]), hk[...], gks[...]).reshape(ks.shape)
    vs[...] = _dot(hb, wvs[...]).reshape(vs.shape)
    kw[...] = _head_rms_wide(_dot(hb, wkw[...]), hk[...], gkw[...]).reshape(kw.shape)
    vw[...] = _dot(hb, wvw[...]).reshape(vw.shape)
    gates[0] = jax.nn.sigmoid(_dot(hb, wg0[...])).reshape(gates.shape[1:])
    gates[1] = jax.nn.sigmoid(_dot(hb, wg1[...])).reshape(gates.shape[1:])


def odd_proj(x, shift, scale, g, ws, consts):
    B, L, D = x.shape
    tb, tl = _row_tiles(B, L)
    xs = pl.BlockSpec((tb, tl, D), lambda b, l: (b, l, 0))
    ms = pl.BlockSpec((tb, 1, D), lambda b, l: (b, 0, 0))
    widths = [w.shape[1] for w in ws[:7]]
    out_specs = [pl.BlockSpec((tb, tl, n), lambda b, l: (b, l, 0)) for n in widths]
    out_specs.append(pl.BlockSpec((2, tb, tl, LANES), lambda b, l: (0, b, l, 0)))
    out_shape = [jax.ShapeDtypeStruct((B, L, n), F32) for n in widths]
    out_shape.append(jax.ShapeDtypeStruct((2, B, L, LANES), F32))
    return pl.pallas_call(
        _odd_proj_kernel,
        grid=(B // tb, L // tl),
        in_specs=[xs, ms, ms, _const_spec((1, D))] + [_const_spec(w.shape) for w in ws]
        + [_const_spec(c.shape) for c in consts],
        out_specs=out_specs,
        out_shape=out_shape,
        compiler_params=_cparams(("parallel", "parallel")),
        name="odd_proj",
    )(x, shift, scale, g, *ws, *consts)


def _out_proj_kernel(*refs, n_in):
    x_ref, gt_ref = refs[0], refs[1]
    a_refs = refs[2:2 + n_in]
    w_refs = refs[2 + n_in:2 + 2 * n_in]
    o_ref = refs[2 + 2 * n_in]
    x = x_ref[...]
    tb, tl, D = x.shape
    acc = jnp.zeros((tb * tl, D), F32)
    for a, w in zip(a_refs, w_refs):
        av = a[...]
        acc = acc + _dot(av.reshape(tb * tl, av.shape[-1]).astype(BF16), w[...])
    o_ref[...] = x + gt_ref[...] * acc.reshape(tb, tl, D)


def out_proj(x, gate, acts, ws):
    B, L, D = x.shape
    tb, tl = _row_tiles(B, L)
    xs = pl.BlockSpec((tb, tl, D), lambda b, l: (b, l, 0))
    ms = pl.BlockSpec((tb, 1, D), lambda b, l: (b, 0, 0))
    return pl.pallas_call(
        functools.partial(_out_proj_kernel, n_in=len(acts)),
        grid=(B // tb, L // tl),
        in_specs=[xs, ms] + [pl.BlockSpec((tb, tl, a.shape[-1]), lambda b, l: (b, l, 0)) for a in acts]
        + [_const_spec(w.shape) for w in ws],
        out_specs=xs,
        out_shape=jax.ShapeDtypeStruct(x.shape, F32),
        compiler_params=_cparams(("parallel", "parallel")),
        name="out_proj",
    )(x, gate, *acts, *ws)


def _log2(n):
    assert n > 0 and n & (n - 1) == 0, n
    return n.bit_length() - 1


def _later_keys_01(tk):
    return jnp.where(_iota((tk, tk), 0) > _iota((tk, tk), 1), 1.0, 0.0).astype(BF16)


def _sb_tile(q2, kt, vt, run, acc, u01, before, feature_major=False):
    lk, ls = _sb_logs2(_dot(q2, kt) if feature_major else _dot_nt(q2, kt))
    if before is not None:
        lk = jnp.where(before, lk, 0.0)
    cs = _dot_f32_by01(lk, u01)
    w = jnp.exp2(ls + cs + run)
    if before is not None:
        w = jnp.where(before, w, 0.0)
    wb = w.astype(BF16)
    acc = acc + (_dot_nt(wb, vt) if feature_major else _dot(wb, vt))
    run = run + cs[:, 0:1] + lk[:, 0:1]
    return run, acc


def _sb_prompt_kernel(q_ref, k_ref, v_ref, o_ref, *, tq):
    i = pl.program_id(2)
    q = q_ref[0] * QSCALE
    lane = _iota((tq, LANES), 1)
    q2 = jnp.concatenate([jnp.where(lane < HEAD_DIM, q, 0.0), jnp.where(lane >= HEAD_DIM, q, 0.0)],
                         axis=0).astype(BF16)
    u01 = _later_keys_01(tq)
    before = _iota((2 * tq, tq), 1) < (_iota((2 * tq, tq), 0) & (tq - 1))

    def tile(j, run, acc, mask):
        start = pl.multiple_of(j * tq, tq)
        kt = k_ref[0, pl.ds(start, tq), :].astype(BF16)
        vt = v_ref[0, pl.ds(start, tq), :].astype(BF16)
        return _sb_tile(q2, kt, vt, run, acc, u01, mask)

    run, acc = tile(i, jnp.zeros((2 * tq, 1), F32), jnp.zeros((2 * tq, LANES), F32), before)
    run, acc = lax.fori_loop(1, i + 1, lambda jj, c: tile(i - jj, c[0], c[1], None), (run, acc))
    o_ref[0] = jnp.where(lane < HEAD_DIM, acc[:tq], acc[tq:])


def sb_prompt(q, k, v):
    B, L, W = q.shape
    tq = 256
    qs = pl.BlockSpec((1, tq, LANES), lambda b, p, i: (b, i, p))
    ks = pl.BlockSpec((1, L, LANES), lambda b, p, i: (b, 0, p))
    return pl.pallas_call(
        functools.partial(_sb_prompt_kernel, tq=tq),
        grid=(B, W // LANES, L // tq),
        in_specs=[qs, ks, ks],
        out_specs=qs,
        out_shape=jax.ShapeDtypeStruct(q.shape, F32),
        compiler_params=_cparams(("parallel", "parallel", "arbitrary")),
        name="sb_prompt",
    )(q, k, v)


def _sb_sample_kernel(pt_ref, q_ref, kn_ref, vn_ref, *rest, n_tok, group):
    k_refs, v_refs = rest[:group], rest[group:2 * group]
    o_ref, run_ref, acc_ref = rest[2 * group:]
    s = pl.program_id(1)
    q = q_ref[0] * QSCALE
    W = q.shape[1]
    n_heads = W // HEAD_DIM
    R = n_heads * n_tok
    tk = kn_ref.shape[1]
    own = (_iota((R, W), 0) >> _log2(n_tok)) == (_iota((R, W), 1) >> _log2(HEAD_DIM))
    q2 = jnp.where(own, jnp.concatenate([q] * n_heads, axis=0), 0.0).astype(BF16)
    u01 = _later_keys_01(tk)

    @pl.when(s == 0)
    def _():
        before = _iota((R, tk), 1) < (_iota((R, tk), 0) & (n_tok - 1))
        run, acc = _sb_tile(q2, kn_ref[0].astype(BF16), vn_ref[0].astype(BF16),
                            jnp.zeros((R, 1), F32), jnp.zeros((R, W), F32), u01, before)
        run_ref[...] = run
        acc_ref[...] = acc

    @pl.when(s > 0)
    def _():
        kt = jnp.concatenate([r[0, 0] for r in k_refs], axis=1).astype(BF16)
        vt = jnp.concatenate([r[0, 0] for r in v_refs], axis=1).astype(BF16)

        def stack(x):
            return jnp.concatenate([x[:, kk * tk:(kk + 1) * tk] for kk in range(group)], axis=0)

        lk, ls = _sb_logs2(_dot(q2, kt))
        lk = stack(lk)
        cs = _dot_f32_by01(lk, u01)
        tot = cs[:, 0:1] + lk[:, 0:1]
        run = run_ref[...]
        runs = []
        for kk in range(group):
            runs.append(run)
            run = run + tot[kk * R:(kk + 1) * R]
        w = jnp.exp2(stack(ls) + cs + jnp.concatenate(runs, axis=0))
        w = jnp.concatenate([w[kk * R:(kk + 1) * R] for kk in range(group)], axis=1).astype(BF16)
        acc_ref[...] = acc_ref[...] + _dot_nt(w, vt)
        run_ref[...] = run

    @pl.when(s == pl.num_programs(1) - 1)
    def _():
        acc = jnp.where(own, acc_ref[...], 0.0)
        out = acc[0:n_tok]
        for h in range(1, n_heads):
            out = out + acc[h * n_tok:(h + 1) * n_tok]
        o_ref[0] = out


def sb_sample(q, k_new, v_new, pool_k, pool_v, layer, page_table):
    B, n_tok, W = q.shape
    page = pool_k.shape[3]
    n_pages = page_table.shape[1]
    group = PAGE_GROUP
    n_steps = 1 + n_pages // group
    pad = ((0, 0), (0, page - n_tok), (0, 0))
    kn, vn = jnp.pad(k_new, pad), jnp.pad(v_new, pad)

    def page_spec(kk):
        def imap(b, s, pt):
            return (layer, pt[b, n_pages - 1 - (jnp.maximum(s, 1) - 1) * group - kk], 0, 0)
        return pl.BlockSpec((1, 1, W, page), imap)

    qs = pl.BlockSpec((1, n_tok, W), lambda b, s, pt: (b, 0, 0))
    ns = pl.BlockSpec((1, page, W), lambda b, s, pt: (b, 0, 0))
    R = (W // HEAD_DIM) * n_tok
    grid_spec = pltpu.PrefetchScalarGridSpec(
        num_scalar_prefetch=1,
        grid=(B, n_steps),
        in_specs=[qs, ns, ns] + [page_spec(kk) for kk in range(group)] * 2,
        out_specs=qs,
        scratch_shapes=[pltpu.VMEM((R, 1), F32), pltpu.VMEM((R, W), F32)],
    )
    return pl.pallas_call(
        functools.partial(_sb_sample_kernel, n_tok=n_tok, group=group),
        grid_spec=grid_spec,
        out_shape=jax.ShapeDtypeStruct(q.shape, F32),
        compiler_params=_cparams(("parallel", "arbitrary")),
        name="sb_sample",
    )(page_table, q, kn, vn, *([pool_k] * group), *([pool_v] * group))


def _gla_kernel(q_ref, k_ref, v_ref, r_ref, la_ref, s0_ref, g_ref, o_ref, sfin_ref, st, *, dk, dv):
    c = pl.program_id(1)

    @pl.when(c == 0)
    def _():
        st[...] = s0_ref[...]

    for ib in range(q_ref.shape[0]):
        _gla_chunk(q_ref.at[ib], k_ref.at[ib], v_ref.at[ib], r_ref.at[ib], la_ref.at[ib], g_ref, o_ref.at[ib],
                   st.at[ib], dk=dk, dv=dv)

    @pl.when(c == pl.num_programs(1) - 1)
    def _():
        sfin_ref[...] = st[...]


def _gla_chunk(q_ref, k_ref, v_ref, r_ref, la_ref, g_ref, o_ref, st, *, dk, dv):
    q = q_ref[...] * (dk ** -0.5)
    k = k_ref[...]
    la = la_ref[...]
    C, HK = q.shape
    n_heads = HK // dk
    causal = _iota((C, C), 0) >= _iota((C, C), 1)
    ltri = jnp.where(causal, 1.0, 0.0).astype(BF16)
    hi = la.astype(BF16)
    r1 = la - hi.astype(F32)
    mid = r1.astype(BF16)
    lo = (r1 - mid.astype(F32)).astype(BF16)
    b = _dot(ltri, hi) + _dot(ltri, mid) + _dot(ltri, lo)
    b_last = b[C - 1:C, :]
    q_in = q * jnp.exp(b)
    k_in = (k * jnp.exp(-b)).astype(BF16)
    k_dec = k * jnp.exp(b_last - b)
    lane_h = _iota((C, HK), 1) >> _log2(dk)
    s_old = st[...]
    s_old_b = s_old.astype(BF16)
    upd = jnp.zeros(s_old.shape, F32)
    for h in range(n_heads):
        own = lane_h == h
        qh = jnp.where(own, q_in, 0.0).astype(BF16)
        att = jnp.where(causal, _dot_nt(qh, k_in), 0.0)
        vh = v_ref[:, h * dv:(h + 1) * dv].astype(BF16)
        oh = _dot(att.astype(BF16), vh) + _dot_nt(qh, s_old_b)
        y = oh * lax.rsqrt(jnp.mean(oh * oh, axis=-1, keepdims=True) + EPS) * g_ref[...]
        o_ref[:, h * dv:(h + 1) * dv] = y * _silu(r_ref[:, h * dv:(h + 1) * dv])
        kd = jnp.where(own, k_dec, 0.0).astype(BF16)
        upd = upd + _dot_tn(vh, kd)
    st[...] = s_old * jnp.exp(b_last) + upd


def gla(q, k, v, r, la, s0t, g):
    B, L, HK = q.shape
    HV = v.shape[2]
    dv = g.shape[1]
    dk = HK // (HV // dv)
    C = min(GLA_CHUNK, L)
    bb = next(n for n in (4, 2, 1) if B % n == 0)
    ks = pl.BlockSpec((bb, C, HK), lambda b, c: (b, c, 0))
    vs = pl.BlockSpec((bb, C, HV), lambda b, c: (b, c, 0))
    ss = pl.BlockSpec((bb, dv, HK), lambda b, c: (b, 0, 0))
    return pl.pallas_call(
        functools.partial(_gla_kernel, dk=dk, dv=dv),
        grid=(B // bb, L // C),
        in_specs=[ks, ks, vs, vs, ks, ss, pl.BlockSpec((1, dv), lambda b, c: (0, 0))],
        out_specs=[vs, ss],
        out_shape=[jax.ShapeDtypeStruct(v.shape, F32), jax.ShapeDtypeStruct(s0t.shape, F32)],
        scratch_shapes=[pltpu.VMEM((bb, dv, HK), F32)],
        compiler_params=_cparams(("parallel", "arbitrary")),
        name="gla",
    )(q, k, v, r, la, s0t, g)


def _gelu_tanh(x):
    return x * (0.5 * (1.0 + jnp.tanh(math.sqrt(2.0 / math.pi) * (x + 0.044715 * (x * x * x)))))


def _compress_mlp(seg_rows, w1_ref, b1_ref, w2_ref, b2_ref, bd_ref, g_ref, o_ref, *, n_pairs, nseg, norm):
    rk = w1_ref.shape[2] // LANES
    wh = w1_ref.shape[3]
    R = n_pairs * nseg
    p0 = jnp.zeros((R, wh), F32)
    p1 = jnp.zeros((R, wh), F32)
    for rr in range(CMP_STRIDE // rk):
        rows = jnp.concatenate(
            [jnp.concatenate([seg_rows(p, rr * rk + u) for u in range(rk)], axis=1) for p in range(n_pairs)],
            axis=0).astype(BF16)
        p0 = p0 + _dot(rows, w1_ref[0, rr])
        p1 = p1 + _dot(rows, w1_ref[1, rr])
    hid = b1_ref[...] + p0 + pltpu.roll(p1, R - 1, 0)
    y = _dot(_gelu_tanh(hid).astype(BF16), w2_ref[...]) + b2_ref[...]
    if norm:
        y = _head_rms(y, bd_ref[...], g_ref[...])
    y = jnp.where((_iota(y.shape, 0) & (nseg - 1)) < nseg - 1, y, 0.0)
    for p in range(n_pairs):
        o_ref[0, :, p * LANES:(p + 1) * LANES] = y[p * nseg:(p + 1) * nseg]


def _compress_kernel(*refs, n_pairs, nseg, norm):
    x_refs = refs[:n_pairs]
    _compress_mlp(lambda p, r: x_refs[p][0, pl.ds(r, nseg, stride=CMP_STRIDE), :], *refs[n_pairs:],
                  n_pairs=n_pairs, nseg=nseg, norm=norm)


def compress(x, n_rows, w1e, b1e, w2e, b2e, bd, g, norm):
    B, _, W = x.shape
    nseg = n_rows // CMP_STRIDE
    _log2(nseg)
    n_pairs = W // LANES
    return pl.pallas_call(
        functools.partial(_compress_kernel, n_pairs=n_pairs, nseg=nseg, norm=norm),
        grid=(B,),
        in_specs=[pl.BlockSpec((1, n_rows, LANES), functools.partial(lambda b, p: (b, 0, p), p=p))
                  for p in range(n_pairs)]
        + [_const_spec(a.shape) for a in (w1e, b1e, w2e, b2e, bd, g)],
        out_specs=pl.BlockSpec((1, nseg, W), lambda b: (b, 0, 0)),
        out_shape=jax.ShapeDtypeStruct((B, nseg, W), F32),
        compiler_params=_cparams(("parallel",)),
        name="compress",
    )(*([x] * n_pairs), w1e, b1e, w2e, b2e, bd, g)


def _compress_paged_kernel(pt_ref, *refs, group, n_pairs, nseg, norm):
    page_refs = refs[:group]
    xs = refs[-1]
    s = pl.program_id(1)
    page = page_refs[0].shape[3]
    for kk, pr in enumerate(page_refs):
        start = pl.multiple_of((s * group + kk) * page, page)
        for p in range(n_pairs):
            xs[p, pl.ds(start, page), :] = pr[0, 0, p * LANES:(p + 1) * LANES, :].T

    @pl.when(s == pl.num_programs(1) - 1)
    def _():
        _compress_mlp(lambda p, r: xs[p, pl.ds(r, nseg, stride=CMP_STRIDE), :], *refs[group:-1],
                      n_pairs=n_pairs, nseg=nseg, norm=norm)


def compress_paged(pool, layer, page_table, w1e, b1e, w2e, b2e, bd, g, norm):
    _, _, W, page = pool.shape
    B, n_pages = page_table.shape
    group = PAGE_GROUP
    n_rows = n_pages * page
    nseg = n_rows // CMP_STRIDE
    _log2(nseg)
    n_pairs = W // LANES

    def page_spec(kk):
        return pl.BlockSpec((1, 1, W, page), lambda b, s, pt: (layer, pt[b, s * group + kk], 0, 0))

    consts = (w1e, b1e, w2e, b2e, bd, g)
    grid_spec = pltpu.PrefetchScalarGridSpec(
        num_scalar_prefetch=1,
        grid=(B, n_pages // group),
        in_specs=[page_spec(kk) for kk in range(group)]
        + [pl.BlockSpec(a.shape, functools.partial(lambda b, s, pt, nd: (0,) * nd, nd=a.ndim),
                        pipeline_mode=pl.Buffered(1)) for a in consts],
        out_specs=pl.BlockSpec((1, nseg, W), lambda b, s, pt: (b, 0, 0)),
        scratch_shapes=[pltpu.VMEM((n_pairs, n_rows, LANES), F32)],
    )
    return pl.pallas_call(
        functools.partial(_compress_paged_kernel, group=group, n_pairs=n_pairs, nseg=nseg, norm=norm),
        grid_spec=grid_spec,
        out_shape=jax.ShapeDtypeStruct((B, nseg, W), F32),
        compiler_params=_cparams(("parallel", "arbitrary")),
        name="compress_paged",
    )(page_table, *([pool] * group), *consts)


_REL_EXACT = REL_BUCKETS // 2
_REL_THRESHOLDS = tuple(
    math.ceil(_REL_EXACT * (REL_MAX_DIST / _REL_EXACT) ** (k / (REL_BUCKETS - _REL_EXACT)) - 1e-9)
    for k in range(1, REL_BUCKETS - _REL_EXACT))


def _bias_kernel(base_ref, tbl_ref, o_ref, *, cstep):
    t = pl.program_id(0)
    _, H, R, C = o_ref.shape
    dist = base_ref[t] + _iota((R, C), 0) - cstep * _iota((R, C), 1)
    d = jnp.maximum(dist, 0)
    big = jnp.full((R, C), _REL_EXACT, jnp.int32)
    for th in _REL_THRESHOLDS:
        big = big + jnp.where(d >= th, 1, 0)
    bucket = jnp.where(d < _REL_EXACT, d, big)
    for h in range(H):
        acc = jnp.full((R, C), tbl_ref[h, REL_BUCKETS - 1], F32)
        for u in range(REL_BUCKETS - 1):
            acc = jnp.where(bucket == u, tbl_ref[h, u], acc)
        o_ref[0, h] = acc * LOG2E


def bias_tiles(rel_table, bases, R, C, cstep):
    H = rel_table.shape[1]
    T = bases.shape[0]
    grid_spec = pltpu.PrefetchScalarGridSpec(
        num_scalar_prefetch=1,
        grid=(T,),
        in_specs=[pl.BlockSpec(memory_space=pltpu.SMEM)],
        out_specs=pl.BlockSpec((1, H, R, C), lambda t, base: (t, 0, 0, 0)),
    )
    return pl.pallas_call(
        functools.partial(_bias_kernel, cstep=cstep),
        grid_spec=grid_spec,
        out_shape=jax.ShapeDtypeStruct((T, H, R, C), F32),
        compiler_params=_cparams(("arbitrary",)),
        name="bias_tiles",
    )(bases, rel_table.T)


def _masked_softmax(s, mask):
    s = jnp.where(mask, s, NEG)
    m = jnp.max(s, axis=-1, keepdims=True)
    p = jnp.where(mask, jnp.exp2(s - m), 0.0)
    return p * (1.0 / jnp.maximum(jnp.sum(p, axis=-1, keepdims=True), 1e-30))


def _off_mask(keep):
    return jnp.where(keep, 0.0, NEG)


def _softmax_update_add(s, m, l, acc, v, feature_major=False):
    m_new = jnp.maximum(m, jnp.max(s, axis=-1, keepdims=True))
    alpha = jnp.exp2(m - m_new)
    p = jnp.exp2(s - m_new)
    l = alpha * l + jnp.sum(p, axis=-1, keepdims=True)
    pb = p.astype(BF16)
    acc = alpha * acc + (_dot_nt(pb, v) if feature_major else _dot(pb, v))
    return m_new, l, acc


def _softmax_step(s, m, acc, v1):
    m_new = jnp.maximum(m, jnp.max(s, axis=-1, keepdims=True))
    p = jnp.exp2(s - m_new)
    return m_new, jnp.exp2(m - m_new) * acc + _dot(p.astype(BF16), v1)


def _softmax_init(R, W):
    return jnp.full((R, 1), NEG, F32), jnp.zeros((R, 1), F32), jnp.zeros((R, W), F32)


def _softmax_finish(l, acc):
    return acc * (1.0 / jnp.maximum(l, 1e-30))


def _top_blocks(score, n_real, n_top, axis=1):
    jidx = _iota(score.shape, axis)
    rank = jnp.zeros(score.shape, F32)
    for i in range(n_real):
        si = score[:, i:i + 1] if axis == 1 else score[i:i + 1, :]
        beats = (si > score) | ((si == score) & (jidx > i))
        rank = rank + jnp.where(beats, 1.0, 0.0)
    return jnp.where(rank < n_top, 1.0, 0.0)


def _block_scores(imp, q_pos, n_sel, axis=1):
    j = _iota(imp.shape, axis)
    cur = q_pos >> _log2(SLC_BLOCK)
    valid = j * SLC_BLOCK <= q_pos
    forced = valid & ((j == 0) | (j == cur) | (j == cur - 1))
    score = jnp.where(forced, FORCE, jnp.where(valid, imp, -FORCE))
    return jnp.where(j < n_sel, score, -3e38)


def _block_of_key_01(n_blocks, tk, pos0):
    j = _iota((n_blocks, tk), 0)
    c = _iota((n_blocks, tk), 1)
    return jnp.where(((pos0 + c) >> _log2(SLC_BLOCK)) == j, 1.0, 0.0).astype(BF16)


def _half_to(x, src_high, dst_high):
    return x if src_high == dst_high else pltpu.roll(x, HEAD_DIM, 1)


def _nsa_prompt_kernel(q_ref, g_ref, kc_ref, vc_ref, ks_ref, vs_ref, kw_ref, vw_ref, bd_ref, bc_ref, covt_ref,
                       o_ref, woff_sc, *, tq, n_cmp, n_sel, n_top, group):
    i = pl.program_id(2)
    G = group
    R = G * tq
    nt = tq // LANES
    JP = _round_up(n_sel, 8)
    n_win = WINDOW // tq
    low = _iota((tq, LANES), 1) < HEAD_DIM

    @pl.when(i == 0)
    def _():
        dist0 = _iota((tq, tq), 0) - _iota((tq, tq), 1)
        for dd in range(n_win + 1):
            woff_sc[dd] = _off_mask((dist0 + dd * tq >= 0) & (dist0 + dd * tq < WINDOW))

    def tile_off(dd):
        return jnp.concatenate([woff_sc[dd]] * G, axis=0)

    def prepare(hh):
        own = low if hh == 0 else jnp.logical_not(low)
        pieces = []
        for g in range(G):
            x = q_ref[0, :, (hh * 2 + g // 2) * LANES:(hh * 2 + g // 2 + 1) * LANES] * QSCALE
            pieces.append(_half_to(x, g % 2 == 1, hh == 1))
        qs = jnp.concatenate([jnp.where(own, x, 0.0) for x in pieces], axis=0).astype(BF16)

        bias_c = jnp.concatenate([bc_ref[a, hh * G + g] for g in range(G) for a in range(nt)], axis=0)
        s = _dot_nt(qs, kc_ref[0].astype(BF16)) + bias_c
        rowc = _iota(s.shape, 0) & (tq - 1)
        colc = _iota(s.shape, 1)
        dist_c = i * tq + rowc - (colc * CMP_STRIDE + (CMP_BLOCK - 1))
        p_c = _masked_softmax(s, (dist_c >= 0) & (colc < n_cmp)).astype(BF16)
        o_c = _dot(p_c, vc_ref[0].astype(BF16))
        covt = covt_ref[0:JP, :]
        impt = _dot_nt(covt, p_c[0:tq])
        for g in range(1, G):
            impt = impt + _dot_nt(covt, p_c[g * tq:(g + 1) * tq])
        q_pos = i * tq + _iota(impt.shape, 1)
        selt = _top_blocks(_block_scores(impt, q_pos, n_sel, 0), n_sel, n_top, 0)
        selt = jnp.concatenate([selt, jnp.zeros((LANES - JP, tq), F32)], axis=0)
        sel = selt.T

        sel_off = _half_to(_off_mask(sel > 0.5), False, hh == 0)
        qsel = jnp.concatenate([jnp.where(own, x, sel_off) for x in pieces], axis=0).astype(BF16)
        return qs, qsel, o_c

    def attend(hh, qs, qsel):
        own = low if hh == 0 else jnp.logical_not(low)
        spare_block = _iota((tq, LANES), 1) - (HEAD_DIM if hh == 0 else 0)
        key_row = _iota((tq, LANES), 0)

        def values(ref, start):
            return jnp.where(own, ref[0, pl.ds(start, tq), :], 1.0).astype(BF16)

        def dense_bias(d):
            blocks = []
            for g in range(G):
                for a in range(nt):
                    blocks.append(jnp.concatenate(
                        [bd_ref[jnp.maximum(nt * d + a - b, 0), hh * G + g] for b in range(nt)], axis=1))
            return jnp.concatenate(blocks, axis=0)

        def slc_tile(j, carry, diag):
            start = pl.multiple_of(j * tq, tq)
            in_block = ((start + key_row) >> _log2(SLC_BLOCK)) == spare_block
            kt = jnp.where(own, ks_ref[0, pl.ds(start, tq), :], jnp.where(in_block, 1.0, 0.0)).astype(BF16)
            s = _dot_nt(qsel, kt) + dense_bias(i - j)
            if diag:
                s = s + tile_off(0)
            return _softmax_step(s, *carry, values(vs_ref, start))

        init = (jnp.full((R, 1), NEG, F32), jnp.zeros((R, LANES), F32))
        carry = lax.fori_loop(0, i, lambda j, c: slc_tile(j, c, False), init)
        acc_s = lax.fori_loop(i, i + 1, lambda j, c: slc_tile(j, c, True), carry)[1]

        def win_tile(dd, carry):
            start = pl.multiple_of((i - dd) * tq, tq)
            kt = kw_ref[0, pl.ds(start, tq), :].astype(BF16)
            s = _dot_nt(qs, kt) + dense_bias(dd) + tile_off(dd)
            return _softmax_step(s, *carry, values(vw_ref, start))

        return acc_s, lax.fori_loop(0, jnp.minimum(i, n_win) + 1, win_tile, init)[1]

    def finish(hh, o_c, acc_s, acc_w):
        def gate_col(br):
            return jnp.concatenate([g_ref[0, 0, :, hh * 3 * G + g * 3 + br:hh * 3 * G + g * 3 + br + 1]
                                    for g in range(G)], axis=0)

        def weight_col(br, acc):
            c = HEAD_DIM if hh == 0 else 0
            return gate_col(br) * (1.0 / jnp.maximum(acc[:, c:c + 1], 1e-30))

        o = gate_col(0) * o_c + weight_col(1, acc_s) * acc_s + weight_col(2, acc_w) * acc_w
        for vc in range(G // 2):
            lo_piece = _half_to(o[(2 * vc) * tq:(2 * vc + 1) * tq], hh == 1, False)
            hi_piece = _half_to(o[(2 * vc + 1) * tq:(2 * vc + 2) * tq], hh == 1, True)
            o_ref[0, :, (hh * 2 + vc) * LANES:(hh * 2 + vc + 1) * LANES] = jnp.where(low, lo_piece, hi_piece)

    prepared = [prepare(hh) for hh in range(2)]
    attended = [attend(hh, qs, qsel) for hh, (qs, qsel, _) in enumerate(prepared)]
    for hh in range(2):
        finish(hh, prepared[hh][2], *attended[hh])


def nsa_prompt(q, gates, kcmp, vcmp, ks, vs, kw, vw, bias_dense, bias_cmp, cover, n_cmp, n_sel):
    B, L, WQ = q.shape
    WK = ks.shape[2]
    group = WQ // WK
    tq = 256
    nt = tq // LANES
    assert n_sel <= HEAD_DIM
    n_pairs = WK // LANES
    n_top = min(SLC_TOPN, n_sel)
    qw = WQ // n_pairs
    hp = bias_dense.shape[1] // n_pairs
    kv = pl.BlockSpec((1, L, LANES), lambda b, p, i: (b, 0, p))
    cm = pl.BlockSpec((1, kcmp.shape[1], LANES), lambda b, p, i: (b, 0, p))
    return pl.pallas_call(
        functools.partial(_nsa_prompt_kernel, tq=tq, n_cmp=n_cmp, n_sel=n_sel, n_top=n_top, group=group),
        grid=(B, n_pairs, L // tq),
        in_specs=[
            pl.BlockSpec((1, tq, qw), lambda b, p, i: (b, i, p)),
            pl.BlockSpec((1, 1, tq, LANES), lambda b, p, i: (p, b, i, 0)),
            cm, cm, kv, kv, kv, kv,
            pl.BlockSpec((bias_dense.shape[0], hp, LANES, LANES), lambda b, p, i: (0, p, 0, 0)),
            pl.BlockSpec((nt, hp, LANES, LANES), lambda b, p, i: (i, p, 0, 0)),
            _const_spec(cover.shape),
        ],
        out_specs=pl.BlockSpec((1, tq, qw), lambda b, p, i: (b, i, p)),
        out_shape=jax.ShapeDtypeStruct(q.shape, F32),
        scratch_shapes=[pltpu.VMEM((WINDOW // tq + 1, tq, tq), F32)],
        compiler_params=_cparams(("parallel", "parallel", "arbitrary")),
        name="nsa_prompt",
    )(q, gates, kcmp, vcmp, ks, vs, kw, vw, bias_dense, bias_cmp, cover)


def _nsa_sample_kernel(pt_ref, q_ref, g_ref, kc_ref, vc_ref, ksn_ref, vsn_ref, kwp_ref, vwp_ref,
                       kwn_ref, vwn_ref, bs_ref, bn_ref, bc_ref, bw_ref, cov_ref, e_ref, *rest,
                       n_tok, q_base, n_cmp, n_sel, n_top, group, pages):
    k_refs, v_refs = rest[:pages], rest[pages:2 * pages]
    o_ref, qs_sc, sel_sc, oc_sc, m_sc, l_sc, acc_sc = rest[2 * pages:]
    ci = pl.program_id(1)
    R, W = qs_sc.shape
    n_heads = R // n_tok
    J = sel_sc.shape[1]

    def tok(shape):
        return _iota(shape, 0) & (n_tok - 1)

    @pl.when(ci == 0)
    def _():
        blocks = []
        for h in range(n_heads):
            hkv = h // group
            x = q_ref[0, :, (h // 2) * LANES:(h // 2 + 1) * LANES] * QSCALE
            x = _half_to(x, h % 2 == 1, hkv % 2 == 1)
            keep = (_iota(x.shape, 1) >= HEAD_DIM) if hkv % 2 == 1 else (_iota(x.shape, 1) < HEAD_DIM)
            x = jnp.where(keep, x, 0.0)
            blocks.append(jnp.concatenate(
                [x if cc == hkv // 2 else jnp.zeros_like(x) for cc in range(W // LANES)], axis=1))
        qs = jnp.concatenate(blocks, axis=0).astype(BF16)
        qs_sc[...] = qs
        s = _dot_nt(qs, kc_ref[0].astype(BF16)) + bc_ref[...]
        n = _iota(s.shape, 1)
        dist_c = q_base + tok(s.shape) - (n * CMP_STRIDE + (CMP_BLOCK - 1))
        p_c = _masked_softmax(s, (dist_c >= 0) & (n < n_cmp)).astype(BF16)
        oc_sc[...] = _dot(p_c, vc_ref[0].astype(BF16))
        ic = _dot(p_c, cov_ref[...])
        rows = group * n_tok
        imp_blocks = []
        for hkv in range(n_heads // group):
            a = ic[hkv * rows:hkv * rows + n_tok]
            for g in range(1, group):
                a = a + ic[hkv * rows + g * n_tok:hkv * rows + (g + 1) * n_tok]
            imp_blocks += [a] * group
        imp = jnp.concatenate(imp_blocks, axis=0)
        sel_sc[...] = _top_blocks(_block_scores(imp, q_base + tok(imp.shape), n_sel), n_sel, n_top)
        m0, l0, a0 = _softmax_init(R, W)
        m_sc[...] = m0
        l_sc[...] = l0
        acc_sc[...] = a0

    qs = qs_sc[...]
    sel = sel_sc[...].astype(BF16)

    kt = jnp.concatenate([r[0, 0] for r in k_refs], axis=1).astype(BF16)
    vt = jnp.concatenate([r[0, 0] for r in v_refs], axis=1).astype(BF16)
    s = _dot(qs, kt) + bs_ref[...] + _off_mask(_dot(sel, e_ref[...]) > 0.5)
    m, l, acc = _softmax_update_add(s, m_sc[...], l_sc[...], acc_sc[...], vt, True)
    m_sc[...] = m
    l_sc[...] = l
    acc_sc[...] = acc

    @pl.when(ci == pl.num_programs(1) - 1)
    def _():
        tn = ksn_ref.shape[1]
        c = _iota((R, tn), 1)
        causal_new = c <= tok((R, tn))
        s = _dot_nt(qs, ksn_ref[0].astype(BF16)) + bn_ref[...]
        chosen = _dot(sel, _block_of_key_01(J, tn, q_base))
        _, l1, a1 = _softmax_update_add(s + _off_mask((chosen > 0.5) & causal_new), m, l, acc,
                                        vsn_ref[0].astype(BF16))
        o_s = _softmax_finish(l1, a1)
        s = _dot(qs, kwp_ref[0, 0].astype(BF16)) + bw_ref[...]
        dist = WINDOW + tok(s.shape) - _iota(s.shape, 1)
        cw = _softmax_update_add(s + _off_mask((dist >= 0) & (dist < WINDOW)), *_softmax_init(R, W),
                                 vwp_ref[0, 0].astype(BF16), True)
        s = _dot_nt(qs, kwn_ref[0].astype(BF16)) + bn_ref[...]
        _, l2, a2 = _softmax_update_add(s + _off_mask(causal_new), *cw, vwn_ref[0].astype(BF16))
        o_w = _softmax_finish(l2, a2)

        def gate_col(br):
            cols = []
            for h in range(n_heads):
                hkv, g = divmod(h, group)
                cidx = (hkv % 2) * 3 * group + g * 3 + br
                cols.append(g_ref[hkv // 2, 0, :, cidx:cidx + 1])
            return jnp.concatenate(cols, axis=0)

        o = gate_col(0) * oc_sc[...] + gate_col(1) * o_s + gate_col(2) * o_w
        low = _iota((n_tok, LANES), 1) < HEAD_DIM
        for oc in range(n_heads // 2):
            hkv = (2 * oc) // group
            src = slice((hkv // 2) * LANES, (hkv // 2 + 1) * LANES)
            lo_piece = _half_to(o[(2 * oc) * n_tok:(2 * oc + 1) * n_tok, src], hkv % 2 == 1, False)
            hi_piece = _half_to(o[(2 * oc + 1) * n_tok:(2 * oc + 2) * n_tok, src], hkv % 2 == 1, True)
            o_ref[0, :, oc * LANES:(oc + 1) * LANES] = jnp.where(low, lo_piece, hi_piece)


def nsa_sample(q, gates, kcmp, vcmp, pool_k, pool_v, win_k, win_v, layer, page_table, ks_new, vs_new,
               kw_new, vw_new, bias_slc, bias_new, bias_cmp, bias_win, cover, n_cmp, n_sel):
    B, n_tok, WQ = q.shape
    WK, page = pool_k.shape[2], pool_k.shape[3]
    n_pages = page_table.shape[1]
    past = n_pages * page
    group = WQ // WK
    n_heads = WQ // HEAD_DIM
    R = n_heads * n_tok
    J = cover.shape[1]
    pages = PAGE_GROUP
    chunk = pages * page
    block_of_key = (jnp.arange(past)[None, :] // SLC_BLOCK == jnp.arange(J)[:, None]).astype(BF16)

    def bspec(a):
        return pl.BlockSpec((1,) + a.shape[1:], lambda b, c, pt: (b,) + (0,) * (a.ndim - 1))

    def cspec(a):
        return pl.BlockSpec(a.shape, lambda b, c, pt: (0,) * a.ndim, pipeline_mode=pl.Buffered(1))

    def page_spec(kk):
        return pl.BlockSpec((1, 1, WK, page), lambda b, c, pt: (layer, pt[b, c * pages + kk], 0, 0))

    wspec = pl.BlockSpec((1, 1, WK, win_k.shape[3]), lambda b, c, pt: (layer, b, 0, 0))
    grid_spec = pltpu.PrefetchScalarGridSpec(
        num_scalar_prefetch=1,
        grid=(B, n_pages // pages),
        in_specs=[
            bspec(q),
            pl.BlockSpec((gates.shape[0], 1, n_tok, LANES), lambda b, c, pt: (0, b, 0, 0)),
            bspec(kcmp), bspec(vcmp), bspec(ks_new), bspec(vs_new), wspec, wspec, bspec(kw_new), bspec(vw_new),
            pl.BlockSpec((R, chunk), lambda b, c, pt: (0, c)),
            cspec(bias_new), cspec(bias_cmp), cspec(bias_win), cspec(cover),
            pl.BlockSpec((J, chunk), lambda b, c, pt: (0, c)),
        ] + [page_spec(kk) for kk in range(pages)] * 2,
        out_specs=bspec(q),
        scratch_shapes=[pltpu.VMEM((R, WK), BF16), pltpu.VMEM((R, J), F32), pltpu.VMEM((R, WK), F32),
                        pltpu.VMEM((R, 1), F32), pltpu.VMEM((R, 1), F32), pltpu.VMEM((R, WK), F32)],
    )
    return pl.pallas_call(
        functools.partial(_nsa_sample_kernel, n_tok=n_tok, q_base=past, n_cmp=n_cmp, n_sel=n_sel,
                          n_top=min(SLC_TOPN, n_sel), group=group, pages=pages),
        grid_spec=grid_spec,
        out_shape=jax.ShapeDtypeStruct(q.shape, F32),
        compiler_params=_cparams(("parallel", "arbitrary")),
        name="nsa_sample",
    )(page_table, q, gates, kcmp, vcmp, ks_new, vs_new, win_k, win_v, kw_new, vw_new,
      bias_slc, bias_new, bias_cmp, bias_win, cover, block_of_key, *([pool_k] * pages), *([pool_v] * pages))


def _head_of_lane(width):
    assert width // HEAD_DIM <= LANES
    return (jnp.arange(width)[:, None] // HEAD_DIM == jnp.arange(LANES)[None, :]).astype(BF16)


def _block_diag_ones(width):
    h = jnp.arange(width) // HEAD_DIM
    return (h[:, None] == h[None, :]).astype(BF16)


def _cover_01(n_cmp, n_sel, rows, cols):
    ci = jnp.arange(rows) * CMP_STRIDE
    sj = jnp.arange(cols) * SLC_BLOCK
    hit = (ci[:, None] < sj[None, :] + SLC_BLOCK) & (ci[:, None] + CMP_BLOCK > sj[None, :])
    hit = hit & (jnp.arange(rows)[:, None] < n_cmp) & (jnp.arange(cols)[None, :] < n_sel)
    return hit.astype(BF16)


def _round_up(n, m):
    return -(-n // m) * m


def _compress_weights(w1, b1, w2, b2, n_kv):
    hidden = w1.shape[1]
    eye = jnp.eye(n_kv, dtype=F32)
    w1r = w1.reshape(CMP_BLOCK // CMP_STRIDE, CMP_STRIDE, HEAD_DIM, hidden)
    rk = 2
    w1e = jnp.einsum('mrde,hk->mrhdke', w1r, eye).reshape(
        CMP_BLOCK // CMP_STRIDE, CMP_STRIDE // rk, rk * n_kv * HEAD_DIM, n_kv * hidden).astype(BF16)
    w2e = jnp.einsum('ed,hk->hekd', w2, eye).reshape(n_kv * hidden, n_kv * HEAD_DIM).astype(BF16)
    return w1e, jnp.tile(b1, n_kv)[None, :], w2e, jnp.tile(b2, n_kv)[None, :]


def _trunk(x, mods, past, P):
    B, L, D = x.shape
    depth = P['norm_g'].shape[0]
    even_states, odd_states = [], []
    for li in range(depth):
        def m(k):
            return mods[li, :, k][:, None, :]

        def g(k):
            return P['norm_g'][li, k][None, :]

        x = ffn(x, m(0), m(1), m(2), g(0), P['ffn_w_in'][li, 0], P['ffn_w_out'][li, 0])
        if li % 2 == 0:
            e = li // 2
            w = P['even_w_in'][e]
            n_sb = P['sb_width']
            HK, HV = P['gla_hk'], P['gla_hv']
            offs = [0, n_sb, 2 * n_sb, 3 * n_sb, 3 * n_sb + HK, 3 * n_sb + 2 * HK, 3 * n_sb + 2 * HK + HV,
                    3 * n_sb + 2 * HK + 2 * HV]
            ws = [w[:, offs[k]:offs[k + 1]] for k in range(7)]
            rank = w.shape[1] - offs[7]
            wa1 = jnp.pad(w[:, offs[7]:], ((0, 0), (0, LANES - rank)))
            wa2 = jnp.pad(P['gla_w_a2'][e], ((0, LANES - rank), (0, 0)))
            qa, ka, va, qb, kb, vb, rb, la = even_proj(x, m(3), m(4), g(1), ws, wa1, wa2, P['gla_b_a2'][e][None, :])
            dv = P['gla_onorm_g'].shape[1]
            n_gh = HV // dv
            dk = HK // n_gh
            if past is None:
                o_a = sb_prompt(qa, ka, va)
                s0t = jnp.zeros((B, dv, HK), F32)
            else:
                o_a = sb_sample(qa, ka, va, past['sb_k'], past['sb_v'], e, past['page_table'])
                s0t = past['gla'][e].transpose(0, 3, 1, 2).reshape(B, dv, HK)
            o_b, st = gla(qb, kb, vb, rb, la, s0t, P['gla_onorm_g'][e][None, :])
            wo = P['even_w_out'][e]
            x = out_proj(x, m(5), [o_a, o_b], [wo[:n_sb], wo[n_sb:]])
            s_fin = st.reshape(B, dv, n_gh, dk).transpose(0, 2, 3, 1)
            even_states.append((ka.reshape(B, L, -1, HEAD_DIM), va.reshape(B, L, -1, HEAD_DIM), s_fin))
        else:
            o_i = li // 2
            w = P['odd_w_in'][o_i]
            WQ, WK = P['nsa_wq'], P['nsa_wk']
            n_kv = WK // HEAD_DIM
            group = WQ // WK
            offs = [0, WQ] + [WQ + (k + 1) * WK for k in range(6)]
            ws = [w[:, offs[k]:offs[k + 1]] for k in range(7)]
            wg = w[:, offs[7]:]
            half = wg.shape[1] // 2
            ws += [jnp.pad(wg[:, :half], ((0, 0), (0, LANES - half))),
                   jnp.pad(wg[:, half:], ((0, 0), (0, LANES - half)))]
            qk_g = P['nsa_qk_g'][o_i]
            consts = [_head_of_lane(WQ), _head_of_lane(WK), jnp.tile(qk_g[0], WQ // HEAD_DIM)[None, :],
                      jnp.tile(qk_g[2], n_kv)[None, :], jnp.tile(qk_g[3], n_kv)[None, :]]
            q, kc, vc, ks, vs, kw, vw, gates = odd_proj(x, m(3), m(4), g(1), ws, consts)
            pair = LANES // HEAD_DIM
            gk = jnp.tile(qk_g[1], pair)[None, :]
            bdp = _block_diag_ones(LANES)
            cw = [_compress_weights(P['cmp_w1'][o_i, t], P['cmp_b1'][o_i, t], P['cmp_w2'][o_i, t],
                                    P['cmp_b2'][o_i, t], pair) for t in range(2)]
            rel = P['rel_table']
            if past is None:
                n_rows = (L // CMP_STRIDE) * CMP_STRIDE
                assert n_rows == L and L % LANES == 0
                n_cmp = L // CMP_STRIDE - CMP_BLOCK // CMP_STRIDE + 1
                n_sel = -(-L // SLC_BLOCK)
                kcmp = compress(kc, n_rows, *cw[0], bdp, gk, True)
                vcmp = compress(vc, n_rows, *cw[1], bdp, gk, False)
                assert kcmp.shape[1] == LANES and n_sel <= LANES
                nq = L // LANES
                bias_dense = bias_tiles(rel, jnp.arange(nq, dtype=jnp.int32) * LANES, LANES, LANES, 1)
                bias_cmp = bias_tiles(rel, jnp.arange(nq, dtype=jnp.int32) * LANES - (CMP_BLOCK - 1),
                                      LANES, LANES, CMP_STRIDE)
                cover = _cover_01(n_cmp, n_sel, LANES, LANES).T
                o = nsa_prompt(q, gates, kcmp, vcmp, ks, vs, kw, vw, bias_dense, bias_cmp, cover, n_cmp, n_sel)
                w_keep = min(WINDOW, L)
                win_k, win_v = kw[:, L - w_keep:], vw[:, L - w_keep:]
            else:
                pt = past['page_table']
                past_len = pt.shape[1] * past['cmp_k'].shape[3]
                T = past_len + L
                n_rows = (T // CMP_STRIDE) * CMP_STRIDE
                assert n_rows == past_len and past['win_k'].shape[3] == WINDOW and L <= LANES
                n_cmp = n_rows // CMP_STRIDE - CMP_BLOCK // CMP_STRIDE + 1
                n_sel = -(-T // SLC_BLOCK)
                kcmp = compress_paged(past['cmp_k'], o_i, pt, *cw[0], bdp, gk, True)
                vcmp = compress_paged(past['cmp_v'], o_i, pt, *cw[1], bdp, gk, False)
                J = _round_up(n_sel, LANES)
                n_heads = WQ // HEAD_DIM
                R = n_heads * L

                def sample_bias(base, C, cstep):
                    return bias_tiles(rel, jnp.array([base], jnp.int32), L, C, cstep).reshape(R, C)

                bias_slc = sample_bias(past_len, past_len, 1)
                bias_new = sample_bias(0, LANES, 1)
                bias_cmp = sample_bias(past_len - (CMP_BLOCK - 1), kcmp.shape[1], CMP_STRIDE)
                bias_win = sample_bias(WINDOW, WINDOW, 1)
                cover = _cover_01(n_cmp, n_sel, kcmp.shape[1], J)
                pad = ((0, 0), (0, LANES - L), (0, 0))
                o = nsa_sample(q, gates, kcmp, vcmp, past['slc_k'], past['slc_v'], past['win_k'], past['win_v'],
                               o_i, pt, jnp.pad(ks, pad), jnp.pad(vs, pad), jnp.pad(kw, pad), jnp.pad(vw, pad),
                               bias_slc, bias_new, bias_cmp, bias_win, cover, n_cmp, n_sel)
                win_k = jnp.concatenate([past['win_k_rows'][o_i], kw], axis=1)[:, L:]
                win_v = jnp.concatenate([past['win_v_rows'][o_i], vw], axis=1)[:, L:]
            x = out_proj(x, m(5), [o], [P['odd_w_out'][o_i]])

            def rows(t):
                return t.reshape(B, t.shape[1], n_kv, HEAD_DIM)

            odd_states.append(tuple(rows(t) for t in (kc, vc, ks, vs, win_k, win_v)))
        x = ffn(x, m(6), m(7), m(8), g(2), P['ffn_w_in'][li, 1], P['ffn_w_out'][li, 1])
    even_new = [jnp.stack([s[i] for s in even_states]) for i in range(3)]
    odd_new = [jnp.stack([s[i] for s in odd_states]) for i in range(6)]
    return x, even_new, odd_new


def kernel(x_prompt, x_sample, cache_sb_k, cache_sb_v, state_gla, cache_cmp_k, cache_cmp_v, cache_slc_k,
           cache_slc_v, cache_win_k, cache_win_v, page_table, c_prompt, c_sample, norm_g, ada_w, ada_b,
           ffn_w_in, ffn_w_out, even_w_in, gla_w_a2, gla_b_a2, gla_onorm_g, even_w_out, odd_w_in, nsa_qk_g,
           cmp_w1, cmp_b1, cmp_w2, cmp_b2, rel_table, odd_w_out):
    D = x_prompt.shape[-1]
    depth = norm_g.shape[0]
    Bp, Bs = x_prompt.shape[0], x_sample.shape[0]
    mods = ada_mod(jnp.concatenate([c_prompt, c_sample], axis=0), ada_w, ada_b).reshape(depth, Bp + Bs, 9, D)
    n_sb = cache_sb_k.shape[3] * cache_sb_k.shape[4]
    dv = gla_onorm_g.shape[1]
    HK = gla_w_a2.shape[2]
    n_gh = state_gla.shape[2]
    WK = cache_cmp_k.shape[3] * cache_cmp_k.shape[4]
    P = {
        'norm_g': norm_g, 'ffn_w_in': ffn_w_in.astype(BF16), 'ffn_w_out': ffn_w_out.astype(BF16),
        'even_w_in': even_w_in.astype(BF16), 'gla_w_a2': gla_w_a2.astype(BF16), 'gla_b_a2': gla_b_a2,
        'gla_onorm_g': gla_onorm_g, 'even_w_out': even_w_out.astype(BF16), 'odd_w_in': odd_w_in.astype(BF16),
        'nsa_qk_g': nsa_qk_g, 'cmp_w1': cmp_w1, 'cmp_b1': cmp_b1, 'cmp_w2': cmp_w2, 'cmp_b2': cmp_b2,
        'rel_table': rel_table, 'odd_w_out': odd_w_out.astype(BF16),
        'sb_width': n_sb, 'gla_hk': HK, 'gla_hv': n_gh * dv, 'nsa_wq': odd_w_out.shape[1], 'nsa_wk': WK,
    }

    def pool(c):
        return c.transpose(0, 1, 3, 4, 2).reshape(c.shape[0], c.shape[1], c.shape[3] * c.shape[4], c.shape[2])

    def rows(c):
        return c.reshape(c.shape[0], c.shape[1], c.shape[2], c.shape[3] * c.shape[4])

    past = {
        'page_table': page_table, 'sb_k': pool(cache_sb_k), 'sb_v': pool(cache_sb_v), 'gla': state_gla,
        'cmp_k': pool(cache_cmp_k), 'cmp_v': pool(cache_cmp_v), 'slc_k': pool(cache_slc_k),
        'slc_v': pool(cache_slc_v), 'win_k': pool(cache_win_k), 'win_v': pool(cache_win_v),
        'win_k_rows': rows(cache_win_k), 'win_v_rows': rows(cache_win_v),
    }
    y_p, p_even, p_odd = _trunk(x_prompt, mods[:, :Bp], None, P)
    y_s, s_even, s_odd = _trunk(x_sample, mods[:, Bp:], past, P)
    return (y_p, y_s, *p_even, *p_odd, *s_even, *s_odd)
```

```python
import functools
import math

import jax
import jax.numpy as jnp
from jax import lax
from jax.experimental import pallas as pl
from jax.experimental.pallas import tpu as pltpu

F32 = jnp.float32
BF16 = jnp.bfloat16

EPS = 1e-6
NEG = -1e30
FORCE = 1e6
HEAD_DIM = 64
GLA_TAU = 16.0
GLA_CHUNK = 64
CMP_BLOCK = 32
CMP_STRIDE = 16
SLC_BLOCK = 64
SLC_TOPN = 16
WINDOW = 512
REL_BUCKETS = 32
REL_MAX_DIST = 1024
PAGE_GROUP = 16
LANES = 128
VMEM_LIMIT = 56 * 1024 * 1024


def _cparams(sem, vmem=VMEM_LIMIT):
    return pltpu.CompilerParams(dimension_semantics=sem, vmem_limit_bytes=vmem)


def _dot(a, b):
    return jnp.dot(a, b, preferred_element_type=F32)


def _dot_nt(a, b):
    return lax.dot_general(a, b, (((1,), (1,)), ((), ())), preferred_element_type=F32)


def _dot_tn(a, b):
    return lax.dot_general(a, b, (((0,), (0,)), ((), ())), preferred_element_type=F32)


def _split2(x):
    hi = x.astype(BF16)
    lo = (x - hi.astype(F32)).astype(BF16)
    return hi, lo


def _dot_f32_by01(x, m01):
    hi, lo = _split2(x)
    return _dot(hi, m01) + _dot(lo, m01)


def _dot_01_by_f32(m01, x):
    hi, lo = _split2(x)
    return _dot(m01, hi) + _dot(m01, lo)


def _softplus(z):
    return jnp.maximum(z, 0.0) + jnp.log1p(jnp.exp(-jnp.abs(z)))


LOG2E = 1.4426950408889634
QSCALE = HEAD_DIM ** -0.5 * LOG2E
MAX_SHIFT_SPAN = 100.0


def _sb_logs2(z2):
    l2 = jnp.log(1.0 + jnp.exp2(-jnp.abs(z2))) * LOG2E
    lk = -(jnp.maximum(z2, 0.0) + l2)
    return lk, z2 + lk


def _silu(a):
    return a * jax.nn.sigmoid(a)


def _iota(shape, dim):
    return lax.broadcasted_iota(jnp.int32, shape, dim)


def _modulate(x, g, shift, scale):
    ms = jnp.mean(x * x, axis=-1, keepdims=True)
    y = x * lax.rsqrt(ms + EPS) * g
    return y * (1.0 + scale) + shift


def _head_rms(x, bd, g):
    ms = _dot_f32_by01(x * x, bd) * (1.0 / HEAD_DIM)
    return x * lax.rsqrt(ms + EPS) * g


def _const_spec(shape):
    nd = len(shape)
    return pl.BlockSpec(shape, lambda *_: (0,) * nd, pipeline_mode=pl.Buffered(1))


def _row_tiles(B, L):
    if L >= 512:
        return 1, 512
    return B, L


def _ada_kernel(c_ref, w_ref, b_ref, o_ref):
    c = c_ref[...]
    a = _silu(c).astype(BF16)
    o_ref[0] = _dot(a, w_ref[0].astype(BF16)) + b_ref[0]


def ada_mod(c, ada_w, ada_b):
    R, D = c.shape
    depth, _, N = ada_w.shape
    tn = 1024
    return pl.pallas_call(
        _ada_kernel,
        grid=(depth, N // tn),
        in_specs=[
            pl.BlockSpec((R, D), lambda l, n: (0, 0)),
            pl.BlockSpec((1, D, tn), lambda l, n: (l, 0, n)),
            pl.BlockSpec((1, 1, tn), lambda l, n: (l, 0, n)),
        ],
        out_specs=pl.BlockSpec((1, R, tn), lambda l, n: (l, 0, n)),
        out_shape=jax.ShapeDtypeStruct((depth, R, N), F32),
        compiler_params=_cparams(("parallel", "parallel")),
        name="ada_mod",
    )(c, ada_w, ada_b.reshape(depth, 1, N))


def _ffn_kernel(x_ref, sh_ref, sc_ref, gt_ref, g_ref, win_ref, wout_ref, o_ref, *, d_ff, tf):
    x = x_ref[...]
    tb, tl, D = x.shape
    h = _modulate(x, g_ref[...], sh_ref[...], sc_ref[...])
    hb = h.reshape(tb * tl, D).astype(BF16)
    acc = jnp.zeros((tb * tl, D), F32)
    for j in range(d_ff // tf):
        a = _dot(hb, win_ref[:, j * tf:(j + 1) * tf])
        b = _dot(hb, win_ref[:, d_ff + j * tf:d_ff + (j + 1) * tf])
        u = (_silu(a) * b).astype(BF16)
        acc = acc + _dot(u, wout_ref[j * tf:(j + 1) * tf, :])
    o_ref[...] = x + 0.5 * gt_ref[...] * acc.reshape(tb, tl, D)


def ffn(x, shift, scale, gate, g, w_in, w_out):
    B, L, D = x.shape
    d_ff = w_out.shape[0]
    tb, tl = _row_tiles(B, L)
    xs = pl.BlockSpec((tb, tl, D), lambda b, l: (b, l, 0))
    ms = pl.BlockSpec((tb, 1, D), lambda b, l: (b, 0, 0))
    return pl.pallas_call(
        functools.partial(_ffn_kernel, d_ff=d_ff, tf=256),
        grid=(B // tb, L // tl),
        in_specs=[xs, ms, ms, ms, _const_spec((1, D)), _const_spec(w_in.shape), _const_spec(w_out.shape)],
        out_specs=xs,
        out_shape=jax.ShapeDtypeStruct(x.shape, F32),
        compiler_params=_cparams(("parallel", "parallel")),
        name="ffn",
    )(x, shift, scale, gate, g, w_in, w_out)


def _even_proj_kernel(x_ref, sh_ref, sc_ref, g_ref, wqa, wka, wva, wqb, wkb, wvb, wrb, wa1, wa2, ba2,
                      qa, ka, va, qb, kb, vb, rb, la):
    x = x_ref[...]
    tb, tl, D = x.shape
    h = _modulate(x, g_ref[...], sh_ref[...], sc_ref[...])
    hb = h.reshape(tb * tl, D).astype(BF16)
    for w, o in ((wqa, qa), (wka, ka), (wva, va), (wqb, qb), (wkb, kb), (wvb, vb), (wrb, rb)):
        o[...] = _dot(hb, w[...]).reshape(o.shape)
    a1 = _dot(hb, wa1[...]).astype(BF16)
    z = _dot(a1, wa2[...]) + ba2[...]
    la[...] = (-_softplus(-z) * (1.0 / GLA_TAU)).reshape(la.shape)


def even_proj(x, shift, scale, g, ws, wa1, wa2, ba2):
    B, L, D = x.shape
    tb, tl = _row_tiles(B, L)
    xs = pl.BlockSpec((tb, tl, D), lambda b, l: (b, l, 0))
    ms = pl.BlockSpec((tb, 1, D), lambda b, l: (b, 0, 0))
    widths = [w.shape[1] for w in ws] + [wa2.shape[1]]
    return pl.pallas_call(
        _even_proj_kernel,
        grid=(B // tb, L // tl),
        in_specs=[xs, ms, ms, _const_spec((1, D))] + [_const_spec(w.shape) for w in ws]
        + [_const_spec(wa1.shape), _const_spec(wa2.shape), _const_spec(ba2.shape)],
        out_specs=[pl.BlockSpec((tb, tl, n), lambda b, l: (b, l, 0)) for n in widths],
        out_shape=[jax.ShapeDtypeStruct((B, L, n), F32) for n in widths],
        compiler_params=_cparams(("parallel", "parallel")),
        name="even_proj",
    )(x, shift, scale, g, *ws, wa1, wa2, ba2)


def _odd_proj_kernel(x_ref, sh_ref, sc_ref, g_ref, wq, wkc, wvc, wks, wvs, wkw, wvw, wg0, wg1,
                     hq, hk, gq, gks, gkw,
                     q, kc, vc, ks, vs, kw, vw, gates):
    x = x_ref[...]
    tb, tl, D = x.shape
    h = _modulate(x, g_ref[...], sh_ref[...], sc_ref[...])
    hb = h.reshape(tb * tl, D).astype(BF16)
    q[...] = _head_rms(_dot(hb, wq[...]), hq[...], gq[...]).reshape(q.shape)
    kc[...] = _dot(hb, wkc[...]).reshape(kc.shape)
    vc[...] = _dot(hb, wvc[...]).reshape(vc.shape)
    ks[...] = _head_rms(_dot(hb, wks[...]), hk[...], gks[...]).reshape(ks.shape)
    vs[...] = _dot(hb, wvs[...]).reshape(vs.shape)
    kw[...] = _head_rms(_dot(hb, wkw[...]), hk[...], gkw[...]).reshape(kw.shape)
    vw[...] = _dot(hb, wvw[...]).reshape(vw.shape)
    gates[0] = jax.nn.sigmoid(_dot(hb, wg0[...])).reshape(gates.shape[1:])
    gates[1] = jax.nn.sigmoid(_dot(hb, wg1[...])).reshape(gates.shape[1:])


def odd_proj(x, shift, scale, g, ws, consts):
    B, L, D = x.shape
    tb, tl = _row_tiles(B, L)
    xs = pl.BlockSpec((tb, tl, D), lambda b, l: (b, l, 0))
    ms = pl.BlockSpec((tb, 1, D), lambda b, l: (b, 0, 0))
    widths = [w.shape[1] for w in ws[:7]]
    out_specs = [pl.BlockSpec((tb, tl, n), lambda b, l: (b, l, 0)) for n in widths]
    out_specs.append(pl.BlockSpec((2, tb, tl, LANES), lambda b, l: (0, b, l, 0)))
    out_shape = [jax.ShapeDtypeStruct((B, L, n), F32) for n in widths]
    out_shape.append(jax.ShapeDtypeStruct((2, B, L, LANES), F32))
    return pl.pallas_call(
        _odd_proj_kernel,
        grid=(B // tb, L // tl),
        in_specs=[xs, ms, ms, _const_spec((1, D))] + [_const_spec(w.shape) for w in ws]
        + [_const_spec(c.shape) for c in consts],
        out_specs=out_specs,
        out_shape=out_shape,
        compiler_params=_cparams(("parallel", "parallel")),
        name="odd_proj",
    )(x, shift, scale, g, *ws, *consts)


def _out_proj_kernel(*refs, n_in):
    x_ref, gt_ref = refs[0], refs[1]
    a_refs = refs[2:2 + n_in]
    w_refs = refs[2 + n_in:2 + 2 * n_in]
    o_ref = refs[2 + 2 * n_in]
    x = x_ref[...]
    tb, tl, D = x.shape
    acc = jnp.zeros((tb * tl, D), F32)
    for a, w in zip(a_refs, w_refs):
        av = a[...]
        acc = acc + _dot(av.reshape(tb * tl, av.shape[-1]).astype(BF16), w[...])
    o_ref[...] = x + gt_ref[...] * acc.reshape(tb, tl, D)


def out_proj(x, gate, acts, ws):
    B, L, D = x.shape
    tb, tl = _row_tiles(B, L)
    xs = pl.BlockSpec((tb, tl, D), lambda b, l: (b, l, 0))
    ms = pl.BlockSpec((tb, 1, D), lambda b, l: (b, 0, 0))
    return pl.pallas_call(
        functools.partial(_out_proj_kernel, n_in=len(acts)),
        grid=(B // tb, L // tl),
        in_specs=[xs, ms] + [pl.BlockSpec((tb, tl, a.shape[-1]), lambda b, l: (b, l, 0)) for a in acts]
        + [_const_spec(w.shape) for w in ws],
        out_specs=xs,
        out_shape=jax.ShapeDtypeStruct(x.shape, F32),
        compiler_params=_cparams(("parallel", "parallel")),
        name="out_proj",
    )(x, gate, *acts, *ws)


def _log2(n):
    assert n > 0 and n & (n - 1) == 0, n
    return n.bit_length() - 1


def _later_keys_01(tk):
    return jnp.where(_iota((tk, tk), 0) > _iota((tk, tk), 1), 1.0, 0.0).astype(BF16)


def _sb_tile(q2, kt, vt, run, acc, u01, before, feature_major=False):
    lk, ls = _sb_logs2(_dot(q2, kt) if feature_major else _dot_nt(q2, kt))
    if before is not None:
        lk = jnp.where(before, lk, 0.0)
    cs = _dot_f32_by01(lk, u01)
    w = jnp.exp2(ls + cs + run)
    if before is not None:
        w = jnp.where(before, w, 0.0)
    wb = w.astype(BF16)
    acc = acc + (_dot_nt(wb, vt) if feature_major else _dot(wb, vt))
    run = run + cs[:, 0:1] + lk[:, 0:1]
    return run, acc


def _sb_prompt_kernel(q_ref, k_ref, v_ref, o_ref, *, tq):
    i = pl.program_id(2)
    q = q_ref[0] * QSCALE
    lane = _iota((tq, LANES), 1)
    q2 = jnp.concatenate([jnp.where(lane < HEAD_DIM, q, 0.0), jnp.where(lane >= HEAD_DIM, q, 0.0)],
                         axis=0).astype(BF16)
    u01 = _later_keys_01(tq)
    before = _iota((2 * tq, tq), 1) < (_iota((2 * tq, tq), 0) & (tq - 1))

    def tile(j, run, acc, mask):
        start = pl.multiple_of(j * tq, tq)
        kt = k_ref[0, pl.ds(start, tq), :].astype(BF16)
        vt = v_ref[0, pl.ds(start, tq), :].astype(BF16)
        return _sb_tile(q2, kt, vt, run, acc, u01, mask)

    run, acc = tile(i, jnp.zeros((2 * tq, 1), F32), jnp.zeros((2 * tq, LANES), F32), before)
    run, acc = lax.fori_loop(1, i + 1, lambda jj, c: tile(i - jj, c[0], c[1], None), (run, acc))
    o_ref[0] = jnp.where(lane < HEAD_DIM, acc[:tq], acc[tq:])


def sb_prompt(q, k, v):
    B, L, W = q.shape
    tq = 256
    qs = pl.BlockSpec((1, tq, LANES), lambda b, p, i: (b, i, p))
    ks = pl.BlockSpec((1, L, LANES), lambda b, p, i: (b, 0, p))
    return pl.pallas_call(
        functools.partial(_sb_prompt_kernel, tq=tq),
        grid=(B, W // LANES, L // tq),
        in_specs=[qs, ks, ks],
        out_specs=qs,
        out_shape=jax.ShapeDtypeStruct(q.shape, F32),
        compiler_params=_cparams(("parallel", "parallel", "arbitrary")),
        name="sb_prompt",
    )(q, k, v)


def _sb_sample_kernel(pt_ref, q_ref, kn_ref, vn_ref, *rest, n_tok, group):
    k_refs, v_refs = rest[:group], rest[group:2 * group]
    o_ref, run_ref, acc_ref = rest[2 * group:]
    s = pl.program_id(1)
    q = q_ref[0] * QSCALE
    W = q.shape[1]
    n_heads = W // HEAD_DIM
    R = n_heads * n_tok
    tk = kn_ref.shape[1]
    own = (_iota((R, W), 0) >> _log2(n_tok)) == (_iota((R, W), 1) >> _log2(HEAD_DIM))
    q2 = jnp.where(own, jnp.concatenate([q] * n_heads, axis=0), 0.0).astype(BF16)
    u01 = _later_keys_01(tk)

    @pl.when(s == 0)
    def _():
        before = _iota((R, tk), 1) < (_iota((R, tk), 0) & (n_tok - 1))
        run, acc = _sb_tile(q2, kn_ref[0].astype(BF16), vn_ref[0].astype(BF16),
                            jnp.zeros((R, 1), F32), jnp.zeros((R, W), F32), u01, before)
        run_ref[...] = run
        acc_ref[...] = acc

    @pl.when(s > 0)
    def _():
        kt = jnp.concatenate([r[0, 0] for r in k_refs], axis=1).astype(BF16)
        vt = jnp.concatenate([r[0, 0] for r in v_refs], axis=1).astype(BF16)

        def stack(x):
            return jnp.concatenate([x[:, kk * tk:(kk + 1) * tk] for kk in range(group)], axis=0)

        lk, ls = _sb_logs2(_dot(q2, kt))
        lk = stack(lk)
        cs = _dot_f32_by01(lk, u01)
        tot = cs[:, 0:1] + lk[:, 0:1]
        run = run_ref[...]
        runs = []
        for kk in range(group):
            runs.append(run)
            run = run + tot[kk * R:(kk + 1) * R]
        w = jnp.exp2(stack(ls) + cs + jnp.concatenate(runs, axis=0))
        w = jnp.concatenate([w[kk * R:(kk + 1) * R] for kk in range(group)], axis=1).astype(BF16)
        acc_ref[...] = acc_ref[...] + _dot_nt(w, vt)
        run_ref[...] = run

    @pl.when(s == pl.num_programs(1) - 1)
    def _():
        acc = jnp.where(own, acc_ref[...], 0.0)
        out = acc[0:n_tok]
        for h in range(1, n_heads):
            out = out + acc[h * n_tok:(h + 1) * n_tok]
        o_ref[0] = out


def sb_sample(q, k_new, v_new, pool_k, pool_v, layer, page_table):
    B, n_tok, W = q.shape
    page = pool_k.shape[3]
    n_pages = page_table.shape[1]
    group = PAGE_GROUP
    n_steps = 1 + n_pages // group
    pad = ((0, 0), (0, page - n_tok), (0, 0))
    kn, vn = jnp.pad(k_new, pad), jnp.pad(v_new, pad)

    def page_spec(kk):
        def imap(b, s, pt):
            return (layer, pt[b, n_pages - 1 - (jnp.maximum(s, 1) - 1) * group - kk], 0, 0)
        return pl.BlockSpec((1, 1, W, page), imap)

    qs = pl.BlockSpec((1, n_tok, W), lambda b, s, pt: (b, 0, 0))
    ns = pl.BlockSpec((1, page, W), lambda b, s, pt: (b, 0, 0))
    R = (W // HEAD_DIM) * n_tok
    grid_spec = pltpu.PrefetchScalarGridSpec(
        num_scalar_prefetch=1,
        grid=(B, n_steps),
        in_specs=[qs, ns, ns] + [page_spec(kk) for kk in range(group)] * 2,
        out_specs=qs,
        scratch_shapes=[pltpu.VMEM((R, 1), F32), pltpu.VMEM((R, W), F32)],
    )
    return pl.pallas_call(
        functools.partial(_sb_sample_kernel, n_tok=n_tok, group=group),
        grid_spec=grid_spec,
        out_shape=jax.ShapeDtypeStruct(q.shape, F32),
        compiler_params=_cparams(("parallel", "arbitrary")),
        name="sb_sample",
    )(page_table, q, kn, vn, *([pool_k] * group), *([pool_v] * group))


def _gla_kernel(q_ref, k_ref, v_ref, r_ref, la_ref, s0_ref, g_ref, o_ref, sfin_ref, st, *, dk, dv):
    c = pl.program_id(1)

    @pl.when(c == 0)
    def _():
        st[...] = s0_ref[...]

    for ib in range(q_ref.shape[0]):
        _gla_chunk(q_ref.at[ib], k_ref.at[ib], v_ref.at[ib], r_ref.at[ib], la_ref.at[ib], g_ref, o_ref.at[ib],
                   st.at[ib], dk=dk, dv=dv)

    @pl.when(c == pl.num_programs(1) - 1)
    def _():
        sfin_ref[...] = st[...]


def _gla_chunk(q_ref, k_ref, v_ref, r_ref, la_ref, g_ref, o_ref, st, *, dk, dv):
    q = q_ref[...] * (dk ** -0.5)
    k = k_ref[...]
    la = la_ref[...]
    C, HK = q.shape
    n_heads = HK // dk
    causal = _iota((C, C), 0) >= _iota((C, C), 1)
    ltri = jnp.where(causal, 1.0, 0.0).astype(BF16)
    hi = la.astype(BF16)
    r1 = la - hi.astype(F32)
    mid = r1.astype(BF16)
    lo = (r1 - mid.astype(F32)).astype(BF16)
    b = _dot(ltri, hi) + _dot(ltri, mid) + _dot(ltri, lo)
    b_last = b[C - 1:C, :]
    q_in = q * jnp.exp(b)
    k_in = (k * jnp.exp(-b)).astype(BF16)
    k_dec = k * jnp.exp(b_last - b)
    lane_h = _iota((C, HK), 1) >> _log2(dk)
    s_old = st[...]
    s_old_b = s_old.astype(BF16)
    upd = jnp.zeros(s_old.shape, F32)
    for h in range(n_heads):
        own = lane_h == h
        qh = jnp.where(own, q_in, 0.0).astype(BF16)
        att = jnp.where(causal, _dot_nt(qh, k_in), 0.0)
        vh = v_ref[:, h * dv:(h + 1) * dv].astype(BF16)
        oh = _dot(att.astype(BF16), vh) + _dot_nt(qh, s_old_b)
        y = oh * lax.rsqrt(jnp.mean(oh * oh, axis=-1, keepdims=True) + EPS) * g_ref[...]
        o_ref[:, h * dv:(h + 1) * dv] = y * _silu(r_ref[:, h * dv:(h + 1) * dv])
        kd = jnp.where(own, k_dec, 0.0).astype(BF16)
        upd = upd + _dot_tn(vh, kd)
    st[...] = s_old * jnp.exp(b_last) + upd


def gla(q, k, v, r, la, s0t, g):
    B, L, HK = q.shape
    HV = v.shape[2]
    dv = g.shape[1]
    dk = HK // (HV // dv)
    C = min(GLA_CHUNK, L)
    bb = next(n for n in (4, 2, 1) if B % n == 0)
    ks = pl.BlockSpec((bb, C, HK), lambda b, c: (b, c, 0))
    vs = pl.BlockSpec((bb, C, HV), lambda b, c: (b, c, 0))
    ss = pl.BlockSpec((bb, dv, HK), lambda b, c: (b, 0, 0))
    return pl.pallas_call(
        functools.partial(_gla_kernel, dk=dk, dv=dv),
        grid=(B // bb, L // C),
        in_specs=[ks, ks, vs, vs, ks, ss, pl.BlockSpec((1, dv), lambda b, c: (0, 0))],
        out_specs=[vs, ss],
        out_shape=[jax.ShapeDtypeStruct(v.shape, F32), jax.ShapeDtypeStruct(s0t.shape, F32)],
        scratch_shapes=[pltpu.VMEM((bb, dv, HK), F32)],
        compiler_params=_cparams(("parallel", "arbitrary")),
        name="gla",
    )(q, k, v, r, la, s0t, g)


def _gelu_tanh(x):
    return x * (0.5 * (1.0 + jnp.tanh(math.sqrt(2.0 / math.pi) * (x + 0.044715 * (x * x * x)))))


def _compress_mlp(seg_rows, w1_ref, b1_ref, w2_ref, b2_ref, bd_ref, g_ref, o_ref, *, n_pairs, nseg, norm):
    rk = w1_ref.shape[2] // LANES
    wh = w1_ref.shape[3]
    R = n_pairs * nseg
    p0 = jnp.zeros((R, wh), F32)
    p1 = jnp.zeros((R, wh), F32)
    for rr in range(CMP_STRIDE // rk):
        rows = jnp.concatenate(
            [jnp.concatenate([seg_rows(p, rr * rk + u) for u in range(rk)], axis=1) for p in range(n_pairs)],
            axis=0).astype(BF16)
        p0 = p0 + _dot(rows, w1_ref[0, rr])
        p1 = p1 + _dot(rows, w1_ref[1, rr])
    hid = b1_ref[...] + p0 + pltpu.roll(p1, R - 1, 0)
    y = _dot(_gelu_tanh(hid).astype(BF16), w2_ref[...]) + b2_ref[...]
    if norm:
        y = _head_rms(y, bd_ref[...], g_ref[...])
    y = jnp.where((_iota(y.shape, 0) & (nseg - 1)) < nseg - 1, y, 0.0)
    for p in range(n_pairs):
        o_ref[0, :, p * LANES:(p + 1) * LANES] = y[p * nseg:(p + 1) * nseg]


def _compress_kernel(*refs, n_pairs, nseg, norm):
    x_refs = refs[:n_pairs]
    _compress_mlp(lambda p, r: x_refs[p][0, pl.ds(r, nseg, stride=CMP_STRIDE), :], *refs[n_pairs:],
                  n_pairs=n_pairs, nseg=nseg, norm=norm)


def compress(x, n_rows, w1e, b1e, w2e, b2e, bd, g, norm):
    B, _, W = x.shape
    nseg = n_rows // CMP_STRIDE
    _log2(nseg)
    n_pairs = W // LANES
    return pl.pallas_call(
        functools.partial(_compress_kernel, n_pairs=n_pairs, nseg=nseg, norm=norm),
        grid=(B,),
        in_specs=[pl.BlockSpec((1, n_rows, LANES), functools.partial(lambda b, p: (b, 0, p), p=p))
                  for p in range(n_pairs)]
        + [_const_spec(a.shape) for a in (w1e, b1e, w2e, b2e, bd, g)],
        out_specs=pl.BlockSpec((1, nseg, W), lambda b: (b, 0, 0)),
        out_shape=jax.ShapeDtypeStruct((B, nseg, W), F32),
        compiler_params=_cparams(("parallel",)),
        name="compress",
    )(*([x] * n_pairs), w1e, b1e, w2e, b2e, bd, g)


def _compress_paged_kernel(pt_ref, *refs, group, n_pairs, nseg, norm):
    page_refs = refs[:group]
    xs = refs[-1]
    s = pl.program_id(1)
    page = page_refs[0].shape[3]
    for kk, pr in enumerate(page_refs):
        start = pl.multiple_of((s * group + kk) * page, page)
        for p in range(n_pairs):
            xs[p, pl.ds(start, page), :] = pr[0, 0, p * LANES:(p + 1) * LANES, :].T

    @pl.when(s == pl.num_programs(1) - 1)
    def _():
        _compress_mlp(lambda p, r: xs[p, pl.ds(r, nseg, stride=CMP_STRIDE), :], *refs[group:-1],
                      n_pairs=n_pairs, nseg=nseg, norm=norm)


def compress_paged(pool, layer, page_table, w1e, b1e, w2e, b2e, bd, g, norm):
    _, _, W, page = pool.shape
    B, n_pages = page_table.shape
    group = PAGE_GROUP
    n_rows = n_pages * page
    nseg = n_rows // CMP_STRIDE
    _log2(nseg)
    n_pairs = W // LANES

    def page_spec(kk):
        return pl.BlockSpec((1, 1, W, page), lambda b, s, pt: (layer, pt[b, s * group + kk], 0, 0))

    consts = (w1e, b1e, w2e, b2e, bd, g)
    grid_spec = pltpu.PrefetchScalarGridSpec(
        num_scalar_prefetch=1,
        grid=(B, n_pages // group),
        in_specs=[page_spec(kk) for kk in range(group)]
        + [pl.BlockSpec(a.shape, functools.partial(lambda b, s, pt, nd: (0,) * nd, nd=a.ndim),
                        pipeline_mode=pl.Buffered(1)) for a in consts],
        out_specs=pl.BlockSpec((1, nseg, W), lambda b, s, pt: (b, 0, 0)),
        scratch_shapes=[pltpu.VMEM((n_pairs, n_rows, LANES), F32)],
    )
    return pl.pallas_call(
        functools.partial(_compress_paged_kernel, group=group, n_pairs=n_pairs, nseg=nseg, norm=norm),
        grid_spec=grid_spec,
        out_shape=jax.ShapeDtypeStruct((B, nseg, W), F32),
        compiler_params=_cparams(("parallel", "arbitrary")),
        name="compress_paged",
    )(page_table, *([pool] * group), *consts)


_REL_EXACT = REL_BUCKETS // 2
_REL_THRESHOLDS = tuple(
    math.ceil(_REL_EXACT * (REL_MAX_DIST / _REL_EXACT) ** (k / (REL_BUCKETS - _REL_EXACT)) - 1e-9)
    for k in range(1, REL_BUCKETS - _REL_EXACT))


def _bias_kernel(base_ref, tbl_ref, o_ref, *, cstep):
    t = pl.program_id(0)
    _, H, R, C = o_ref.shape
    dist = base_ref[t] + _iota((R, C), 0) - cstep * _iota((R, C), 1)
    d = jnp.maximum(dist, 0)
    big = jnp.full((R, C), _REL_EXACT, jnp.int32)
    for th in _REL_THRESHOLDS:
        big = big + jnp.where(d >= th, 1, 0)
    bucket = jnp.where(d < _REL_EXACT, d, big)
    for h in range(H):
        acc = jnp.full((R, C), tbl_ref[h, REL_BUCKETS - 1], F32)
        for u in range(REL_BUCKETS - 1):
            acc = jnp.where(bucket == u, tbl_ref[h, u], acc)
        o_ref[0, h] = acc * LOG2E


def bias_tiles(rel_table, bases, R, C, cstep):
    H = rel_table.shape[1]
    T = bases.shape[0]
    grid_spec = pltpu.PrefetchScalarGridSpec(
        num_scalar_prefetch=1,
        grid=(T,),
        in_specs=[pl.BlockSpec(memory_space=pltpu.SMEM)],
        out_specs=pl.BlockSpec((1, H, R, C), lambda t, base: (t, 0, 0, 0)),
    )
    return pl.pallas_call(
        functools.partial(_bias_kernel, cstep=cstep),
        grid_spec=grid_spec,
        out_shape=jax.ShapeDtypeStruct((T, H, R, C), F32),
        compiler_params=_cparams(("arbitrary",)),
        name="bias_tiles",
    )(bases, rel_table.T)


def _masked_softmax(s, mask):
    s = jnp.where(mask, s, NEG)
    m = jnp.max(s, axis=-1, keepdims=True)
    p = jnp.where(mask, jnp.exp2(s - m), 0.0)
    return p * (1.0 / jnp.maximum(jnp.sum(p, axis=-1, keepdims=True), 1e-30))


def _off_mask(keep):
    return jnp.where(keep, 0.0, NEG)


def _softmax_update_add(s, m, l, acc, v, feature_major=False):
    m_new = jnp.maximum(m, jnp.max(s, axis=-1, keepdims=True))
    alpha = jnp.exp2(m - m_new)
    p = jnp.exp2(s - m_new)
    l = alpha * l + jnp.sum(p, axis=-1, keepdims=True)
    pb = p.astype(BF16)
    acc = alpha * acc + (_dot_nt(pb, v) if feature_major else _dot(pb, v))
    return m_new, l, acc


def _softmax_step(s, m, acc, v1):
    m_new = jnp.maximum(m, jnp.max(s, axis=-1, keepdims=True))
    p = jnp.exp2(s - m_new)
    return m_new, jnp.exp2(m - m_new) * acc + _dot(p.astype(BF16), v1)


def _softmax_init(R, W):
    return jnp.full((R, 1), NEG, F32), jnp.zeros((R, 1), F32), jnp.zeros((R, W), F32)


def _softmax_finish(l, acc):
    return acc * (1.0 / jnp.maximum(l, 1e-30))


def _top_blocks(score, n_real, n_top, axis=1):
    jidx = _iota(score.shape, axis)
    rank = jnp.zeros(score.shape, F32)
    for i in range(n_real):
        si = score[:, i:i + 1] if axis == 1 else score[i:i + 1, :]
        beats = (si > score) | ((si == score) & (jidx > i))
        rank = rank + jnp.where(beats, 1.0, 0.0)
    return jnp.where(rank < n_top, 1.0, 0.0)


def _block_scores(imp, q_pos, n_sel, axis=1):
    j = _iota(imp.shape, axis)
    cur = q_pos >> _log2(SLC_BLOCK)
    valid = j * SLC_BLOCK <= q_pos
    forced = valid & ((j == 0) | (j == cur) | (j == cur - 1))
    score = jnp.where(forced, FORCE, jnp.where(valid, imp, -FORCE))
    return jnp.where(j < n_sel, score, -3e38)


def _block_of_key_01(n_blocks, tk, pos0):
    j = _iota((n_blocks, tk), 0)
    c = _iota((n_blocks, tk), 1)
    return jnp.where(((pos0 + c) >> _log2(SLC_BLOCK)) == j, 1.0, 0.0).astype(BF16)


def _half_to(x, src_high, dst_high):
    return x if src_high == dst_high else pltpu.roll(x, HEAD_DIM, 1)


def _nsa_prompt_kernel(q_ref, g_ref, kc_ref, vc_ref, ks_ref, vs_ref, kw_ref, vw_ref, bd_ref, bc_ref, covt_ref,
                       brange_ref, o_ref, woff_sc, knorm_sc, *, tq, n_cmp, n_sel, n_top, group):
    i = pl.program_id(2)
    G = group
    R = G * tq
    nt = tq // LANES
    JP = _round_up(n_sel, 8)
    n_win = WINDOW // tq
    low = _iota((tq, LANES), 1) < HEAD_DIM
    SHIFT_LANE = HEAD_DIM - 1

    @pl.when(i == 0)
    def _():
        dist0 = _iota((tq, tq), 0) - _iota((tq, tq), 1)
        for dd in range(n_win + 1):
            woff_sc[dd] = _off_mask((dist0 + dd * tq >= 0) & (dist0 + dd * tq < WINDOW))
        for a, ref in enumerate((ks_ref, kw_ref)):
            k2 = ref[0] * ref[0]
            low_k = _iota(k2.shape, 1) < HEAD_DIM
            for hh in range(2):
                n2 = jnp.sum(jnp.where(low_k if hh == 0 else jnp.logical_not(low_k), k2, 0.0), axis=-1, keepdims=True)
                knorm_sc[2 * a + hh] = jnp.broadcast_to(jnp.sqrt(jnp.max(n2, axis=0, keepdims=True)), (8, LANES))

    def tile_off(dd):
        return jnp.concatenate([woff_sc[dd]] * G, axis=0)

    def prepare(hh):
        own = low if hh == 0 else jnp.logical_not(low)
        pieces = []
        for g in range(G):
            x = q_ref[0, :, (hh * 2 + g // 2) * LANES:(hh * 2 + g // 2 + 1) * LANES] * QSCALE
            pieces.append(_half_to(x, g % 2 == 1, hh == 1))
        qs = jnp.concatenate([jnp.where(own, x, 0.0) for x in pieces], axis=0).astype(BF16)

        bias_c = jnp.concatenate([bc_ref[a, hh * G + g] for g in range(G) for a in range(nt)], axis=0)
        s = _dot_nt(qs, kc_ref[0].astype(BF16)) + bias_c
        rowc = _iota(s.shape, 0) & (tq - 1)
        colc = _iota(s.shape, 1)
        dist_c = i * tq + rowc - (colc * CMP_STRIDE + (CMP_BLOCK - 1))
        p_c = _masked_softmax(s, (dist_c >= 0) & (colc < n_cmp)).astype(BF16)
        o_c = _dot(p_c, vc_ref[0].astype(BF16))
        covt = covt_ref[0:JP, :]
        impt = _dot_nt(covt, p_c[0:tq])
        for g in range(1, G):
            impt = impt + _dot_nt(covt, p_c[g * tq:(g + 1) * tq])
        q_pos = i * tq + _iota(impt.shape, 1)
        selt = _top_blocks(_block_scores(impt, q_pos, n_sel, 0), n_sel, n_top, 0)
        selt = jnp.concatenate([selt, jnp.zeros((LANES - JP, tq), F32)], axis=0)
        sel = selt.T

        sel_off = _half_to(_off_mask(sel > 0.5), False, hh == 0)
        qsel = jnp.concatenate([jnp.where(own, x, sel_off) for x in pieces], axis=0).astype(BF16)

        is_shift = (_iota((tq, LANES), 1) - (HEAD_DIM if hh == 0 else 0)) == SHIFT_LANE
        qn = [jnp.sqrt(jnp.sum(jnp.where(own, x * x, 0.0), axis=-1, keepdims=True)) for x in pieces]
        kn_s, kn_w = knorm_sc[hh][0:1, 0:1], knorm_sc[2 + hh][0:1, 0:1]
        bmax, bspan = brange_ref[0], brange_ref[1]
        q_slc = jnp.concatenate([jnp.where(own, x, jnp.where(is_shift, -(n * kn_s + bmax), sel_off))
                                 for x, n in zip(pieces, qn)], axis=0).astype(BF16)
        q_win = jnp.concatenate([jnp.where(own, x, jnp.where(is_shift, -(n * kn_w + bmax), 0.0))
                                 for x, n in zip(pieces, qn)], axis=0).astype(BF16)
        span = 2.0 * jnp.max(jnp.concatenate(qn, axis=0) * jnp.maximum(kn_s, kn_w)) + bspan
        return dict(qs=qs, qsel=qsel, q_slc=q_slc, q_win=q_win, o_c=o_c, fixed_shift=span <= MAX_SHIFT_SPAN)

    def tiles(hh, P):
        qs, qsel = P['qs'], P['qsel']
        own = low if hh == 0 else jnp.logical_not(low)
        spare_block = _iota((tq, LANES), 1) - (HEAD_DIM if hh == 0 else 0)
        is_shift = spare_block == SHIFT_LANE
        key_row = _iota((tq, LANES), 0)

        def values(ref, start):
            return jnp.where(own, ref[0, pl.ds(start, tq), :], 1.0).astype(BF16)

        def dense_bias(d):
            blocks = []
            for g in range(G):
                for a in range(nt):
                    blocks.append(jnp.concatenate(
                        [bd_ref[jnp.maximum(nt * d + a - b, 0), hh * G + g] for b in range(nt)], axis=1))
            return jnp.concatenate(blocks, axis=0)

        def slc_tile(j, carry, diag):
            start = pl.multiple_of(j * tq, tq)
            in_block = ((start + key_row) >> _log2(SLC_BLOCK)) == spare_block
            kt = jnp.where(own, ks_ref[0, pl.ds(start, tq), :], jnp.where(in_block, 1.0, 0.0)).astype(BF16)
            s = _dot_nt(qsel, kt) + dense_bias(i - j)
            if diag:
                s = s + tile_off(0)
            return _softmax_step(s, *carry, values(vs_ref, start))

        def win_tile(dd, carry):
            start = pl.multiple_of((i - dd) * tq, tq)
            kt = kw_ref[0, pl.ds(start, tq), :].astype(BF16)
            s = _dot_nt(qs, kt) + dense_bias(dd) + tile_off(dd)
            return _softmax_step(s, *carry, values(vw_ref, start))

        def running_max():
            init = (jnp.full((R, 1), NEG, F32), jnp.zeros((R, LANES), F32))
            carry = lax.fori_loop(0, i, lambda j, c: slc_tile(j, c, False), init)
            acc_s = lax.fori_loop(i, i + 1, lambda j, c: slc_tile(j, c, True), carry)[1]
            return acc_s, lax.fori_loop(0, jnp.minimum(i, n_win) + 1, win_tile, init)[1]

        def slc_shifted(j, acc, diag):
            start = pl.multiple_of(j * tq, tq)
            in_block = ((start + key_row) >> _log2(SLC_BLOCK)) == spare_block
            kt = jnp.where(own, ks_ref[0, pl.ds(start, tq), :],
                           jnp.where(in_block | is_shift, 1.0, 0.0)).astype(BF16)
            s = _dot_nt(P['q_slc'], kt) + dense_bias(i - j)
            if diag:
                s = s + tile_off(0)
            return acc + _dot(jnp.exp2(s).astype(BF16), values(vs_ref, start))

        def win_shifted(dd, acc):
            start = pl.multiple_of((i - dd) * tq, tq)
            kt = jnp.where(own, kw_ref[0, pl.ds(start, tq), :], jnp.where(is_shift, 1.0, 0.0)).astype(BF16)
            s = _dot_nt(P['q_win'], kt) + dense_bias(dd) + tile_off(dd)
            return acc + _dot(jnp.exp2(s).astype(BF16), values(vw_ref, start))

        return running_max, slc_shifted, win_shifted

    def attend(prepared):
        (run0, slc0, win0), (run1, slc1, win1) = [tiles(hh, prepared[hh]) for hh in range(2)]

        def fixed_shift():
            zero = jnp.zeros((R, LANES), F32)
            acc = lax.fori_loop(0, i, lambda j, a: (slc0(j, a[0], False), slc1(j, a[1], False)), (zero, zero))
            acc = lax.fori_loop(i, i + 1, lambda j, a: (slc0(j, a[0], True), slc1(j, a[1], True)), acc)
            win = lax.fori_loop(0, jnp.minimum(i, n_win) + 1, lambda dd, a: (win0(dd, a[0]), win1(dd, a[1])),
                                (zero, zero))
            return (acc[0], win[0]), (acc[1], win[1])

        both = prepared[0]['fixed_shift'] & prepared[1]['fixed_shift']
        return lax.cond(both, fixed_shift, lambda: (run0(), run1()))

    def finish(hh, o_c, acc_s, acc_w):
        def gate_col(br):
            return jnp.concatenate([g_ref[0, 0, :, hh * 3 * G + g * 3 + br:hh * 3 * G + g * 3 + br + 1]
                                    for g in range(G)], axis=0)

        def weight_col(br, acc):
            c = HEAD_DIM if hh == 0 else 0
            return gate_col(br) * (1.0 / jnp.maximum(acc[:, c:c + 1], 1e-30))

        o = gate_col(0) * o_c + weight_col(1, acc_s) * acc_s + weight_col(2, acc_w) * acc_w
        for vc in range(G // 2):
            lo_piece = _half_to(o[(2 * vc) * tq:(2 * vc + 1) * tq], hh == 1, False)
            hi_piece = _half_to(o[(2 * vc + 1) * tq:(2 * vc + 2) * tq], hh == 1, True)
            o_ref[0, :, (hh * 2 + vc) * LANES:(hh * 2 + vc + 1) * LANES] = jnp.where(low, lo_piece, hi_piece)

    prepared = [prepare(hh) for hh in range(2)]
    attended = attend(prepared)
    for hh in range(2):
        finish(hh, prepared[hh]['o_c'], *attended[hh])


def nsa_prompt(q, gates, kcmp, vcmp, ks, vs, kw, vw, bias_dense, bias_cmp, cover, bias_range, n_cmp, n_sel):
    B, L, WQ = q.shape
    WK = ks.shape[2]
    group = WQ // WK
    tq = 256
    nt = tq // LANES
    assert n_sel < HEAD_DIM
    n_pairs = WK // LANES
    n_top = min(SLC_TOPN, n_sel)
    qw = WQ // n_pairs
    hp = bias_dense.shape[1] // n_pairs
    kv = pl.BlockSpec((1, L, LANES), lambda b, p, i: (b, 0, p))
    cm = pl.BlockSpec((1, kcmp.shape[1], LANES), lambda b, p, i: (b, 0, p))
    return pl.pallas_call(
        functools.partial(_nsa_prompt_kernel, tq=tq, n_cmp=n_cmp, n_sel=n_sel, n_top=n_top, group=group),
        grid=(B, n_pairs, L // tq),
        in_specs=[
            pl.BlockSpec((1, tq, qw), lambda b, p, i: (b, i, p)),
            pl.BlockSpec((1, 1, tq, LANES), lambda b, p, i: (p, b, i, 0)),
            cm, cm, kv, kv, kv, kv,
            pl.BlockSpec((bias_dense.shape[0], hp, LANES, LANES), lambda b, p, i: (0, p, 0, 0)),
            pl.BlockSpec((nt, hp, LANES, LANES), lambda b, p, i: (i, p, 0, 0)),
            _const_spec(cover.shape),
            pl.BlockSpec(memory_space=pltpu.SMEM),
        ],
        out_specs=pl.BlockSpec((1, tq, qw), lambda b, p, i: (b, i, p)),
        out_shape=jax.ShapeDtypeStruct(q.shape, F32),
        scratch_shapes=[pltpu.VMEM((WINDOW // tq + 1, tq, tq), F32), pltpu.VMEM((4, 8, LANES), F32)],
        compiler_params=_cparams(("parallel", "parallel", "arbitrary")),
        name="nsa_prompt",
    )(q, gates, kcmp, vcmp, ks, vs, kw, vw, bias_dense, bias_cmp, cover, bias_range)


def _nsa_sample_kernel(pt_ref, q_ref, g_ref, kc_ref, vc_ref, ksn_ref, vsn_ref, kwp_ref, vwp_ref,
                       kwn_ref, vwn_ref, bs_ref, bn_ref, bc_ref, bw_ref, cov_ref, e_ref, *rest,
                       n_tok, q_base, n_cmp, n_sel, n_top, group, pages):
    k_refs, v_refs = rest[:pages], rest[pages:2 * pages]
    o_ref, qs_sc, sel_sc, oc_sc, m_sc, l_sc, acc_sc = rest[2 * pages:]
    ci = pl.program_id(1)
    R, W = qs_sc.shape
    n_heads = R // n_tok
    J = sel_sc.shape[1]

    def tok(shape):
        return _iota(shape, 0) & (n_tok - 1)

    @pl.when(ci == 0)
    def _():
        blocks = []
        for h in range(n_heads):
            hkv = h // group
            x = q_ref[0, :, (h // 2) * LANES:(h // 2 + 1) * LANES] * QSCALE
            x = _half_to(x, h % 2 == 1, hkv % 2 == 1)
            keep = (_iota(x.shape, 1) >= HEAD_DIM) if hkv % 2 == 1 else (_iota(x.shape, 1) < HEAD_DIM)
            x = jnp.where(keep, x, 0.0)
            blocks.append(jnp.concatenate(
                [x if cc == hkv // 2 else jnp.zeros_like(x) for cc in range(W // LANES)], axis=1))
        qs = jnp.concatenate(blocks, axis=0).astype(BF16)
        qs_sc[...] = qs
        s = _dot_nt(qs, kc_ref[0].astype(BF16)) + bc_ref[...]
        n = _iota(s.shape, 1)
        dist_c = q_base + tok(s.shape) - (n * CMP_STRIDE + (CMP_BLOCK - 1))
        p_c = _masked_softmax(s, (dist_c >= 0) & (n < n_cmp)).astype(BF16)
        oc_sc[...] = _dot(p_c, vc_ref[0].astype(BF16))
        ic = _dot(p_c, cov_ref[...])
        rows = group * n_tok
        imp_blocks = []
        for hkv in range(n_heads // group):
            a = ic[hkv * rows:hkv * rows + n_tok]
            for g in range(1, group):
                a = a + ic[hkv * rows + g * n_tok:hkv * rows + (g + 1) * n_tok]
            imp_blocks += [a] * group
        imp = jnp.concatenate(imp_blocks, axis=0)
        sel_sc[...] = _top_blocks(_block_scores(imp, q_base + tok(imp.shape), n_sel), n_sel, n_top)
        m0, l0, a0 = _softmax_init(R, W)
        m_sc[...] = m0
        l_sc[...] = l0
        acc_sc[...] = a0

    qs = qs_sc[...]
    sel = sel_sc[...].astype(BF16)

    kt = jnp.concatenate([r[0, 0] for r in k_refs], axis=1).astype(BF16)
    vt = jnp.concatenate([r[0, 0] for r in v_refs], axis=1).astype(BF16)
    s = _dot(qs, kt) + bs_ref[...] + _off_mask(_dot(sel, e_ref[...]) > 0.5)
    m, l, acc = _softmax_update_add(s, m_sc[...], l_sc[...], acc_sc[...], vt, True)
    m_sc[...] = m
    l_sc[...] = l
    acc_sc[...] = acc

    @pl.when(ci == pl.num_programs(1) - 1)
    def _():
        tn = ksn_ref.shape[1]
        c = _iota((R, tn), 1)
        causal_new = c <= tok((R, tn))
        s = _dot_nt(qs, ksn_ref[0].astype(BF16)) + bn_ref[...]
        chosen = _dot(sel, _block_of_key_01(J, tn, q_base))
        _, l1, a1 = _softmax_update_add(s + _off_mask((chosen > 0.5) & causal_new), m, l, acc,
                                        vsn_ref[0].astype(BF16))
        o_s = _softmax_finish(l1, a1)
        s = _dot(qs, kwp_ref[0, 0].astype(BF16)) + bw_ref[...]
        dist = WINDOW + tok(s.shape) - _iota(s.shape, 1)
        cw = _softmax_update_add(s + _off_mask((dist >= 0) & (dist < WINDOW)), *_softmax_init(R, W),
                                 vwp_ref[0, 0].astype(BF16), True)
        s = _dot_nt(qs, kwn_ref[0].astype(BF16)) + bn_ref[...]
        _, l2, a2 = _softmax_update_add(s + _off_mask(causal_new), *cw, vwn_ref[0].astype(BF16))
        o_w = _softmax_finish(l2, a2)

        def gate_col(br):
            cols = []
            for h in range(n_heads):
                hkv, g = divmod(h, group)
                cidx = (hkv % 2) * 3 * group + g * 3 + br
                cols.append(g_ref[hkv // 2, 0, :, cidx:cidx + 1])
            return jnp.concatenate(cols, axis=0)

        o = gate_col(0) * oc_sc[...] + gate_col(1) * o_s + gate_col(2) * o_w
        low = _iota((n_tok, LANES), 1) < HEAD_DIM
        for oc in range(n_heads // 2):
            hkv = (2 * oc) // group
            src = slice((hkv // 2) * LANES, (hkv // 2 + 1) * LANES)
            lo_piece = _half_to(o[(2 * oc) * n_tok:(2 * oc + 1) * n_tok, src], hkv % 2 == 1, False)
            hi_piece = _half_to(o[(2 * oc + 1) * n_tok:(2 * oc + 2) * n_tok, src], hkv % 2 == 1, True)
            o_ref[0, :, oc * LANES:(oc + 1) * LANES] = jnp.where(low, lo_piece, hi_piece)


def nsa_sample(q, gates, kcmp, vcmp, pool_k, pool_v, win_k, win_v, layer, page_table, ks_new, vs_new,
               kw_new, vw_new, bias_slc, bias_new, bias_cmp, bias_win, cover, n_cmp, n_sel):
    B, n_tok, WQ = q.shape
    WK, page = pool_k.shape[2], pool_k.shape[3]
    n_pages = page_table.shape[1]
    past = n_pages * page
    group = WQ // WK
    n_heads = WQ // HEAD_DIM
    R = n_heads * n_tok
    J = cover.shape[1]
    pages = PAGE_GROUP
    chunk = pages * page
    block_of_key = (jnp.arange(past)[None, :] // SLC_BLOCK == jnp.arange(J)[:, None]).astype(BF16)

    def bspec(a):
        return pl.BlockSpec((1,) + a.shape[1:], lambda b, c, pt: (b,) + (0,) * (a.ndim - 1))

    def cspec(a):
        return pl.BlockSpec(a.shape, lambda b, c, pt: (0,) * a.ndim, pipeline_mode=pl.Buffered(1))

    def page_spec(kk):
        return pl.BlockSpec((1, 1, WK, page), lambda b, c, pt: (layer, pt[b, c * pages + kk], 0, 0))

    wspec = pl.BlockSpec((1, 1, WK, win_k.shape[3]), lambda b, c, pt: (layer, b, 0, 0))
    grid_spec = pltpu.PrefetchScalarGridSpec(
        num_scalar_prefetch=1,
        grid=(B, n_pages // pages),
        in_specs=[
            bspec(q),
            pl.BlockSpec((gates.shape[0], 1, n_tok, LANES), lambda b, c, pt: (0, b, 0, 0)),
            bspec(kcmp), bspec(vcmp), bspec(ks_new), bspec(vs_new), wspec, wspec, bspec(kw_new), bspec(vw_new),
            pl.BlockSpec((R, chunk), lambda b, c, pt: (0, c)),
            cspec(bias_new), cspec(bias_cmp), cspec(bias_win), cspec(cover),
            pl.BlockSpec((J, chunk), lambda b, c, pt: (0, c)),
        ] + [page_spec(kk) for kk in range(pages)] * 2,
        out_specs=bspec(q),
        scratch_shapes=[pltpu.VMEM((R, WK), BF16), pltpu.VMEM((R, J), F32), pltpu.VMEM((R, WK), F32),
                        pltpu.VMEM((R, 1), F32), pltpu.VMEM((R, 1), F32), pltpu.VMEM((R, WK), F32)],
    )
    return pl.pallas_call(
        functools.partial(_nsa_sample_kernel, n_tok=n_tok, q_base=past, n_cmp=n_cmp, n_sel=n_sel,
                          n_top=min(SLC_TOPN, n_sel), group=group, pages=pages),
        grid_spec=grid_spec,
        out_shape=jax.ShapeDtypeStruct(q.shape, F32),
        compiler_params=_cparams(("parallel", "arbitrary")),
        name="nsa_sample",
    )(page_table, q, gates, kcmp, vcmp, ks_new, vs_new, win_k, win_v, kw_new, vw_new,
      bias_slc, bias_new, bias_cmp, bias_win, cover, block_of_key, *([pool_k] * pages), *([pool_v] * pages))


def _block_diag_ones(width):
    h = jnp.arange(width) // HEAD_DIM
    return (h[:, None] == h[None, :]).astype(BF16)


def _cover_01(n_cmp, n_sel, rows, cols):
    ci = jnp.arange(rows) * CMP_STRIDE
    sj = jnp.arange(cols) * SLC_BLOCK
    hit = (ci[:, None] < sj[None, :] + SLC_BLOCK) & (ci[:, None] + CMP_BLOCK > sj[None, :])
    hit = hit & (jnp.arange(rows)[:, None] < n_cmp) & (jnp.arange(cols)[None, :] < n_sel)
    return hit.astype(BF16)


def _round_up(n, m):
    return -(-n // m) * m


def _compress_weights(w1, b1, w2, b2, n_kv):
    hidden = w1.shape[1]
    eye = jnp.eye(n_kv, dtype=F32)
    w1r = w1.reshape(CMP_BLOCK // CMP_STRIDE, CMP_STRIDE, HEAD_DIM, hidden)
    rk = 2
    w1e = jnp.einsum('mrde,hk->mrhdke', w1r, eye).reshape(
        CMP_BLOCK // CMP_STRIDE, CMP_STRIDE // rk, rk * n_kv * HEAD_DIM, n_kv * hidden).astype(BF16)
    w2e = jnp.einsum('ed,hk->hekd', w2, eye).reshape(n_kv * hidden, n_kv * HEAD_DIM).astype(BF16)
    return w1e, jnp.tile(b1, n_kv)[None, :], w2e, jnp.tile(b2, n_kv)[None, :]


def _trunk(x, mods, past, P):
    B, L, D = x.shape
    depth = P['norm_g'].shape[0]
    even_states, odd_states = [], []
    for li in range(depth):
        def m(k):
            return mods[li, :, k][:, None, :]

        def g(k):
            return P['norm_g'][li, k][None, :]

        x = ffn(x, m(0), m(1), m(2), g(0), P['ffn_w_in'][li, 0], P['ffn_w_out'][li, 0])
        if li % 2 == 0:
            e = li // 2
            w = P['even_w_in'][e]
            n_sb = P['sb_width']
            HK, HV = P['gla_hk'], P['gla_hv']
            offs = [0, n_sb, 2 * n_sb, 3 * n_sb, 3 * n_sb + HK, 3 * n_sb + 2 * HK, 3 * n_sb + 2 * HK + HV,
                    3 * n_sb + 2 * HK + 2 * HV]
            ws = [w[:, offs[k]:offs[k + 1]] for k in range(7)]
            rank = w.shape[1] - offs[7]
            wa1 = jnp.pad(w[:, offs[7]:], ((0, 0), (0, LANES - rank)))
            wa2 = jnp.pad(P['gla_w_a2'][e], ((0, LANES - rank), (0, 0)))
            qa, ka, va, qb, kb, vb, rb, la = even_proj(x, m(3), m(4), g(1), ws, wa1, wa2, P['gla_b_a2'][e][None, :])
            dv = P['gla_onorm_g'].shape[1]
            n_gh = HV // dv
            dk = HK // n_gh
            if past is None:
                o_a = sb_prompt(qa, ka, va)
                s0t = jnp.zeros((B, dv, HK), F32)
            else:
                o_a = sb_sample(qa, ka, va, past['sb_k'], past['sb_v'], e, past['page_table'])
                s0t = past['gla'][e].transpose(0, 3, 1, 2).reshape(B, dv, HK)
            o_b, st = gla(qb, kb, vb, rb, la, s0t, P['gla_onorm_g'][e][None, :])
            wo = P['even_w_out'][e]
            x = out_proj(x, m(5), [o_a, o_b], [wo[:n_sb], wo[n_sb:]])
            s_fin = st.reshape(B, dv, n_gh, dk).transpose(0, 2, 3, 1)
            even_states.append((ka.reshape(B, L, -1, HEAD_DIM), va.reshape(B, L, -1, HEAD_DIM), s_fin))
        else:
            o_i = li // 2
            w = P['odd_w_in'][o_i]
            WQ, WK = P['nsa_wq'], P['nsa_wk']
            n_kv = WK // HEAD_DIM
            group = WQ // WK
            offs = [0, WQ] + [WQ + (k + 1) * WK for k in range(6)]
            ws = [w[:, offs[k]:offs[k + 1]] for k in range(7)]
            wg = w[:, offs[7]:]
            half = wg.shape[1] // 2
            ws += [jnp.pad(wg[:, :half], ((0, 0), (0, LANES - half))),
                   jnp.pad(wg[:, half:], ((0, 0), (0, LANES - half)))]
            qk_g = P['nsa_qk_g'][o_i]
            consts = [_block_diag_ones(WQ), _block_diag_ones(WK), jnp.tile(qk_g[0], WQ // HEAD_DIM)[None, :],
                      jnp.tile(qk_g[2], n_kv)[None, :], jnp.tile(qk_g[3], n_kv)[None, :]]
            q, kc, vc, ks, vs, kw, vw, gates = odd_proj(x, m(3), m(4), g(1), ws, consts)
            pair = LANES // HEAD_DIM
            gk = jnp.tile(qk_g[1], pair)[None, :]
            bdp = _block_diag_ones(LANES)
            cw = [_compress_weights(P['cmp_w1'][o_i, t], P['cmp_b1'][o_i, t], P['cmp_w2'][o_i, t],
                                    P['cmp_b2'][o_i, t], pair) for t in range(2)]
            rel = P['rel_table']
            if past is None:
                n_rows = (L // CMP_STRIDE) * CMP_STRIDE
                assert n_rows == L and L % LANES == 0
                n_cmp = L // CMP_STRIDE - CMP_BLOCK // CMP_STRIDE + 1
                n_sel = -(-L // SLC_BLOCK)
                kcmp = compress(kc, n_rows, *cw[0], bdp, gk, True)
                vcmp = compress(vc, n_rows, *cw[1], bdp, gk, False)
                assert kcmp.shape[1] == LANES and n_sel <= LANES
                nq = L // LANES
                bias_dense = bias_tiles(rel, jnp.arange(nq, dtype=jnp.int32) * LANES, LANES, LANES, 1)
                bias_cmp = bias_tiles(rel, jnp.arange(nq, dtype=jnp.int32) * LANES - (CMP_BLOCK - 1),
                                      LANES, LANES, CMP_STRIDE)
                cover = _cover_01(n_cmp, n_sel, LANES, LANES).T
                bias_range = jnp.stack([jnp.max(rel), jnp.max(rel) - jnp.min(rel)]) * LOG2E
                o = nsa_prompt(q, gates, kcmp, vcmp, ks, vs, kw, vw, bias_dense, bias_cmp, cover, bias_range,
                               n_cmp, n_sel)
                w_keep = min(WINDOW, L)
                win_k, win_v = kw[:, L - w_keep:], vw[:, L - w_keep:]
            else:
                pt = past['page_table']
                past_len = pt.shape[1] * past['cmp_k'].shape[3]
                T = past_len + L
                n_rows = (T // CMP_STRIDE) * CMP_STRIDE
                assert n_rows == past_len and past['win_k'].shape[3] == WINDOW and L <= LANES
                n_cmp = n_rows // CMP_STRIDE - CMP_BLOCK // CMP_STRIDE + 1
                n_sel = -(-T // SLC_BLOCK)
                kcmp = compress_paged(past['cmp_k'], o_i, pt, *cw[0], bdp, gk, True)
                vcmp = compress_paged(past['cmp_v'], o_i, pt, *cw[1], bdp, gk, False)
                J = _round_up(n_sel, LANES)
                n_heads = WQ // HEAD_DIM
                R = n_heads * L

                def sample_bias(base, C, cstep):
                    return bias_tiles(rel, jnp.array([base], jnp.int32), L, C, cstep).reshape(R, C)

                bias_slc = sample_bias(past_len, past_len, 1)
                bias_new = sample_bias(0, LANES, 1)
                bias_cmp = sample_bias(past_len - (CMP_BLOCK - 1), kcmp.shape[1], CMP_STRIDE)
                bias_win = sample_bias(WINDOW, WINDOW, 1)
                cover = _cover_01(n_cmp, n_sel, kcmp.shape[1], J)
                pad = ((0, 0), (0, LANES - L), (0, 0))
                o = nsa_sample(q, gates, kcmp, vcmp, past['slc_k'], past['slc_v'], past['win_k'], past['win_v'],
                               o_i, pt, jnp.pad(ks, pad), jnp.pad(vs, pad), jnp.pad(kw, pad), jnp.pad(vw, pad),
                               bias_slc, bias_new, bias_cmp, bias_win, cover, n_cmp, n_sel)
                win_k = jnp.concatenate([past['win_k_rows'][o_i], kw], axis=1)[:, L:]
                win_v = jnp.concatenate([past['win_v_rows'][o_i], vw], axis=1)[:, L:]
            x = out_proj(x, m(5), [o], [P['odd_w_out'][o_i]])

            def rows(t):
                return t.reshape(B, t.shape[1], n_kv, HEAD_DIM)

            odd_states.append(tuple(rows(t) for t in (kc, vc, ks, vs, win_k, win_v)))
        x = ffn(x, m(6), m(7), m(8), g(2), P['ffn_w_in'][li, 1], P['ffn_w_out'][li, 1])
    even_new = [jnp.stack([s[i] for s in even_states]) for i in range(3)]
    odd_new = [jnp.stack([s[i] for s in odd_states]) for i in range(6)]
    return x, even_new, odd_new


def kernel(x_prompt, x_sample, cache_sb_k, cache_sb_v, state_gla, cache_cmp_k, cache_cmp_v, cache_slc_k,
           cache_slc_v, cache_win_k, cache_win_v, page_table, c_prompt, c_sample, norm_g, ada_w, ada_b,
           ffn_w_in, ffn_w_out, even_w_in, gla_w_a2, gla_b_a2, gla_onorm_g, even_w_out, odd_w_in, nsa_qk_g,
           cmp_w1, cmp_b1, cmp_w2, cmp_b2, rel_table, odd_w_out):
    D = x_prompt.shape[-1]
    depth = norm_g.shape[0]
    Bp, Bs = x_prompt.shape[0], x_sample.shape[0]
    mods = ada_mod(jnp.concatenate([c_prompt, c_sample], axis=0), ada_w, ada_b).reshape(depth, Bp + Bs, 9, D)
    n_sb = cache_sb_k.shape[3] * cache_sb_k.shape[4]
    dv = gla_onorm_g.shape[1]
    HK = gla_w_a2.shape[2]
    n_gh = state_gla.shape[2]
    WK = cache_cmp_k.shape[3] * cache_cmp_k.shape[4]
    P = {
        'norm_g': norm_g, 'ffn_w_in': ffn_w_in.astype(BF16), 'ffn_w_out': ffn_w_out.astype(BF16),
        'even_w_in': even_w_in.astype(BF16), 'gla_w_a2': gla_w_a2.astype(BF16), 'gla_b_a2': gla_b_a2,
        'gla_onorm_g': gla_onorm_g, 'even_w_out': even_w_out.astype(BF16), 'odd_w_in': odd_w_in.astype(BF16),
        'nsa_qk_g': nsa_qk_g, 'cmp_w1': cmp_w1, 'cmp_b1': cmp_b1, 'cmp_w2': cmp_w2, 'cmp_b2': cmp_b2,
        'rel_table': rel_table, 'odd_w_out': odd_w_out.astype(BF16),
        'sb_width': n_sb, 'gla_hk': HK, 'gla_hv': n_gh * dv, 'nsa_wq': odd_w_out.shape[1], 'nsa_wk': WK,
    }

    def pool(c):
        return c.transpose(0, 1, 3, 4, 2).reshape(c.shape[0], c.shape[1], c.shape[3] * c.shape[4], c.shape[2])

    def rows(c):
        return c.reshape(c.shape[0], c.shape[1], c.shape[2], c.shape[3] * c.shape[4])

    past = {
        'page_table': page_table, 'sb_k': pool(cache_sb_k), 'sb_v': pool(cache_sb_v), 'gla': state_gla,
        'cmp_k': pool(cache_cmp_k), 'cmp_v': pool(cache_cmp_v), 'slc_k': pool(cache_slc_k),
        'slc_v': pool(cache_slc_v), 'win_k': pool(cache_win_k), 'win_v': pool(cache_win_v),
        'win_k_rows': rows(cache_win_k), 'win_v_rows': rows(cache_win_v),
    }
    y_p, p_even, p_odd = _trunk(x_prompt, mods[:, :Bp], None, P)
    y_s, s_even, s_odd = _trunk(x_sample, mods[:, Bp:], past, P)
    return (y_p, y_s, *p_even, *p_odd, *s_even, *s_odd)
```

```python
import functools
import math

import jax
import jax.numpy as jnp
from jax import lax
from jax.experimental import pallas as pl
from jax.experimental.pallas import tpu as pltpu

F32 = jnp.float32
BF16 = jnp.bfloat16

EPS = 1e-6
NEG = -1e30
FORCE = 1e6
HEAD_DIM = 64
GLA_TAU = 16.0
GLA_CHUNK = 64
CMP_BLOCK = 32
CMP_STRIDE = 16
SLC_BLOCK = 64
SLC_TOPN = 16
WINDOW = 512
REL_BUCKETS = 32
REL_MAX_DIST = 1024
PAGE_GROUP = 16
LANES = 128
VMEM_LIMIT = 56 * 1024 * 1024


def _cparams(sem, vmem=VMEM_LIMIT):
    return pltpu.CompilerParams(dimension_semantics=sem, vmem_limit_bytes=vmem)


def _dot(a, b):
    return jnp.dot(a, b, preferred_element_type=F32)


def _dot_nt(a, b):
    return lax.dot_general(a, b, (((1,), (1,)), ((), ())), preferred_element_type=F32)


def _dot_tn(a, b):
    return lax.dot_general(a, b, (((0,), (0,)), ((), ())), preferred_element_type=F32)


def _split2(x):
    hi = x.astype(BF16)
    lo = (x - hi.astype(F32)).astype(BF16)
    return hi, lo


def _dot_f32_by01(x, m01):
    hi, lo = _split2(x)
    return _dot(hi, m01) + _dot(lo, m01)


def _dot_01_by_f32(m01, x):
    hi, lo = _split2(x)
    return _dot(m01, hi) + _dot(m01, lo)


def _softplus(z):
    return jnp.maximum(z, 0.0) + jnp.log1p(jnp.exp(-jnp.abs(z)))


LOG2E = 1.4426950408889634
QSCALE = HEAD_DIM ** -0.5 * LOG2E
MAX_SHIFT_SPAN = 100.0


def _sb_logs2(z2):
    l2 = jnp.log(1.0 + jnp.exp2(-jnp.abs(z2))) * LOG2E
    lk = -(jnp.maximum(z2, 0.0) + l2)
    return lk, z2 + lk


def _silu(a):
    return a * jax.nn.sigmoid(a)


def _iota(shape, dim):
    return lax.broadcasted_iota(jnp.int32, shape, dim)


def _modulate(x, g, shift, scale):
    ms = jnp.mean(x * x, axis=-1, keepdims=True)
    y = x * lax.rsqrt(ms + EPS) * g
    return y * (1.0 + scale) + shift


def _head_rms(x, bd, g):
    ms = _dot_f32_by01(x * x, bd) * (1.0 / HEAD_DIM)
    return x * lax.rsqrt(ms + EPS) * g


def _const_spec(shape):
    nd = len(shape)
    return pl.BlockSpec(shape, lambda *_: (0,) * nd, pipeline_mode=pl.Buffered(1))


def _row_tiles(B, L):
    if L >= 512:
        return 1, 512
    return B, L


def _ada_kernel(c_ref, w_ref, b_ref, o_ref):
    c = c_ref[...]
    a = _silu(c).astype(BF16)
    o_ref[0] = _dot(a, w_ref[0].astype(BF16)) + b_ref[0]


def ada_mod(c, ada_w, ada_b):
    R, D = c.shape
    depth, _, N = ada_w.shape
    tn = 1024
    return pl.pallas_call(
        _ada_kernel,
        grid=(depth, N // tn),
        in_specs=[
            pl.BlockSpec((R, D), lambda l, n: (0, 0)),
            pl.BlockSpec((1, D, tn), lambda l, n: (l, 0, n)),
            pl.BlockSpec((1, 1, tn), lambda l, n: (l, 0, n)),
        ],
        out_specs=pl.BlockSpec((1, R, tn), lambda l, n: (l, 0, n)),
        out_shape=jax.ShapeDtypeStruct((depth, R, N), F32),
        compiler_params=_cparams(("parallel", "parallel")),
        name="ada_mod",
    )(c, ada_w, ada_b.reshape(depth, 1, N))


def _ffn_kernel(x_ref, sh_ref, sc_ref, gt_ref, g_ref, win_ref, wout_ref, *rest, d_ff, tf, n_mix):
    o_ref = rest[-1]
    x = x_ref[...]
    tb, tl, D = x.shape
    if n_mix:
        mix = jnp.zeros((tb * tl, D), F32)
        for a, w in zip(rest[1:1 + n_mix], rest[1 + n_mix:1 + 2 * n_mix]):
            av = a[...]
            mix = mix + _dot(av.reshape(tb * tl, av.shape[-1]).astype(BF16), w[...])
        x = x + rest[0][...] * mix.reshape(tb, tl, D)
    h = _modulate(x, g_ref[...], sh_ref[...], sc_ref[...])
    hb = h.reshape(tb * tl, D).astype(BF16)
    acc = jnp.zeros((tb * tl, D), F32)
    for j in range(d_ff // tf):
        a = _dot(hb, win_ref[:, j * tf:(j + 1) * tf])
        b = _dot(hb, win_ref[:, d_ff + j * tf:d_ff + (j + 1) * tf])
        u = (_silu(a) * b).astype(BF16)
        acc = acc + _dot(u, wout_ref[j * tf:(j + 1) * tf, :])
    o_ref[...] = x + 0.5 * gt_ref[...] * acc.reshape(tb, tl, D)


def ffn(x, shift, scale, gate, g, w_in, w_out, mix_gate=None, mix_acts=(), mix_ws=()):
    B, L, D = x.shape
    d_ff = w_out.shape[0]
    tb, tl = _row_tiles(B, L)
    xs = pl.BlockSpec((tb, tl, D), lambda b, l: (b, l, 0))
    ms = pl.BlockSpec((tb, 1, D), lambda b, l: (b, 0, 0))
    n_mix = len(mix_acts)
    mix_specs = []
    if n_mix:
        mix_specs = [ms] + [pl.BlockSpec((tb, tl, a.shape[-1]), lambda b, l: (b, l, 0)) for a in mix_acts] \
            + [_const_spec(w.shape) for w in mix_ws]
    return pl.pallas_call(
        functools.partial(_ffn_kernel, d_ff=d_ff, tf=256, n_mix=n_mix),
        grid=(B // tb, L // tl),
        in_specs=[xs, ms, ms, ms, _const_spec((1, D)), _const_spec(w_in.shape), _const_spec(w_out.shape)]
        + mix_specs,
        out_specs=xs,
        out_shape=jax.ShapeDtypeStruct(x.shape, F32),
        compiler_params=_cparams(("parallel", "parallel")),
        name="ffn",
    )(x, shift, scale, gate, g, w_in, w_out, *(([mix_gate] if n_mix else []) + list(mix_acts) + list(mix_ws)))


def _even_proj_kernel(x_ref, sh_ref, sc_ref, g_ref, wqa, wka, wva, wqb, wkb, wvb, wrb, wa1, wa2, ba2,
                      qa, ka, va, qb, kb, vb, rb, la):
    x = x_ref[...]
    tb, tl, D = x.shape
    h = _modulate(x, g_ref[...], sh_ref[...], sc_ref[...])
    hb = h.reshape(tb * tl, D).astype(BF16)
    for w, o in ((wqa, qa), (wka, ka), (wva, va), (wqb, qb), (wkb, kb), (wvb, vb), (wrb, rb)):
        o[...] = _dot(hb, w[...]).reshape(o.shape)
    a1 = _dot(hb, wa1[...]).astype(BF16)
    z = _dot(a1, wa2[...]) + ba2[...]
    la[...] = (-_softplus(-z) * (1.0 / GLA_TAU)).reshape(la.shape)


def even_proj(x, shift, scale, g, ws, wa1, wa2, ba2):
    B, L, D = x.shape
    tb, tl = _row_tiles(B, L)
    xs = pl.BlockSpec((tb, tl, D), lambda b, l: (b, l, 0))
    ms = pl.BlockSpec((tb, 1, D), lambda b, l: (b, 0, 0))
    widths = [w.shape[1] for w in ws] + [wa2.shape[1]]
    return pl.pallas_call(
        _even_proj_kernel,
        grid=(B // tb, L // tl),
        in_specs=[xs, ms, ms, _const_spec((1, D))] + [_const_spec(w.shape) for w in ws]
        + [_const_spec(wa1.shape), _const_spec(wa2.shape), _const_spec(ba2.shape)],
        out_specs=[pl.BlockSpec((tb, tl, n), lambda b, l: (b, l, 0)) for n in widths],
        out_shape=[jax.ShapeDtypeStruct((B, L, n), F32) for n in widths],
        compiler_params=_cparams(("parallel", "parallel")),
        name="even_proj",
    )(x, shift, scale, g, *ws, wa1, wa2, ba2)


def _odd_proj_kernel(x_ref, sh_ref, sc_ref, g_ref, wq, wkc, wvc, wks, wvs, wkw, wvw, wg0, wg1,
                     hq, hk, gq, gks, gkw,
                     q, kc, vc, ks, vs, kw, vw, gates):
    x = x_ref[...]
    tb, tl, D = x.shape
    h = _modulate(x, g_ref[...], sh_ref[...], sc_ref[...])
    hb = h.reshape(tb * tl, D).astype(BF16)
    q[...] = _head_rms(_dot(hb, wq[...]), hq[...], gq[...]).reshape(q.shape)
    kc[...] = _dot(hb, wkc[...]).reshape(kc.shape)
    vc[...] = _dot(hb, wvc[...]).reshape(vc.shape)
    ks[...] = _head_rms(_dot(hb, wks[...]), hk[...], gks[...]).reshape(ks.shape)
    vs[...] = _dot(hb, wvs[...]).reshape(vs.shape)
    kw[...] = _head_rms(_dot(hb, wkw[...]), hk[...], gkw[...]).reshape(kw.shape)
    vw[...] = _dot(hb, wvw[...]).reshape(vw.shape)
    gates[0] = jax.nn.sigmoid(_dot(hb, wg0[...])).reshape(gates.shape[1:])
    gates[1] = jax.nn.sigmoid(_dot(hb, wg1[...])).reshape(gates.shape[1:])


def odd_proj(x, shift, scale, g, ws, consts):
    B, L, D = x.shape
    tb, tl = _row_tiles(B, L)
    xs = pl.BlockSpec((tb, tl, D), lambda b, l: (b, l, 0))
    ms = pl.BlockSpec((tb, 1, D), lambda b, l: (b, 0, 0))
    widths = [w.shape[1] for w in ws[:7]]
    out_specs = [pl.BlockSpec((tb, tl, n), lambda b, l: (b, l, 0)) for n in widths]
    out_specs.append(pl.BlockSpec((2, tb, tl, LANES), lambda b, l: (0, b, l, 0)))
    out_shape = [jax.ShapeDtypeStruct((B, L, n), F32) for n in widths]
    out_shape.append(jax.ShapeDtypeStruct((2, B, L, LANES), F32))
    return pl.pallas_call(
        _odd_proj_kernel,
        grid=(B // tb, L // tl),
        in_specs=[xs, ms, ms, _const_spec((1, D))] + [_const_spec(w.shape) for w in ws]
        + [_const_spec(c.shape) for c in consts],
        out_specs=out_specs,
        out_shape=out_shape,
        compiler_params=_cparams(("parallel", "parallel")),
        name="odd_proj",
    )(x, shift, scale, g, *ws, *consts)


def _log2(n):
    assert n > 0 and n & (n - 1) == 0, n
    return n.bit_length() - 1


def _later_keys_01(tk):
    return jnp.where(_iota((tk, tk), 0) > _iota((tk, tk), 1), 1.0, 0.0).astype(BF16)


def _sb_tile(q2, kt, vt, run, acc, u01, before, feature_major=False):
    lk, ls = _sb_logs2(_dot(q2, kt) if feature_major else _dot_nt(q2, kt))
    if before is not None:
        lk = jnp.where(before, lk, 0.0)
    cs = _dot_f32_by01(lk, u01)
    w = jnp.exp2(ls + cs + run)
    if before is not None:
        w = jnp.where(before, w, 0.0)
    wb = w.astype(BF16)
    acc = acc + (_dot_nt(wb, vt) if feature_major else _dot(wb, vt))
    run = run + cs[:, 0:1] + lk[:, 0:1]
    return run, acc


def _sb_prompt_kernel(q_ref, k_ref, v_ref, o_ref, *, tq):
    i = pl.program_id(2)
    n_pairs = q_ref.shape[2] // LANES
    lane = _iota((tq, LANES), 1)
    u01 = _later_keys_01(tq)
    before = _iota((2 * tq, tq), 1) < (_iota((2 * tq, tq), 0) & (tq - 1))
    q2 = []
    for p in range(n_pairs):
        q = q_ref[0, :, p * LANES:(p + 1) * LANES] * QSCALE
        q2.append(jnp.concatenate([jnp.where(lane < HEAD_DIM, q, 0.0), jnp.where(lane >= HEAD_DIM, q, 0.0)],
                                  axis=0).astype(BF16))

    def tile(j, carry, mask):
        start = pl.multiple_of(j * tq, tq)
        out = []
        for p in range(n_pairs):
            kt = k_ref[0, pl.ds(start, tq), p * LANES:(p + 1) * LANES].astype(BF16)
            vt = v_ref[0, pl.ds(start, tq), p * LANES:(p + 1) * LANES].astype(BF16)
            out.append(_sb_tile(q2[p], kt, vt, *carry[p], u01, mask))
        return tuple(out)

    init = tuple((jnp.zeros((2 * tq, 1), F32), jnp.zeros((2 * tq, LANES), F32)) for _ in range(n_pairs))
    carry = lax.fori_loop(i, i + 1, lambda j, c: tile(j, c, before), init)
    carry = lax.fori_loop(1, i + 1, lambda jj, c: tile(i - jj, c, None), carry)
    for p in range(n_pairs):
        acc = carry[p][1]
        o_ref[0, :, p * LANES:(p + 1) * LANES] = jnp.where(lane < HEAD_DIM, acc[:tq], acc[tq:])


def sb_prompt(q, k, v):
    B, L, W = q.shape
    tq = 256
    wb = 2 * LANES
    qs = pl.BlockSpec((1, tq, wb), lambda b, p, i: (b, i, p))
    ks = pl.BlockSpec((1, L, wb), lambda b, p, i: (b, 0, p))
    return pl.pallas_call(
        functools.partial(_sb_prompt_kernel, tq=tq),
        grid=(B, W // wb, L // tq),
        in_specs=[qs, ks, ks],
        out_specs=qs,
        out_shape=jax.ShapeDtypeStruct(q.shape, F32),
        compiler_params=_cparams(("parallel", "parallel", "arbitrary")),
        name="sb_prompt",
    )(q, k, v)


def _sb_sample_kernel(pt_ref, q_ref, kn_ref, vn_ref, *rest, n_tok, group):
    k_refs, v_refs = rest[:group], rest[group:2 * group]
    o_ref, run_ref, acc_ref = rest[2 * group:]
    s = pl.program_id(1)
    q = q_ref[0] * QSCALE
    W = q.shape[1]
    n_heads = W // HEAD_DIM
    R = n_heads * n_tok
    tk = kn_ref.shape[1]
    own = (_iota((R, W), 0) >> _log2(n_tok)) == (_iota((R, W), 1) >> _log2(HEAD_DIM))
    q2 = jnp.where(own, jnp.concatenate([q] * n_heads, axis=0), 0.0).astype(BF16)
    u01 = _later_keys_01(tk)

    @pl.when(s == 0)
    def _():
        before = _iota((R, tk), 1) < (_iota((R, tk), 0) & (n_tok - 1))
        run, acc = _sb_tile(q2, kn_ref[0].astype(BF16), vn_ref[0].astype(BF16),
                            jnp.zeros((R, 1), F32), jnp.zeros((R, W), F32), u01, before)
        run_ref[...] = run
        acc_ref[...] = acc

    @pl.when(s > 0)
    def _():
        kt = jnp.concatenate([r[0, 0] for r in k_refs], axis=1).astype(BF16)
        vt = jnp.concatenate([r[0, 0] for r in v_refs], axis=1).astype(BF16)

        def stack(x):
            return jnp.concatenate([x[:, kk * tk:(kk + 1) * tk] for kk in range(group)], axis=0)

        lk, ls = _sb_logs2(_dot(q2, kt))
        lk = stack(lk)
        cs = _dot_f32_by01(lk, u01)
        tot = cs[:, 0:1] + lk[:, 0:1]
        run = run_ref[...]
        runs = []
        for kk in range(group):
            runs.append(run)
            run = run + tot[kk * R:(kk + 1) * R]
        w = jnp.exp2(stack(ls) + cs + jnp.concatenate(runs, axis=0))
        w = jnp.concatenate([w[kk * R:(kk + 1) * R] for kk in range(group)], axis=1).astype(BF16)
        acc_ref[...] = acc_ref[...] + _dot_nt(w, vt)
        run_ref[...] = run

    @pl.when(s == pl.num_programs(1) - 1)
    def _():
        acc = jnp.where(own, acc_ref[...], 0.0)
        out = acc[0:n_tok]
        for h in range(1, n_heads):
            out = out + acc[h * n_tok:(h + 1) * n_tok]
        o_ref[0] = out


def sb_sample(q, k_new, v_new, pool_k, pool_v, layer, page_table):
    B, n_tok, W = q.shape
    page = pool_k.shape[3]
    n_pages = page_table.shape[1]
    group = PAGE_GROUP
    n_steps = 1 + n_pages // group
    pad = ((0, 0), (0, page - n_tok), (0, 0))
    kn, vn = jnp.pad(k_new, pad), jnp.pad(v_new, pad)

    def page_spec(kk):
        def imap(b, s, pt):
            return (layer, pt[b, n_pages - 1 - (jnp.maximum(s, 1) - 1) * group - kk], 0, 0)
        return pl.BlockSpec((1, 1, W, page), imap)

    qs = pl.BlockSpec((1, n_tok, W), lambda b, s, pt: (b, 0, 0))
    ns = pl.BlockSpec((1, page, W), lambda b, s, pt: (b, 0, 0))
    R = (W // HEAD_DIM) * n_tok
    grid_spec = pltpu.PrefetchScalarGridSpec(
        num_scalar_prefetch=1,
        grid=(B, n_steps),
        in_specs=[qs, ns, ns] + [page_spec(kk) for kk in range(group)] * 2,
        out_specs=qs,
        scratch_shapes=[pltpu.VMEM((R, 1), F32), pltpu.VMEM((R, W), F32)],
    )
    return pl.pallas_call(
        functools.partial(_sb_sample_kernel, n_tok=n_tok, group=group),
        grid_spec=grid_spec,
        out_shape=jax.ShapeDtypeStruct(q.shape, F32),
        compiler_params=_cparams(("parallel", "arbitrary")),
        name="sb_sample",
    )(page_table, q, kn, vn, *([pool_k] * group), *([pool_v] * group))


def _gla_kernel(q_ref, k_ref, v_ref, r_ref, la_ref, s0_ref, g_ref, o_ref, sfin_ref, st, *, dk, dv):
    c = pl.program_id(1)

    @pl.when(c == 0)
    def _():
        st[...] = s0_ref[...]

    for ib in range(q_ref.shape[0]):
        _gla_chunk(q_ref.at[ib], k_ref.at[ib], v_ref.at[ib], r_ref.at[ib], la_ref.at[ib], g_ref, o_ref.at[ib],
                   st.at[ib], dk=dk, dv=dv)

    @pl.when(c == pl.num_programs(1) - 1)
    def _():
        sfin_ref[...] = st[...]


def _gla_chunk(q_ref, k_ref, v_ref, r_ref, la_ref, g_ref, o_ref, st, *, dk, dv):
    q = q_ref[...] * (dk ** -0.5)
    k = k_ref[...]
    la = la_ref[...]
    C, HK = q.shape
    n_heads = HK // dk
    causal = _iota((C, C), 0) >= _iota((C, C), 1)
    ltri = jnp.where(causal, 1.0, 0.0).astype(BF16)
    hi = la.astype(BF16)
    r1 = la - hi.astype(F32)
    mid = r1.astype(BF16)
    lo = (r1 - mid.astype(F32)).astype(BF16)
    b = _dot(ltri, hi) + _dot(ltri, mid) + _dot(ltri, lo)
    b_last = b[C - 1:C, :]
    q_in = q * jnp.exp(b)
    k_in = (k * jnp.exp(-b)).astype(BF16)
    k_dec = k * jnp.exp(b_last - b)
    lane_h = _iota((C, HK), 1) >> _log2(dk)
    s_old = st[...]
    s_old_b = s_old.astype(BF16)
    upd = jnp.zeros(s_old.shape, F32)
    for h in range(n_heads):
        own = lane_h == h
        qh = jnp.where(own, q_in, 0.0).astype(BF16)
        att = jnp.where(causal, _dot_nt(qh, k_in), 0.0)
        vh = v_ref[:, h * dv:(h + 1) * dv].astype(BF16)
        oh = _dot(att.astype(BF16), vh) + _dot_nt(qh, s_old_b)
        y = oh * lax.rsqrt(jnp.mean(oh * oh, axis=-1, keepdims=True) + EPS) * g_ref[...]
        o_ref[:, h * dv:(h + 1) * dv] = y * _silu(r_ref[:, h * dv:(h + 1) * dv])
        kd = jnp.where(own, k_dec, 0.0).astype(BF16)
        upd = upd + _dot_tn(vh, kd)
    st[...] = s_old * jnp.exp(b_last) + upd


def gla(q, k, v, r, la, s0t, g):
    B, L, HK = q.shape
    HV = v.shape[2]
    dv = g.shape[1]
    dk = HK // (HV // dv)
    C = min(GLA_CHUNK, L)
    bb = next(n for n in (4, 2, 1) if B % n == 0)
    ks = pl.BlockSpec((bb, C, HK), lambda b, c: (b, c, 0))
    vs = pl.BlockSpec((bb, C, HV), lambda b, c: (b, c, 0))
    ss = pl.BlockSpec((bb, dv, HK), lambda b, c: (b, 0, 0))
    return pl.pallas_call(
        functools.partial(_gla_kernel, dk=dk, dv=dv),
        grid=(B // bb, L // C),
        in_specs=[ks, ks, vs, vs, ks, ss, pl.BlockSpec((1, dv), lambda b, c: (0, 0))],
        out_specs=[vs, ss],
        out_shape=[jax.ShapeDtypeStruct(v.shape, F32), jax.ShapeDtypeStruct(s0t.shape, F32)],
        scratch_shapes=[pltpu.VMEM((bb, dv, HK), F32)],
        compiler_params=_cparams(("parallel", "arbitrary")),
        name="gla",
    )(q, k, v, r, la, s0t, g)


def _gelu_tanh(x):
    return x * (0.5 * (1.0 + jnp.tanh(math.sqrt(2.0 / math.pi) * (x + 0.044715 * (x * x * x)))))


def _compress_mlp(seg_rows, w1_ref, b1_ref, w2_ref, b2_ref, bd_ref, g_ref, o_ref, *, n_pairs, nseg, norm):
    rk = w1_ref.shape[2] // LANES
    wh = w1_ref.shape[3]
    R = n_pairs * nseg
    p0 = jnp.zeros((R, wh), F32)
    p1 = jnp.zeros((R, wh), F32)
    for rr in range(CMP_STRIDE // rk):
        rows = jnp.concatenate(
            [jnp.concatenate([seg_rows(p, rr * rk + u) for u in range(rk)], axis=1) for p in range(n_pairs)],
            axis=0).astype(BF16)
        p0 = p0 + _dot(rows, w1_ref[0, rr])
        p1 = p1 + _dot(rows, w1_ref[1, rr])
    hid = b1_ref[...] + p0 + pltpu.roll(p1, R - 1, 0)
    y = _dot(_gelu_tanh(hid).astype(BF16), w2_ref[...]) + b2_ref[...]
    if norm:
        y = _head_rms(y, bd_ref[...], g_ref[...])
    y = jnp.where((_iota(y.shape, 0) & (nseg - 1)) < nseg - 1, y, 0.0)
    for p in range(n_pairs):
        o_ref[0, :, p * LANES:(p + 1) * LANES] = y[p * nseg:(p + 1) * nseg]


def _compress_kernel(*refs, n_pairs, nseg, norm):
    x_refs = refs[:n_pairs]
    _compress_mlp(lambda p, r: x_refs[p][0, pl.ds(r, nseg, stride=CMP_STRIDE), :], *refs[n_pairs:],
                  n_pairs=n_pairs, nseg=nseg, norm=norm)


def compress(x, n_rows, w1e, b1e, w2e, b2e, bd, g, norm):
    B, _, W = x.shape
    nseg = n_rows // CMP_STRIDE
    _log2(nseg)
    n_pairs = W // LANES
    return pl.pallas_call(
        functools.partial(_compress_kernel, n_pairs=n_pairs, nseg=nseg, norm=norm),
        grid=(B,),
        in_specs=[pl.BlockSpec((1, n_rows, LANES), functools.partial(lambda b, p: (b, 0, p), p=p))
                  for p in range(n_pairs)]
        + [_const_spec(a.shape) for a in (w1e, b1e, w2e, b2e, bd, g)],
        out_specs=pl.BlockSpec((1, nseg, W), lambda b: (b, 0, 0)),
        out_shape=jax.ShapeDtypeStruct((B, nseg, W), F32),
        compiler_params=_cparams(("parallel",)),
        name="compress",
    )(*([x] * n_pairs), w1e, b1e, w2e, b2e, bd, g)


def _compress_paged_kernel(pt_ref, *refs, group, n_pairs, nseg, norm):
    page_refs = refs[:group]
    xs = refs[-1]
    s = pl.program_id(1)
    page = page_refs[0].shape[3]
    for kk, pr in enumerate(page_refs):
        start = pl.multiple_of((s * group + kk) * page, page)
        for p in range(n_pairs):
            xs[p, pl.ds(start, page), :] = pr[0, 0, p * LANES:(p + 1) * LANES, :].T

    @pl.when(s == pl.num_programs(1) - 1)
    def _():
        _compress_mlp(lambda p, r: xs[p, pl.ds(r, nseg, stride=CMP_STRIDE), :], *refs[group:-1],
                      n_pairs=n_pairs, nseg=nseg, norm=norm)


def compress_paged(pool, layer, page_table, w1e, b1e, w2e, b2e, bd, g, norm):
    _, _, W, page = pool.shape
    B, n_pages = page_table.shape
    group = PAGE_GROUP
    n_rows = n_pages * page
    nseg = n_rows // CMP_STRIDE
    _log2(nseg)
    n_pairs = W // LANES

    def page_spec(kk):
        return pl.BlockSpec((1, 1, W, page), lambda b, s, pt: (layer, pt[b, s * group + kk], 0, 0))

    consts = (w1e, b1e, w2e, b2e, bd, g)
    grid_spec = pltpu.PrefetchScalarGridSpec(
        num_scalar_prefetch=1,
        grid=(B, n_pages // group),
        in_specs=[page_spec(kk) for kk in range(group)]
        + [pl.BlockSpec(a.shape, functools.partial(lambda b, s, pt, nd: (0,) * nd, nd=a.ndim),
                        pipeline_mode=pl.Buffered(1)) for a in consts],
        out_specs=pl.BlockSpec((1, nseg, W), lambda b, s, pt: (b, 0, 0)),
        scratch_shapes=[pltpu.VMEM((n_pairs, n_rows, LANES), F32)],
    )
    return pl.pallas_call(
        functools.partial(_compress_paged_kernel, group=group, n_pairs=n_pairs, nseg=nseg, norm=norm),
        grid_spec=grid_spec,
        out_shape=jax.ShapeDtypeStruct((B, nseg, W), F32),
        compiler_params=_cparams(("parallel", "arbitrary")),
        name="compress_paged",
    )(page_table, *([pool] * group), *consts)


_REL_EXACT = REL_BUCKETS // 2
_REL_THRESHOLDS = tuple(
    math.ceil(_REL_EXACT * (REL_MAX_DIST / _REL_EXACT) ** (k / (REL_BUCKETS - _REL_EXACT)) - 1e-9)
    for k in range(1, REL_BUCKETS - _REL_EXACT))


def _bias_kernel(base_ref, tbl_ref, o_ref, *, cstep):
    t = pl.program_id(0)
    _, H, R, C = o_ref.shape
    dist = base_ref[t] + _iota((R, C), 0) - cstep * _iota((R, C), 1)
    d = jnp.maximum(dist, 0)
    big = jnp.full((R, C), _REL_EXACT, jnp.int32)
    for th in _REL_THRESHOLDS:
        big = big + jnp.where(d >= th, 1, 0)
    bucket = jnp.where(d < _REL_EXACT, d, big)
    for h in range(H):
        acc = jnp.full((R, C), tbl_ref[h, REL_BUCKETS - 1], F32)
        for u in range(REL_BUCKETS - 1):
            acc = jnp.where(bucket == u, tbl_ref[h, u], acc)
        o_ref[0, h] = acc * LOG2E


def bias_tiles(rel_table, bases, R, C, cstep):
    H = rel_table.shape[1]
    T = bases.shape[0]
    grid_spec = pltpu.PrefetchScalarGridSpec(
        num_scalar_prefetch=1,
        grid=(T,),
        in_specs=[pl.BlockSpec(memory_space=pltpu.SMEM)],
        out_specs=pl.BlockSpec((1, H, R, C), lambda t, base: (t, 0, 0, 0)),
    )
    return pl.pallas_call(
        functools.partial(_bias_kernel, cstep=cstep),
        grid_spec=grid_spec,
        out_shape=jax.ShapeDtypeStruct((T, H, R, C), F32),
        compiler_params=_cparams(("arbitrary",)),
        name="bias_tiles",
    )(bases, rel_table.T)


def _masked_softmax(s, mask):
    s = jnp.where(mask, s, NEG)
    m = jnp.max(s, axis=-1, keepdims=True)
    p = jnp.where(mask, jnp.exp2(s - m), 0.0)
    return p * (1.0 / jnp.maximum(jnp.sum(p, axis=-1, keepdims=True), 1e-30))


def _off_mask(keep):
    return jnp.where(keep, 0.0, NEG)


def _softmax_update_add(s, m, l, acc, v, feature_major=False):
    m_new = jnp.maximum(m, jnp.max(s, axis=-1, keepdims=True))
    alpha = jnp.exp2(m - m_new)
    p = jnp.exp2(s - m_new)
    l = alpha * l + jnp.sum(p, axis=-1, keepdims=True)
    pb = p.astype(BF16)
    acc = alpha * acc + (_dot_nt(pb, v) if feature_major else _dot(pb, v))
    return m_new, l, acc


def _softmax_step(s, m, acc, v1):
    m_new = jnp.maximum(m, jnp.max(s, axis=-1, keepdims=True))
    p = jnp.exp2(s - m_new)
    return m_new, jnp.exp2(m - m_new) * acc + _dot(p.astype(BF16), v1)


def _softmax_init(R, W):
    return jnp.full((R, 1), NEG, F32), jnp.zeros((R, 1), F32), jnp.zeros((R, W), F32)


def _softmax_finish(l, acc):
    return acc * (1.0 / jnp.maximum(l, 1e-30))


def _top_blocks(score, n_real, n_top, axis=1):
    jidx = _iota(score.shape, axis)
    rank = jnp.zeros(score.shape, F32)
    for i in range(n_real):
        si = score[:, i:i + 1] if axis == 1 else score[i:i + 1, :]
        beats = (si > score) | ((si == score) & (jidx > i))
        rank = rank + jnp.where(beats, 1.0, 0.0)
    return jnp.where(rank < n_top, 1.0, 0.0)


def _block_scores(imp, q_pos, n_sel, axis=1):
    j = _iota(imp.shape, axis)
    cur = q_pos >> _log2(SLC_BLOCK)
    valid = j * SLC_BLOCK <= q_pos
    forced = valid & ((j == 0) | (j == cur) | (j == cur - 1))
    score = jnp.where(forced, FORCE, jnp.where(valid, imp, -FORCE))
    return jnp.where(j < n_sel, score, -3e38)


def _block_of_key_01(n_blocks, tk, pos0):
    j = _iota((n_blocks, tk), 0)
    c = _iota((n_blocks, tk), 1)
    return jnp.where(((pos0 + c) >> _log2(SLC_BLOCK)) == j, 1.0, 0.0).astype(BF16)


def _half_to(x, src_high, dst_high):
    return x if src_high == dst_high else pltpu.roll(x, HEAD_DIM, 1)


def _nsa_prompt_kernel(q_ref, g_ref, kc_ref, vc_ref, ks_ref, vs_ref, kw_ref, vw_ref, bd_ref, bc_ref, covt_ref,
                       brange_ref, o_ref, woff_sc, knorm_sc, *, tq, n_cmp, n_sel, n_top, group):
    i = pl.program_id(2)
    G = group
    R = G * tq
    nt = tq // LANES
    JP = _round_up(n_sel, 8)
    n_win = WINDOW // tq
    low = _iota((tq, LANES), 1) < HEAD_DIM
    SHIFT_LANE = HEAD_DIM - 1

    @pl.when(i == 0)
    def _():
        dist0 = _iota((tq, tq), 0) - _iota((tq, tq), 1)
        for dd in range(n_win + 1):
            woff_sc[dd] = _off_mask((dist0 + dd * tq >= 0) & (dist0 + dd * tq < WINDOW))
        for a, ref in enumerate((ks_ref, kw_ref)):
            k2 = ref[0] * ref[0]
            low_k = _iota(k2.shape, 1) < HEAD_DIM
            for hh in range(2):
                n2 = jnp.sum(jnp.where(low_k if hh == 0 else jnp.logical_not(low_k), k2, 0.0), axis=-1, keepdims=True)
                knorm_sc[2 * a + hh] = jnp.broadcast_to(jnp.sqrt(jnp.max(n2, axis=0, keepdims=True)), (8, LANES))

    def tile_off(dd):
        return jnp.concatenate([woff_sc[dd]] * G, axis=0)

    def prepare(hh):
        own = low if hh == 0 else jnp.logical_not(low)
        pieces = []
        for g in range(G):
            x = q_ref[0, :, (hh * 2 + g // 2) * LANES:(hh * 2 + g // 2 + 1) * LANES] * QSCALE
            pieces.append(_half_to(x, g % 2 == 1, hh == 1))
        qs = jnp.concatenate([jnp.where(own, x, 0.0) for x in pieces], axis=0).astype(BF16)

        bias_c = jnp.concatenate([bc_ref[a, hh * G + g] for g in range(G) for a in range(nt)], axis=0)
        s = _dot_nt(qs, kc_ref[0].astype(BF16)) + bias_c
        rowc = _iota(s.shape, 0) & (tq - 1)
        colc = _iota(s.shape, 1)
        dist_c = i * tq + rowc - (colc * CMP_STRIDE + (CMP_BLOCK - 1))
        p_c = _masked_softmax(s, (dist_c >= 0) & (colc < n_cmp)).astype(BF16)
        o_c = _dot(p_c, vc_ref[0].astype(BF16))
        covt = covt_ref[0:JP, :]
        impt = _dot_nt(covt, p_c[0:tq])
        for g in range(1, G):
            impt = impt + _dot_nt(covt, p_c[g * tq:(g + 1) * tq])
        q_pos = i * tq + _iota(impt.shape, 1)
        selt = _top_blocks(_block_scores(impt, q_pos, n_sel, 0), n_sel, n_top, 0)
        selt = jnp.concatenate([selt, jnp.zeros((LANES - JP, tq), F32)], axis=0)
        sel = selt.T

        sel_off = _half_to(_off_mask(sel > 0.5), False, hh == 0)
        qsel = jnp.concatenate([jnp.where(own, x, sel_off) for x in pieces], axis=0).astype(BF16)

        is_shift = (_iota((tq, LANES), 1) - (HEAD_DIM if hh == 0 else 0)) == SHIFT_LANE
        qn = [jnp.sqrt(jnp.sum(jnp.where(own, x * x, 0.0), axis=-1, keepdims=True)) for x in pieces]
        kn_s, kn_w = knorm_sc[hh][0:1, 0:1], knorm_sc[2 + hh][0:1, 0:1]
        bmax, bspan = brange_ref[0], brange_ref[1]
        q_slc = jnp.concatenate([jnp.where(own, x, jnp.where(is_shift, -(n * kn_s + bmax), sel_off))
                                 for x, n in zip(pieces, qn)], axis=0).astype(BF16)
        q_win = jnp.concatenate([jnp.where(own, x, jnp.where(is_shift, -(n * kn_w + bmax), 0.0))
                                 for x, n in zip(pieces, qn)], axis=0).astype(BF16)
        span = 2.0 * jnp.max(jnp.concatenate(qn, axis=0) * jnp.maximum(kn_s, kn_w)) + bspan
        return dict(qs=qs, qsel=qsel, q_slc=q_slc, q_win=q_win, o_c=o_c, fixed_shift=span <= MAX_SHIFT_SPAN)

    def tiles(hh, P):
        qs, qsel = P['qs'], P['qsel']
        own = low if hh == 0 else jnp.logical_not(low)
        spare_block = _iota((tq, LANES), 1) - (HEAD_DIM if hh == 0 else 0)
        is_shift = spare_block == SHIFT_LANE
        key_row = _iota((tq, LANES), 0)

        def values(ref, start):
            return jnp.where(own, ref[0, pl.ds(start, tq), :], 1.0).astype(BF16)

        def dense_bias(d):
            blocks = []
            for g in range(G):
                for a in range(nt):
                    blocks.append(jnp.concatenate(
                        [bd_ref[jnp.maximum(nt * d + a - b, 0), hh * G + g] for b in range(nt)], axis=1))
            return jnp.concatenate(blocks, axis=0)

        def slc_tile(j, carry, diag):
            start = pl.multiple_of(j * tq, tq)
            in_block = ((start + key_row) >> _log2(SLC_BLOCK)) == spare_block
            kt = jnp.where(own, ks_ref[0, pl.ds(start, tq), :], jnp.where(in_block, 1.0, 0.0)).astype(BF16)
            s = _dot_nt(qsel, kt) + dense_bias(i - j)
            if diag:
                s = s + tile_off(0)
            return _softmax_step(s, *carry, values(vs_ref, start))

        def win_tile(dd, carry):
            start = pl.multiple_of((i - dd) * tq, tq)
            kt = kw_ref[0, pl.ds(start, tq), :].astype(BF16)
            s = _dot_nt(qs, kt) + dense_bias(dd) + tile_off(dd)
            return _softmax_step(s, *carry, values(vw_ref, start))

        def running_max():
            init = (jnp.full((R, 1), NEG, F32), jnp.zeros((R, LANES), F32))
            carry = lax.fori_loop(0, i, lambda j, c: slc_tile(j, c, False), init)
            acc_s = lax.fori_loop(i, i + 1, lambda j, c: slc_tile(j, c, True), carry)[1]
            return acc_s, lax.fori_loop(0, jnp.minimum(i, n_win) + 1, win_tile, init)[1]

        def slc_shifted(j, acc, diag):
            start = pl.multiple_of(j * tq, tq)
            in_block = ((start + key_row) >> _log2(SLC_BLOCK)) == spare_block
            kt = jnp.where(own, ks_ref[0, pl.ds(start, tq), :],
                           jnp.where(in_block | is_shift, 1.0, 0.0)).astype(BF16)
            s = _dot_nt(P['q_slc'], kt) + dense_bias(i - j)
            if diag:
                s = s + tile_off(0)
            return acc + _dot(jnp.exp2(s).astype(BF16), values(vs_ref, start))

        def win_shifted(dd, acc):
            start = pl.multiple_of((i - dd) * tq, tq)
            kt = jnp.where(own, kw_ref[0, pl.ds(start, tq), :], jnp.where(is_shift, 1.0, 0.0)).astype(BF16)
            s = _dot_nt(P['q_win'], kt) + dense_bias(dd) + tile_off(dd)
            return acc + _dot(jnp.exp2(s).astype(BF16), values(vw_ref, start))

        return running_max, slc_shifted, win_shifted

    def attend(prepared):
        (run0, slc0, win0), (run1, slc1, win1) = [tiles(hh, prepared[hh]) for hh in range(2)]

        def fixed_shift():
            zero = jnp.zeros((R, LANES), F32)
            acc = lax.fori_loop(0, i, lambda j, a: (slc0(j, a[0], False), slc1(j, a[1], False)), (zero, zero))
            acc = lax.fori_loop(i, i + 1, lambda j, a: (slc0(j, a[0], True), slc1(j, a[1], True)), acc)
            win = lax.fori_loop(0, jnp.minimum(i, n_win) + 1, lambda dd, a: (win0(dd, a[0]), win1(dd, a[1])),
                                (zero, zero))
            return (acc[0], win[0]), (acc[1], win[1])

        both = prepared[0]['fixed_shift'] & prepared[1]['fixed_shift']
        return lax.cond(both, fixed_shift, lambda: (run0(), run1()))

    def finish(hh, o_c, acc_s, acc_w):
        def gate_col(br):
            return jnp.concatenate([g_ref[0, 0, :, hh * 3 * G + g * 3 + br:hh * 3 * G + g * 3 + br + 1]
                                    for g in range(G)], axis=0)

        def weight_col(br, acc):
            c = HEAD_DIM if hh == 0 else 0
            return gate_col(br) * (1.0 / jnp.maximum(acc[:, c:c + 1], 1e-30))

        o = gate_col(0) * o_c + weight_col(1, acc_s) * acc_s + weight_col(2, acc_w) * acc_w
        for vc in range(G // 2):
            lo_piece = _half_to(o[(2 * vc) * tq:(2 * vc + 1) * tq], hh == 1, False)
            hi_piece = _half_to(o[(2 * vc + 1) * tq:(2 * vc + 2) * tq], hh == 1, True)
            o_ref[0, :, (hh * 2 + vc) * LANES:(hh * 2 + vc + 1) * LANES] = jnp.where(low, lo_piece, hi_piece)

    prepared = [prepare(hh) for hh in range(2)]
    attended = attend(prepared)
    for hh in range(2):
        finish(hh, prepared[hh]['o_c'], *attended[hh])


def nsa_prompt(q, gates, kcmp, vcmp, ks, vs, kw, vw, bias_dense, bias_cmp, cover, bias_range, n_cmp, n_sel):
    B, L, WQ = q.shape
    WK = ks.shape[2]
    group = WQ // WK
    tq = 256
    nt = tq // LANES
    assert n_sel < HEAD_DIM
    n_pairs = WK // LANES
    n_top = min(SLC_TOPN, n_sel)
    qw = WQ // n_pairs
    hp = bias_dense.shape[1] // n_pairs
    kv = pl.BlockSpec((1, L, LANES), lambda b, p, i: (b, 0, p))
    cm = pl.BlockSpec((1, kcmp.shape[1], LANES), lambda b, p, i: (b, 0, p))
    return pl.pallas_call(
        functools.partial(_nsa_prompt_kernel, tq=tq, n_cmp=n_cmp, n_sel=n_sel, n_top=n_top, group=group),
        grid=(B, n_pairs, L // tq),
        in_specs=[
            pl.BlockSpec((1, tq, qw), lambda b, p, i: (b, i, p)),
            pl.BlockSpec((1, 1, tq, LANES), lambda b, p, i: (p, b, i, 0)),
            cm, cm, kv, kv, kv, kv,
            pl.BlockSpec((bias_dense.shape[0], hp, LANES, LANES), lambda b, p, i: (0, p, 0, 0)),
            pl.BlockSpec((nt, hp, LANES, LANES), lambda b, p, i: (i, p, 0, 0)),
            _const_spec(cover.shape),
            pl.BlockSpec(memory_space=pltpu.SMEM),
        ],
        out_specs=pl.BlockSpec((1, tq, qw), lambda b, p, i: (b, i, p)),
        out_shape=jax.ShapeDtypeStruct(q.shape, F32),
        scratch_shapes=[pltpu.VMEM((WINDOW // tq + 1, tq, tq), F32), pltpu.VMEM((4, 8, LANES), F32)],
        compiler_params=_cparams(("parallel", "parallel", "arbitrary")),
        name="nsa_prompt",
    )(q, gates, kcmp, vcmp, ks, vs, kw, vw, bias_dense, bias_cmp, cover, bias_range)


def _nsa_sample_kernel(pt_ref, q_ref, g_ref, kc_ref, vc_ref, ksn_ref, vsn_ref, kwp_ref, vwp_ref,
                       kwn_ref, vwn_ref, bs_ref, bn_ref, bc_ref, bw_ref, cov_ref, e_ref, *rest,
                       n_tok, q_base, n_cmp, n_sel, n_top, group, pages):
    k_refs, v_refs = rest[:pages], rest[pages:2 * pages]
    o_ref, qs_sc, sel_sc, oc_sc, m_sc, l_sc, acc_sc = rest[2 * pages:]
    ci = pl.program_id(1)
    R, W = qs_sc.shape
    n_heads = R // n_tok
    J = sel_sc.shape[1]

    def tok(shape):
        return _iota(shape, 0) & (n_tok - 1)

    @pl.when(ci == 0)
    def _():
        blocks = []
        for h in range(n_heads):
            hkv = h // group
            x = q_ref[0, :, (h // 2) * LANES:(h // 2 + 1) * LANES] * QSCALE
            x = _half_to(x, h % 2 == 1, hkv % 2 == 1)
            keep = (_iota(x.shape, 1) >= HEAD_DIM) if hkv % 2 == 1 else (_iota(x.shape, 1) < HEAD_DIM)
            x = jnp.where(keep, x, 0.0)
            blocks.append(jnp.concatenate(
                [x if cc == hkv // 2 else jnp.zeros_like(x) for cc in range(W // LANES)], axis=1))
        qs = jnp.concatenate(blocks, axis=0).astype(BF16)
        qs_sc[...] = qs
        s = _dot_nt(qs, kc_ref[0].astype(BF16)) + bc_ref[...]
        n = _iota(s.shape, 1)
        dist_c = q_base + tok(s.shape) - (n * CMP_STRIDE + (CMP_BLOCK - 1))
        p_c = _masked_softmax(s, (dist_c >= 0) & (n < n_cmp)).astype(BF16)
        oc_sc[...] = _dot(p_c, vc_ref[0].astype(BF16))
        ic = _dot(p_c, cov_ref[...])
        rows = group * n_tok
        imp_blocks = []
        for hkv in range(n_heads // group):
            a = ic[hkv * rows:hkv * rows + n_tok]
            for g in range(1, group):
                a = a + ic[hkv * rows + g * n_tok:hkv * rows + (g + 1) * n_tok]
            imp_blocks += [a] * group
        imp = jnp.concatenate(imp_blocks, axis=0)
        sel_sc[...] = _top_blocks(_block_scores(imp, q_base + tok(imp.shape), n_sel), n_sel, n_top)
        m0, l0, a0 = _softmax_init(R, W)
        m_sc[...] = m0
        l_sc[...] = l0
        acc_sc[...] = a0

    qs = qs_sc[...]
    sel = sel_sc[...].astype(BF16)

    kt = jnp.concatenate([r[0, 0] for r in k_refs], axis=1).astype(BF16)
    vt = jnp.concatenate([r[0, 0] for r in v_refs], axis=1).astype(BF16)
    s = _dot(qs, kt) + bs_ref[...] + _off_mask(_dot(sel, e_ref[...]) > 0.5)
    m, l, acc = _softmax_update_add(s, m_sc[...], l_sc[...], acc_sc[...], vt, True)
    m_sc[...] = m
    l_sc[...] = l
    acc_sc[...] = acc

    @pl.when(ci == pl.num_programs(1) - 1)
    def _():
        tn = ksn_ref.shape[1]
        c = _iota((R, tn), 1)
        causal_new = c <= tok((R, tn))
        s = _dot_nt(qs, ksn_ref[0].astype(BF16)) + bn_ref[...]
        chosen = _dot(sel, _block_of_key_01(J, tn, q_base))
        _, l1, a1 = _softmax_update_add(s + _off_mask((chosen > 0.5) & causal_new), m, l, acc,
                                        vsn_ref[0].astype(BF16))
        o_s = _softmax_finish(l1, a1)
        s = _dot(qs, kwp_ref[0, 0].astype(BF16)) + bw_ref[...]
        dist = WINDOW + tok(s.shape) - _iota(s.shape, 1)
        cw = _softmax_update_add(s + _off_mask((dist >= 0) & (dist < WINDOW)), *_softmax_init(R, W),
                                 vwp_ref[0, 0].astype(BF16), True)
        s = _dot_nt(qs, kwn_ref[0].astype(BF16)) + bn_ref[...]
        _, l2, a2 = _softmax_update_add(s + _off_mask(causal_new), *cw, vwn_ref[0].astype(BF16))
        o_w = _softmax_finish(l2, a2)

        def gate_col(br):
            cols = []
            for h in range(n_heads):
                hkv, g = divmod(h, group)
                cidx = (hkv % 2) * 3 * group + g * 3 + br
                cols.append(g_ref[hkv // 2, 0, :, cidx:cidx + 1])
            return jnp.concatenate(cols, axis=0)

        o = gate_col(0) * oc_sc[...] + gate_col(1) * o_s + gate_col(2) * o_w
        low = _iota((n_tok, LANES), 1) < HEAD_DIM
        for oc in range(n_heads // 2):
            hkv = (2 * oc) // group
            src = slice((hkv // 2) * LANES, (hkv // 2 + 1) * LANES)
            lo_piece = _half_to(o[(2 * oc) * n_tok:(2 * oc + 1) * n_tok, src], hkv % 2 == 1, False)
            hi_piece = _half_to(o[(2 * oc + 1) * n_tok:(2 * oc + 2) * n_tok, src], hkv % 2 == 1, True)
            o_ref[0, :, oc * LANES:(oc + 1) * LANES] = jnp.where(low, lo_piece, hi_piece)


def nsa_sample(q, gates, kcmp, vcmp, pool_k, pool_v, win_k, win_v, layer, page_table, ks_new, vs_new,
               kw_new, vw_new, bias_slc, bias_new, bias_cmp, bias_win, cover, n_cmp, n_sel):
    B, n_tok, WQ = q.shape
    WK, page = pool_k.shape[2], pool_k.shape[3]
    n_pages = page_table.shape[1]
    past = n_pages * page
    group = WQ // WK
    n_heads = WQ // HEAD_DIM
    R = n_heads * n_tok
    J = cover.shape[1]
    pages = PAGE_GROUP
    chunk = pages * page
    block_of_key = (jnp.arange(past)[None, :] // SLC_BLOCK == jnp.arange(J)[:, None]).astype(BF16)

    def bspec(a):
        return pl.BlockSpec((1,) + a.shape[1:], lambda b, c, pt: (b,) + (0,) * (a.ndim - 1))

    def cspec(a):
        return pl.BlockSpec(a.shape, lambda b, c, pt: (0,) * a.ndim, pipeline_mode=pl.Buffered(1))

    def page_spec(kk):
        return pl.BlockSpec((1, 1, WK, page), lambda b, c, pt: (layer, pt[b, c * pages + kk], 0, 0))

    wspec = pl.BlockSpec((1, 1, WK, win_k.shape[3]), lambda b, c, pt: (layer, b, 0, 0))
    grid_spec = pltpu.PrefetchScalarGridSpec(
        num_scalar_prefetch=1,
        grid=(B, n_pages // pages),
        in_specs=[
            bspec(q),
            pl.BlockSpec((gates.shape[0], 1, n_tok, LANES), lambda b, c, pt: (0, b, 0, 0)),
            bspec(kcmp), bspec(vcmp), bspec(ks_new), bspec(vs_new), wspec, wspec, bspec(kw_new), bspec(vw_new),
            pl.BlockSpec((R, chunk), lambda b, c, pt: (0, c)),
            cspec(bias_new), cspec(bias_cmp), cspec(bias_win), cspec(cover),
            pl.BlockSpec((J, chunk), lambda b, c, pt: (0, c)),
        ] + [page_spec(kk) for kk in range(pages)] * 2,
        out_specs=bspec(q),
        scratch_shapes=[pltpu.VMEM((R, WK), BF16), pltpu.VMEM((R, J), F32), pltpu.VMEM((R, WK), F32),
                        pltpu.VMEM((R, 1), F32), pltpu.VMEM((R, 1), F32), pltpu.VMEM((R, WK), F32)],
    )
    return pl.pallas_call(
        functools.partial(_nsa_sample_kernel, n_tok=n_tok, q_base=past, n_cmp=n_cmp, n_sel=n_sel,
                          n_top=min(SLC_TOPN, n_sel), group=group, pages=pages),
        grid_spec=grid_spec,
        out_shape=jax.ShapeDtypeStruct(q.shape, F32),
        compiler_params=_cparams(("parallel", "arbitrary")),
        name="nsa_sample",
    )(page_table, q, gates, kcmp, vcmp, ks_new, vs_new, win_k, win_v, kw_new, vw_new,
      bias_slc, bias_new, bias_cmp, bias_win, cover, block_of_key, *([pool_k] * pages), *([pool_v] * pages))


def _block_diag_ones(width):
    h = jnp.arange(width) // HEAD_DIM
    return (h[:, None] == h[None, :]).astype(BF16)


def _cover_01(n_cmp, n_sel, rows, cols):
    ci = jnp.arange(rows) * CMP_STRIDE
    sj = jnp.arange(cols) * SLC_BLOCK
    hit = (ci[:, None] < sj[None, :] + SLC_BLOCK) & (ci[:, None] + CMP_BLOCK > sj[None, :])
    hit = hit & (jnp.arange(rows)[:, None] < n_cmp) & (jnp.arange(cols)[None, :] < n_sel)
    return hit.astype(BF16)


def _round_up(n, m):
    return -(-n // m) * m


def _compress_weights(w1, b1, w2, b2, n_kv):
    hidden = w1.shape[1]
    eye = jnp.eye(n_kv, dtype=F32)
    w1r = w1.reshape(CMP_BLOCK // CMP_STRIDE, CMP_STRIDE, HEAD_DIM, hidden)
    rk = 2
    w1e = jnp.einsum('mrde,hk->mrhdke', w1r, eye).reshape(
        CMP_BLOCK // CMP_STRIDE, CMP_STRIDE // rk, rk * n_kv * HEAD_DIM, n_kv * hidden).astype(BF16)
    w2e = jnp.einsum('ed,hk->hekd', w2, eye).reshape(n_kv * hidden, n_kv * HEAD_DIM).astype(BF16)
    return w1e, jnp.tile(b1, n_kv)[None, :], w2e, jnp.tile(b2, n_kv)[None, :]


def _trunk(x, mods, past, P):
    B, L, D = x.shape
    depth = P['norm_g'].shape[0]
    even_states, odd_states = [], []
    for li in range(depth):
        def m(k):
            return mods[li, :, k][:, None, :]

        def g(k):
            return P['norm_g'][li, k][None, :]

        x = ffn(x, m(0), m(1), m(2), g(0), P['ffn_w_in'][li, 0], P['ffn_w_out'][li, 0])
        if li % 2 == 0:
            e = li // 2
            w = P['even_w_in'][e]
            n_sb = P['sb_width']
            HK, HV = P['gla_hk'], P['gla_hv']
            offs = [0, n_sb, 2 * n_sb, 3 * n_sb, 3 * n_sb + HK, 3 * n_sb + 2 * HK, 3 * n_sb + 2 * HK + HV,
                    3 * n_sb + 2 * HK + 2 * HV]
            ws = [w[:, offs[k]:offs[k + 1]] for k in range(7)]
            rank = w.shape[1] - offs[7]
            wa1 = jnp.pad(w[:, offs[7]:], ((0, 0), (0, LANES - rank)))
            wa2 = jnp.pad(P['gla_w_a2'][e], ((0, LANES - rank), (0, 0)))
            qa, ka, va, qb, kb, vb, rb, la = even_proj(x, m(3), m(4), g(1), ws, wa1, wa2, P['gla_b_a2'][e][None, :])
            dv = P['gla_onorm_g'].shape[1]
            n_gh = HV // dv
            dk = HK // n_gh
            if past is None:
                o_a = sb_prompt(qa, ka, va)
                s0t = jnp.zeros((B, dv, HK), F32)
            else:
                o_a = sb_sample(qa, ka, va, past['sb_k'], past['sb_v'], e, past['page_table'])
                s0t = past['gla'][e].transpose(0, 3, 1, 2).reshape(B, dv, HK)
            o_b, st = gla(qb, kb, vb, rb, la, s0t, P['gla_onorm_g'][e][None, :])
            wo = P['even_w_out'][e]
            mix_acts, mix_ws = [o_a, o_b], [wo[:n_sb], wo[n_sb:]]
            s_fin = st.reshape(B, dv, n_gh, dk).transpose(0, 2, 3, 1)
            even_states.append((ka.reshape(B, L, -1, HEAD_DIM), va.reshape(B, L, -1, HEAD_DIM), s_fin))
        else:
            o_i = li // 2
            w = P['odd_w_in'][o_i]
            WQ, WK = P['nsa_wq'], P['nsa_wk']
            n_kv = WK // HEAD_DIM
            group = WQ // WK
            offs = [0, WQ] + [WQ + (k + 1) * WK for k in range(6)]
            ws = [w[:, offs[k]:offs[k + 1]] for k in range(7)]
            wg = w[:, offs[7]:]
            half = wg.shape[1] // 2
            ws += [jnp.pad(wg[:, :half], ((0, 0), (0, LANES - half))),
                   jnp.pad(wg[:, half:], ((0, 0), (0, LANES - half)))]
            qk_g = P['nsa_qk_g'][o_i]
            consts = [_block_diag_ones(WQ), _block_diag_ones(WK), jnp.tile(qk_g[0], WQ // HEAD_DIM)[None, :],
                      jnp.tile(qk_g[2], n_kv)[None, :], jnp.tile(qk_g[3], n_kv)[None, :]]
            q, kc, vc, ks, vs, kw, vw, gates = odd_proj(x, m(3), m(4), g(1), ws, consts)
            pair = LANES // HEAD_DIM
            gk = jnp.tile(qk_g[1], pair)[None, :]
            bdp = _block_diag_ones(LANES)
            cw = [_compress_weights(P['cmp_w1'][o_i, t], P['cmp_b1'][o_i, t], P['cmp_w2'][o_i, t],
                                    P['cmp_b2'][o_i, t], pair) for t in range(2)]
            rel = P['rel_table']
            if past is None:
                n_rows = (L // CMP_STRIDE) * CMP_STRIDE
                assert n_rows == L and L % LANES == 0
                n_cmp = L // CMP_STRIDE - CMP_BLOCK // CMP_STRIDE + 1
                n_sel = -(-L // SLC_BLOCK)
                kcmp = compress(kc, n_rows, *cw[0], bdp, gk, True)
                vcmp = compress(vc, n_rows, *cw[1], bdp, gk, False)
                assert kcmp.shape[1] == LANES and n_sel <= LANES
                nq = L // LANES
                bias_dense = bias_tiles(rel, jnp.arange(nq, dtype=jnp.int32) * LANES, LANES, LANES, 1)
                bias_cmp = bias_tiles(rel, jnp.arange(nq, dtype=jnp.int32) * LANES - (CMP_BLOCK - 1),
                                      LANES, LANES, CMP_STRIDE)
                cover = _cover_01(n_cmp, n_sel, LANES, LANES).T
                bias_range = jnp.stack([jnp.max(rel), jnp.max(rel) - jnp.min(rel)]) * LOG2E
                o = nsa_prompt(q, gates, kcmp, vcmp, ks, vs, kw, vw, bias_dense, bias_cmp, cover, bias_range,
                               n_cmp, n_sel)
                w_keep = min(WINDOW, L)
                win_k, win_v = kw[:, L - w_keep:], vw[:, L - w_keep:]
            else:
                pt = past['page_table']
                past_len = pt.shape[1] * past['cmp_k'].shape[3]
                T = past_len + L
                n_rows = (T // CMP_STRIDE) * CMP_STRIDE
                assert n_rows == past_len and past['win_k'].shape[3] == WINDOW and L <= LANES
                n_cmp = n_rows // CMP_STRIDE - CMP_BLOCK // CMP_STRIDE + 1
                n_sel = -(-T // SLC_BLOCK)
                kcmp = compress_paged(past['cmp_k'], o_i, pt, *cw[0], bdp, gk, True)
                vcmp = compress_paged(past['cmp_v'], o_i, pt, *cw[1], bdp, gk, False)
                J = _round_up(n_sel, LANES)
                n_heads = WQ // HEAD_DIM
                R = n_heads * L

                def sample_bias(base, C, cstep):
                    return bias_tiles(rel, jnp.array([base], jnp.int32), L, C, cstep).reshape(R, C)

                bias_slc = sample_bias(past_len, past_len, 1)
                bias_new = sample_bias(0, LANES, 1)
                bias_cmp = sample_bias(past_len - (CMP_BLOCK - 1), kcmp.shape[1], CMP_STRIDE)
                bias_win = sample_bias(WINDOW, WINDOW, 1)
                cover = _cover_01(n_cmp, n_sel, kcmp.shape[1], J)
                pad = ((0, 0), (0, LANES - L), (0, 0))
                o = nsa_sample(q, gates, kcmp, vcmp, past['slc_k'], past['slc_v'], past['win_k'], past['win_v'],
                               o_i, pt, jnp.pad(ks, pad), jnp.pad(vs, pad), jnp.pad(kw, pad), jnp.pad(vw, pad),
                               bias_slc, bias_new, bias_cmp, bias_win, cover, n_cmp, n_sel)
                win_k = jnp.concatenate([past['win_k_rows'][o_i], kw], axis=1)[:, L:]
                win_v = jnp.concatenate([past['win_v_rows'][o_i], vw], axis=1)[:, L:]
            mix_acts, mix_ws = [o], [P['odd_w_out'][o_i]]

            def rows(t):
                return t.reshape(B, t.shape[1], n_kv, HEAD_DIM)

            odd_states.append(tuple(rows(t) for t in (kc, vc, ks, vs, win_k, win_v)))
        x = ffn(x, m(6), m(7), m(8), g(2), P['ffn_w_in'][li, 1], P['ffn_w_out'][li, 1], m(5), mix_acts, mix_ws)
    even_new = [jnp.stack([s[i] for s in even_states]) for i in range(3)]
    odd_new = [jnp.stack([s[i] for s in odd_states]) for i in range(6)]
    return x, even_new, odd_new


def kernel(x_prompt, x_sample, cache_sb_k, cache_sb_v, state_gla, cache_cmp_k, cache_cmp_v, cache_slc_k,
           cache_slc_v, cache_win_k, cache_win_v, page_table, c_prompt, c_sample, norm_g, ada_w, ada_b,
           ffn_w_in, ffn_w_out, even_w_in, gla_w_a2, gla_b_a2, gla_onorm_g, even_w_out, odd_w_in, nsa_qk_g,
           cmp_w1, cmp_b1, cmp_w2, cmp_b2, rel_table, odd_w_out):
    D = x_prompt.shape[-1]
    depth = norm_g.shape[0]
    Bp, Bs = x_prompt.shape[0], x_sample.shape[0]
    mods = ada_mod(jnp.concatenate([c_prompt, c_sample], axis=0), ada_w, ada_b).reshape(depth, Bp + Bs, 9, D)
    n_sb = cache_sb_k.shape[3] * cache_sb_k.shape[4]
    dv = gla_onorm_g.shape[1]
    HK = gla_w_a2.shape[2]
    n_gh = state_gla.shape[2]
    WK = cache_cmp_k.shape[3] * cache_cmp_k.shape[4]
    P = {
        'norm_g': norm_g, 'ffn_w_in': ffn_w_in.astype(BF16), 'ffn_w_out': ffn_w_out.astype(BF16),
        'even_w_in': even_w_in.astype(BF16), 'gla_w_a2': gla_w_a2.astype(BF16), 'gla_b_a2': gla_b_a2,
        'gla_onorm_g': gla_onorm_g, 'even_w_out': even_w_out.astype(BF16), 'odd_w_in': odd_w_in.astype(BF16),
        'nsa_qk_g': nsa_qk_g, 'cmp_w1': cmp_w1, 'cmp_b1': cmp_b1, 'cmp_w2': cmp_w2, 'cmp_b2': cmp_b2,
        'rel_table': rel_table, 'odd_w_out': odd_w_out.astype(BF16),
        'sb_width': n_sb, 'gla_hk': HK, 'gla_hv': n_gh * dv, 'nsa_wq': odd_w_out.shape[1], 'nsa_wk': WK,
    }

    def pool(c):
        return c.transpose(0, 1, 3, 4, 2).reshape(c.shape[0], c.shape[1], c.shape[3] * c.shape[4], c.shape[2])

    def rows(c):
        return c.reshape(c.shape[0], c.shape[1], c.shape[2], c.shape[3] * c.shape[4])

    past = {
        'page_table': page_table, 'sb_k': pool(cache_sb_k), 'sb_v': pool(cache_sb_v), 'gla': state_gla,
        'cmp_k': pool(cache_cmp_k), 'cmp_v': pool(cache_cmp_v), 'slc_k': pool(cache_slc_k),
        'slc_v': pool(cache_slc_v), 'win_k': pool(cache_win_k), 'win_v': pool(cache_win_v),
        'win_k_rows': rows(cache_win_k), 'win_v_rows': rows(cache_win_v),
    }
    y_p, p_even, p_odd = _trunk(x_prompt, mods[:, :Bp], None, P)
    y_s, s_even, s_odd = _trunk(x_sample, mods[:, Bp:], past, P)
    return (y_p, y_s, *p_even, *p_odd, *s_even, *s_odd)
```

```python
import functools
import math

import jax
import jax.numpy as jnp
from jax import lax
from jax.experimental import pallas as pl
from jax.experimental.pallas import tpu as pltpu

F32 = jnp.float32
BF16 = jnp.bfloat16

EPS = 1e-6
NEG = -1e30
FORCE = 1e6
HEAD_DIM = 64
GLA_TAU = 16.0
GLA_CHUNK = 64
CMP_BLOCK = 32
CMP_STRIDE = 16
SLC_BLOCK = 64
SLC_TOPN = 16
WINDOW = 512
REL_BUCKETS = 32
REL_MAX_DIST = 1024
PAGE_GROUP = 16
LANES = 128
VMEM_LIMIT = 56 * 1024 * 1024


def _cparams(sem, vmem=VMEM_LIMIT):
    return pltpu.CompilerParams(dimension_semantics=sem, vmem_limit_bytes=vmem)


def _dot(a, b):
    return jnp.dot(a, b, preferred_element_type=F32)


def _dot_nt(a, b):
    return lax.dot_general(a, b, (((1,), (1,)), ((), ())), preferred_element_type=F32)


def _dot_tn(a, b):
    return lax.dot_general(a, b, (((0,), (0,)), ((), ())), preferred_element_type=F32)


def _split2(x):
    hi = x.astype(BF16)
    lo = (x - hi.astype(F32)).astype(BF16)
    return hi, lo


def _dot_f32_by01(x, m01):
    hi, lo = _split2(x)
    return _dot(hi, m01) + _dot(lo, m01)


def _dot_01_by_f32(m01, x):
    hi, lo = _split2(x)
    return _dot(m01, hi) + _dot(m01, lo)


def _softplus(z):
    return jnp.maximum(z, 0.0) + jnp.log1p(jnp.exp(-jnp.abs(z)))


LOG2E = 1.4426950408889634
QSCALE = HEAD_DIM ** -0.5 * LOG2E
MAX_SHIFT_SPAN = 100.0


def _sb_logs2(z2):
    l2 = jnp.log(1.0 + jnp.exp2(-jnp.abs(z2))) * LOG2E
    lk = -(jnp.maximum(z2, 0.0) + l2)
    return lk, z2 + lk


def _silu(a):
    return a * jax.nn.sigmoid(a)


def _iota(shape, dim):
    return lax.broadcasted_iota(jnp.int32, shape, dim)


def _modulate(x, g, shift, scale):
    ms = jnp.mean(x * x, axis=-1, keepdims=True)
    y = x * lax.rsqrt(ms + EPS) * g
    return y * (1.0 + scale) + shift


def _head_rms(x, bd, g):
    ms = _dot_f32_by01(x * x, bd) * (1.0 / HEAD_DIM)
    return x * lax.rsqrt(ms + EPS) * g


def _const_spec(shape):
    nd = len(shape)
    return pl.BlockSpec(shape, lambda *_: (0,) * nd, pipeline_mode=pl.Buffered(1))


def _row_tiles(B, L):
    if L >= 512:
        return 1, 512
    return B, L


def _ada_kernel(c_ref, w_ref, b_ref, o_ref):
    c = c_ref[...]
    a = _silu(c).astype(BF16)
    o_ref[0] = _dot(a, w_ref[0].astype(BF16)) + b_ref[0]


def ada_mod(c, ada_w, ada_b):
    R, D = c.shape
    depth, _, N = ada_w.shape
    tn = 1024
    return pl.pallas_call(
        _ada_kernel,
        grid=(depth, N // tn),
        in_specs=[
            pl.BlockSpec((R, D), lambda l, n: (0, 0)),
            pl.BlockSpec((1, D, tn), lambda l, n: (l, 0, n)),
            pl.BlockSpec((1, 1, tn), lambda l, n: (l, 0, n)),
        ],
        out_specs=pl.BlockSpec((1, R, tn), lambda l, n: (l, 0, n)),
        out_shape=jax.ShapeDtypeStruct((depth, R, N), F32),
        compiler_params=_cparams(("parallel", "parallel")),
        name="ada_mod",
    )(c, ada_w, ada_b.reshape(depth, 1, N))


def _ffn_kernel(x_ref, sh_ref, sc_ref, gt_ref, g_ref, win_ref, wout_ref, *rest, d_ff, tf, n_mix):
    o_ref = rest[-1]
    x = x_ref[...]
    tb, tl, D = x.shape
    if n_mix:
        mix = jnp.zeros((tb * tl, D), F32)
        for a, w in zip(rest[1:1 + n_mix], rest[1 + n_mix:1 + 2 * n_mix]):
            av = a[...]
            mix = mix + _dot(av.reshape(tb * tl, av.shape[-1]).astype(BF16), w[...])
        x = x + rest[0][...] * mix.reshape(tb, tl, D)
    h = _modulate(x, g_ref[...], sh_ref[...], sc_ref[...])
    hb = h.reshape(tb * tl, D).astype(BF16)
    acc = jnp.zeros((tb * tl, D), F32)
    for j in range(d_ff // tf):
        a = _dot(hb, win_ref[:, j * tf:(j + 1) * tf])
        b = _dot(hb, win_ref[:, d_ff + j * tf:d_ff + (j + 1) * tf])
        u = (_silu(a) * b).astype(BF16)
        acc = acc + _dot(u, wout_ref[j * tf:(j + 1) * tf, :])
    o_ref[...] = x + 0.5 * gt_ref[...] * acc.reshape(tb, tl, D)


def ffn(x, shift, scale, gate, g, w_in, w_out, mix_gate=None, mix_acts=(), mix_ws=()):
    B, L, D = x.shape
    d_ff = w_out.shape[0]
    tb, tl = _row_tiles(B, L)
    xs = pl.BlockSpec((tb, tl, D), lambda b, l: (b, l, 0))
    ms = pl.BlockSpec((tb, 1, D), lambda b, l: (b, 0, 0))
    n_mix = len(mix_acts)
    mix_specs = []
    if n_mix:
        mix_specs = [ms] + [pl.BlockSpec((tb, tl, a.shape[-1]), lambda b, l: (b, l, 0)) for a in mix_acts] \
            + [_const_spec(w.shape) for w in mix_ws]
    return pl.pallas_call(
        functools.partial(_ffn_kernel, d_ff=d_ff, tf=256, n_mix=n_mix),
        grid=(B // tb, L // tl),
        in_specs=[xs, ms, ms, ms, _const_spec((1, D)), _const_spec(w_in.shape), _const_spec(w_out.shape)]
        + mix_specs,
        out_specs=xs,
        out_shape=jax.ShapeDtypeStruct(x.shape, F32),
        compiler_params=_cparams(("parallel", "parallel")),
        name="ffn",
    )(x, shift, scale, gate, g, w_in, w_out, *(([mix_gate] if n_mix else []) + list(mix_acts) + list(mix_ws)))


def _even_proj_kernel(x_ref, sh_ref, sc_ref, g_ref, wqa, wka, wva, wqb, wkb, wvb, wrb, wa1, wa2, ba2,
                      qa, ka, va, qb, kb, vb, rb, la):
    x = x_ref[...]
    tb, tl, D = x.shape
    h = _modulate(x, g_ref[...], sh_ref[...], sc_ref[...])
    hb = h.reshape(tb * tl, D).astype(BF16)
    for w, o in ((wqa, qa), (wka, ka), (wva, va), (wqb, qb), (wkb, kb), (wvb, vb), (wrb, rb)):
        o[...] = _dot(hb, w[...]).reshape(o.shape)
    a1 = _dot(hb, wa1[...]).astype(BF16)
    z = _dot(a1, wa2[...]) + ba2[...]
    la[...] = (-_softplus(-z) * (1.0 / GLA_TAU)).reshape(la.shape)


def even_proj(x, shift, scale, g, ws, wa1, wa2, ba2):
    B, L, D = x.shape
    tb, tl = _row_tiles(B, L)
    xs = pl.BlockSpec((tb, tl, D), lambda b, l: (b, l, 0))
    ms = pl.BlockSpec((tb, 1, D), lambda b, l: (b, 0, 0))
    widths = [w.shape[1] for w in ws] + [wa2.shape[1]]
    return pl.pallas_call(
        _even_proj_kernel,
        grid=(B // tb, L // tl),
        in_specs=[xs, ms, ms, _const_spec((1, D))] + [_const_spec(w.shape) for w in ws]
        + [_const_spec(wa1.shape), _const_spec(wa2.shape), _const_spec(ba2.shape)],
        out_specs=[pl.BlockSpec((tb, tl, n), lambda b, l: (b, l, 0)) for n in widths],
        out_shape=[jax.ShapeDtypeStruct((B, L, n), F32) for n in widths],
        compiler_params=_cparams(("parallel", "parallel")),
        name="even_proj",
    )(x, shift, scale, g, *ws, wa1, wa2, ba2)


def _odd_proj_kernel(x_ref, sh_ref, sc_ref, g_ref, wq, wkc, wvc, wks, wvs, wkw, wvw, wg0, wg1,
                     hq, hk, gq, gks, gkw,
                     q, kc, vc, ks, vs, kw, vw, gates):
    x = x_ref[...]
    tb, tl, D = x.shape
    h = _modulate(x, g_ref[...], sh_ref[...], sc_ref[...])
    hb = h.reshape(tb * tl, D).astype(BF16)
    q[...] = _head_rms(_dot(hb, wq[...]), hq[...], gq[...]).reshape(q.shape)
    kc[...] = _dot(hb, wkc[...]).reshape(kc.shape)
    vc[...] = _dot(hb, wvc[...]).reshape(vc.shape)
    ks[...] = _head_rms(_dot(hb, wks[...]), hk[...], gks[...]).reshape(ks.shape)
    vs[...] = _dot(hb, wvs[...]).reshape(vs.shape)
    kw[...] = _head_rms(_dot(hb, wkw[...]), hk[...], gkw[...]).reshape(kw.shape)
    vw[...] = _dot(hb, wvw[...]).reshape(vw.shape)
    gates[0] = jax.nn.sigmoid(_dot(hb, wg0[...])).reshape(gates.shape[1:])
    gates[1] = jax.nn.sigmoid(_dot(hb, wg1[...])).reshape(gates.shape[1:])


def odd_proj(x, shift, scale, g, ws, consts):
    B, L, D = x.shape
    tb, tl = _row_tiles(B, L)
    xs = pl.BlockSpec((tb, tl, D), lambda b, l: (b, l, 0))
    ms = pl.BlockSpec((tb, 1, D), lambda b, l: (b, 0, 0))
    widths = [w.shape[1] for w in ws[:7]]
    out_specs = [pl.BlockSpec((tb, tl, n), lambda b, l: (b, l, 0)) for n in widths]
    out_specs.append(pl.BlockSpec((2, tb, tl, LANES), lambda b, l: (0, b, l, 0)))
    out_shape = [jax.ShapeDtypeStruct((B, L, n), F32) for n in widths]
    out_shape.append(jax.ShapeDtypeStruct((2, B, L, LANES), F32))
    return pl.pallas_call(
        _odd_proj_kernel,
        grid=(B // tb, L // tl),
        in_specs=[xs, ms, ms, _const_spec((1, D))] + [_const_spec(w.shape) for w in ws]
        + [_const_spec(c.shape) for c in consts],
        out_specs=out_specs,
        out_shape=out_shape,
        compiler_params=_cparams(("parallel", "parallel")),
        name="odd_proj",
    )(x, shift, scale, g, *ws, *consts)


def _log2(n):
    assert n > 0 and n & (n - 1) == 0, n
    return n.bit_length() - 1


def _later_keys_01(tk):
    return jnp.where(_iota((tk, tk), 0) > _iota((tk, tk), 1), 1.0, 0.0).astype(BF16)


def _sb_tile(q2, kt, vt, run, acc, u01, before, feature_major=False):
    lk, ls = _sb_logs2(_dot(q2, kt) if feature_major else _dot_nt(q2, kt))
    if before is not None:
        lk = jnp.where(before, lk, 0.0)
    cs = _dot_f32_by01(lk, u01)
    w = jnp.exp2(ls + cs + run)
    if before is not None:
        w = jnp.where(before, w, 0.0)
    wb = w.astype(BF16)
    acc = acc + (_dot_nt(wb, vt) if feature_major else _dot(wb, vt))
    run = run + cs[:, 0:1] + lk[:, 0:1]
    return run, acc


def _sb_prompt_kernel(q_ref, k_ref, v_ref, o_ref, *, tq):
    i = pl.program_id(2)
    n_pairs = q_ref.shape[2] // LANES
    lane = _iota((tq, LANES), 1)
    u01 = _later_keys_01(tq)
    before = _iota((2 * tq, tq), 1) < (_iota((2 * tq, tq), 0) & (tq - 1))
    q2 = []
    for p in range(n_pairs):
        q = q_ref[0, :, p * LANES:(p + 1) * LANES] * QSCALE
        q2.append(jnp.concatenate([jnp.where(lane < HEAD_DIM, q, 0.0), jnp.where(lane >= HEAD_DIM, q, 0.0)],
                                  axis=0).astype(BF16))

    def tile(j, carry, mask):
        start = pl.multiple_of(j * tq, tq)
        out = []
        for p in range(n_pairs):
            kt = k_ref[0, pl.ds(start, tq), p * LANES:(p + 1) * LANES].astype(BF16)
            vt = v_ref[0, pl.ds(start, tq), p * LANES:(p + 1) * LANES].astype(BF16)
            out.append(_sb_tile(q2[p], kt, vt, *carry[p], u01, mask))
        return tuple(out)

    init = tuple((jnp.zeros((2 * tq, 1), F32), jnp.zeros((2 * tq, LANES), F32)) for _ in range(n_pairs))
    carry = lax.fori_loop(i, i + 1, lambda j, c: tile(j, c, before), init)
    carry = lax.fori_loop(1, i + 1, lambda jj, c: tile(i - jj, c, None), carry)
    for p in range(n_pairs):
        acc = carry[p][1]
        o_ref[0, :, p * LANES:(p + 1) * LANES] = jnp.where(lane < HEAD_DIM, acc[:tq], acc[tq:])


def sb_prompt(q, k, v):
    B, L, W = q.shape
    tq = 256
    wb = 4 * LANES
    qs = pl.BlockSpec((1, tq, wb), lambda b, p, i: (b, i, p))
    ks = pl.BlockSpec((1, L, wb), lambda b, p, i: (b, 0, p))
    return pl.pallas_call(
        functools.partial(_sb_prompt_kernel, tq=tq),
        grid=(B, W // wb, L // tq),
        in_specs=[qs, ks, ks],
        out_specs=qs,
        out_shape=jax.ShapeDtypeStruct(q.shape, F32),
        compiler_params=_cparams(("parallel", "parallel", "arbitrary")),
        name="sb_prompt",
    )(q, k, v)


def _sb_sample_kernel(pt_ref, q_ref, kn_ref, vn_ref, *rest, n_tok, group):
    k_refs, v_refs = rest[:group], rest[group:2 * group]
    o_ref, run_ref, acc_ref = rest[2 * group:]
    s = pl.program_id(1)
    q = q_ref[0] * QSCALE
    W = q.shape[1]
    n_heads = W // HEAD_DIM
    R = n_heads * n_tok
    tk = kn_ref.shape[1]
    own = (_iota((R, W), 0) >> _log2(n_tok)) == (_iota((R, W), 1) >> _log2(HEAD_DIM))
    q2 = jnp.where(own, jnp.concatenate([q] * n_heads, axis=0), 0.0).astype(BF16)
    u01 = _later_keys_01(tk)

    @pl.when(s == 0)
    def _():
        before = _iota((R, tk), 1) < (_iota((R, tk), 0) & (n_tok - 1))
        run, acc = _sb_tile(q2, kn_ref[0].astype(BF16), vn_ref[0].astype(BF16),
                            jnp.zeros((R, 1), F32), jnp.zeros((R, W), F32), u01, before)
        run_ref[...] = run
        acc_ref[...] = acc

    @pl.when(s > 0)
    def _():
        kt = jnp.concatenate([r[0, 0] for r in k_refs], axis=1).astype(BF16)
        vt = jnp.concatenate([r[0, 0] for r in v_refs], axis=1).astype(BF16)

        def stack(x):
            return jnp.concatenate([x[:, kk * tk:(kk + 1) * tk] for kk in range(group)], axis=0)

        lk, ls = _sb_logs2(_dot(q2, kt))
        lk = stack(lk)
        cs = _dot_f32_by01(lk, u01)
        tot = cs[:, 0:1] + lk[:, 0:1]
        run = run_ref[...]
        runs = []
        for kk in range(group):
            runs.append(run)
            run = run + tot[kk * R:(kk + 1) * R]
        w = jnp.exp2(stack(ls) + cs + jnp.concatenate(runs, axis=0))
        w = jnp.concatenate([w[kk * R:(kk + 1) * R] for kk in range(group)], axis=1).astype(BF16)
        acc_ref[...] = acc_ref[...] + _dot_nt(w, vt)
        run_ref[...] = run

    @pl.when(s == pl.num_programs(1) - 1)
    def _():
        acc = jnp.where(own, acc_ref[...], 0.0)
        out = acc[0:n_tok]
        for h in range(1, n_heads):
            out = out + acc[h * n_tok:(h + 1) * n_tok]
        o_ref[0] = out


def sb_sample(q, k_new, v_new, pool_k, pool_v, layer, page_table):
    B, n_tok, W = q.shape
    page = pool_k.shape[3]
    n_pages = page_table.shape[1]
    group = PAGE_GROUP
    n_steps = 1 + n_pages // group
    pad = ((0, 0), (0, page - n_tok), (0, 0))
    kn, vn = jnp.pad(k_new, pad), jnp.pad(v_new, pad)

    def page_spec(kk):
        def imap(b, s, pt):
            return (layer, pt[b, n_pages - 1 - (jnp.maximum(s, 1) - 1) * group - kk], 0, 0)
        return pl.BlockSpec((1, 1, W, page), imap)

    qs = pl.BlockSpec((1, n_tok, W), lambda b, s, pt: (b, 0, 0))
    ns = pl.BlockSpec((1, page, W), lambda b, s, pt: (b, 0, 0))
    R = (W // HEAD_DIM) * n_tok
    grid_spec = pltpu.PrefetchScalarGridSpec(
        num_scalar_prefetch=1,
        grid=(B, n_steps),
        in_specs=[qs, ns, ns] + [page_spec(kk) for kk in range(group)] * 2,
        out_specs=qs,
        scratch_shapes=[pltpu.VMEM((R, 1), F32), pltpu.VMEM((R, W), F32)],
    )
    return pl.pallas_call(
        functools.partial(_sb_sample_kernel, n_tok=n_tok, group=group),
        grid_spec=grid_spec,
        out_shape=jax.ShapeDtypeStruct(q.shape, F32),
        compiler_params=_cparams(("parallel", "arbitrary")),
        name="sb_sample",
    )(page_table, q, kn, vn, *([pool_k] * group), *([pool_v] * group))


def _gla_kernel(q_ref, k_ref, v_ref, r_ref, la_ref, s0_ref, g_ref, o_ref, sfin_ref, st, *, dk, dv):
    c = pl.program_id(1)

    @pl.when(c == 0)
    def _():
        st[...] = s0_ref[...]

    for ib in range(q_ref.shape[0]):
        _gla_chunk(q_ref.at[ib], k_ref.at[ib], v_ref.at[ib], r_ref.at[ib], la_ref.at[ib], g_ref, o_ref.at[ib],
                   st.at[ib], dk=dk, dv=dv)

    @pl.when(c == pl.num_programs(1) - 1)
    def _():
        sfin_ref[...] = st[...]


def _gla_chunk(q_ref, k_ref, v_ref, r_ref, la_ref, g_ref, o_ref, st, *, dk, dv):
    q = q_ref[...] * (dk ** -0.5)
    k = k_ref[...]
    la = la_ref[...]
    C, HK = q.shape
    n_heads = HK // dk
    causal = _iota((C, C), 0) >= _iota((C, C), 1)
    ltri = jnp.where(causal, 1.0, 0.0).astype(BF16)
    hi = la.astype(BF16)
    r1 = la - hi.astype(F32)
    mid = r1.astype(BF16)
    lo = (r1 - mid.astype(F32)).astype(BF16)
    b = _dot(ltri, hi) + _dot(ltri, mid) + _dot(ltri, lo)
    b_last = b[C - 1:C, :]
    q_in = q * jnp.exp(b)
    k_in = (k * jnp.exp(-b)).astype(BF16)
    k_dec = k * jnp.exp(b_last - b)
    lane_h = _iota((C, HK), 1) >> _log2(dk)
    s_old = st[...]
    s_old_b = s_old.astype(BF16)
    upd = jnp.zeros(s_old.shape, F32)
    for h in range(n_heads):
        own = lane_h == h
        qh = jnp.where(own, q_in, 0.0).astype(BF16)
        att = jnp.where(causal, _dot_nt(qh, k_in), 0.0)
        vh = v_ref[:, h * dv:(h + 1) * dv].astype(BF16)
        oh = _dot(att.astype(BF16), vh) + _dot_nt(qh, s_old_b)
        y = oh * lax.rsqrt(jnp.mean(oh * oh, axis=-1, keepdims=True) + EPS) * g_ref[...]
        o_ref[:, h * dv:(h + 1) * dv] = y * _silu(r_ref[:, h * dv:(h + 1) * dv])
        kd = jnp.where(own, k_dec, 0.0).astype(BF16)
        upd = upd + _dot_tn(vh, kd)
    st[...] = s_old * jnp.exp(b_last) + upd


def gla(q, k, v, r, la, s0t, g):
    B, L, HK = q.shape
    HV = v.shape[2]
    dv = g.shape[1]
    dk = HK // (HV // dv)
    C = min(GLA_CHUNK, L)
    bb = next(n for n in (4, 2, 1) if B % n == 0)
    ks = pl.BlockSpec((bb, C, HK), lambda b, c: (b, c, 0))
    vs = pl.BlockSpec((bb, C, HV), lambda b, c: (b, c, 0))
    ss = pl.BlockSpec((bb, dv, HK), lambda b, c: (b, 0, 0))
    return pl.pallas_call(
        functools.partial(_gla_kernel, dk=dk, dv=dv),
        grid=(B // bb, L // C),
        in_specs=[ks, ks, vs, vs, ks, ss, pl.BlockSpec((1, dv), lambda b, c: (0, 0))],
        out_specs=[vs, ss],
        out_shape=[jax.ShapeDtypeStruct(v.shape, F32), jax.ShapeDtypeStruct(s0t.shape, F32)],
        scratch_shapes=[pltpu.VMEM((bb, dv, HK), F32)],
        compiler_params=_cparams(("parallel", "arbitrary")),
        name="gla",
    )(q, k, v, r, la, s0t, g)


def _gelu_tanh(x):
    return x * (0.5 * (1.0 + jnp.tanh(math.sqrt(2.0 / math.pi) * (x + 0.044715 * (x * x * x)))))


def _compress_mlp(seg_rows, w1_ref, b1_ref, w2_ref, b2_ref, bd_ref, g_ref, o_ref, *, n_pairs, nseg, norm):
    rk = w1_ref.shape[2] // LANES
    wh = w1_ref.shape[3]
    R = n_pairs * nseg
    p0 = jnp.zeros((R, wh), F32)
    p1 = jnp.zeros((R, wh), F32)
    for rr in range(CMP_STRIDE // rk):
        rows = jnp.concatenate(
            [jnp.concatenate([seg_rows(p, rr * rk + u) for u in range(rk)], axis=1) for p in range(n_pairs)],
            axis=0).astype(BF16)
        p0 = p0 + _dot(rows, w1_ref[0, rr])
        p1 = p1 + _dot(rows, w1_ref[1, rr])
    hid = b1_ref[...] + p0 + pltpu.roll(p1, R - 1, 0)
    y = _dot(_gelu_tanh(hid).astype(BF16), w2_ref[...]) + b2_ref[...]
    if norm:
        y = _head_rms(y, bd_ref[...], g_ref[...])
    y = jnp.where((_iota(y.shape, 0) & (nseg - 1)) < nseg - 1, y, 0.0)
    for p in range(n_pairs):
        o_ref[0, :, p * LANES:(p + 1) * LANES] = y[p * nseg:(p + 1) * nseg]


def _compress_kernel(*refs, n_pairs, nseg, norm):
    x_refs = refs[:n_pairs]
    _compress_mlp(lambda p, r: x_refs[p][0, pl.ds(r, nseg, stride=CMP_STRIDE), :], *refs[n_pairs:],
                  n_pairs=n_pairs, nseg=nseg, norm=norm)


def compress(x, n_rows, w1e, b1e, w2e, b2e, bd, g, norm):
    B, _, W = x.shape
    nseg = n_rows // CMP_STRIDE
    _log2(nseg)
    n_pairs = W // LANES
    return pl.pallas_call(
        functools.partial(_compress_kernel, n_pairs=n_pairs, nseg=nseg, norm=norm),
        grid=(B,),
        in_specs=[pl.BlockSpec((1, n_rows, LANES), functools.partial(lambda b, p: (b, 0, p), p=p))
                  for p in range(n_pairs)]
        + [_const_spec(a.shape) for a in (w1e, b1e, w2e, b2e, bd, g)],
        out_specs=pl.BlockSpec((1, nseg, W), lambda b: (b, 0, 0)),
        out_shape=jax.ShapeDtypeStruct((B, nseg, W), F32),
        compiler_params=_cparams(("parallel",)),
        name="compress",
    )(*([x] * n_pairs), w1e, b1e, w2e, b2e, bd, g)


def _compress_paged_kernel(pt_ref, *refs, group, n_pairs, nseg, norm):
    page_refs = refs[:group]
    xs = refs[-1]
    s = pl.program_id(1)
    page = page_refs[0].shape[3]
    for kk, pr in enumerate(page_refs):
        start = pl.multiple_of((s * group + kk) * page, page)
        for p in range(n_pairs):
            xs[p, pl.ds(start, page), :] = pr[0, 0, p * LANES:(p + 1) * LANES, :].T

    @pl.when(s == pl.num_programs(1) - 1)
    def _():
        _compress_mlp(lambda p, r: xs[p, pl.ds(r, nseg, stride=CMP_STRIDE), :], *refs[group:-1],
                      n_pairs=n_pairs, nseg=nseg, norm=norm)


def compress_paged(pool, layer, page_table, w1e, b1e, w2e, b2e, bd, g, norm):
    _, _, W, page = pool.shape
    B, n_pages = page_table.shape
    group = PAGE_GROUP
    n_rows = n_pages * page
    nseg = n_rows // CMP_STRIDE
    _log2(nseg)
    n_pairs = W // LANES

    def page_spec(kk):
        return pl.BlockSpec((1, 1, W, page), lambda b, s, pt: (layer, pt[b, s * group + kk], 0, 0))

    consts = (w1e, b1e, w2e, b2e, bd, g)
    grid_spec = pltpu.PrefetchScalarGridSpec(
        num_scalar_prefetch=1,
        grid=(B, n_pages // group),
        in_specs=[page_spec(kk) for kk in range(group)]
        + [pl.BlockSpec(a.shape, functools.partial(lambda b, s, pt, nd: (0,) * nd, nd=a.ndim),
                        pipeline_mode=pl.Buffered(1)) for a in consts],
        out_specs=pl.BlockSpec((1, nseg, W), lambda b, s, pt: (b, 0, 0)),
        scratch_shapes=[pltpu.VMEM((n_pairs, n_rows, LANES), F32)],
    )
    return pl.pallas_call(
        functools.partial(_compress_paged_kernel, group=group, n_pairs=n_pairs, nseg=nseg, norm=norm),
        grid_spec=grid_spec,
        out_shape=jax.ShapeDtypeStruct((B, nseg, W), F32),
        compiler_params=_cparams(("parallel", "arbitrary")),
        name="compress_paged",
    )(page_table, *([pool] * group), *consts)


_REL_EXACT = REL_BUCKETS // 2
_REL_THRESHOLDS = tuple(
    math.ceil(_REL_EXACT * (REL_MAX_DIST / _REL_EXACT) ** (k / (REL_BUCKETS - _REL_EXACT)) - 1e-9)
    for k in range(1, REL_BUCKETS - _REL_EXACT))


def _bias_kernel(base_ref, tbl_ref, o_ref, *, cstep):
    t = pl.program_id(0)
    _, H, R, C = o_ref.shape
    dist = base_ref[t] + _iota((R, C), 0) - cstep * _iota((R, C), 1)
    d = jnp.maximum(dist, 0)
    big = jnp.full((R, C), _REL_EXACT, jnp.int32)
    for th in _REL_THRESHOLDS:
        big = big + jnp.where(d >= th, 1, 0)
    bucket = jnp.where(d < _REL_EXACT, d, big)
    for h in range(H):
        acc = jnp.full((R, C), tbl_ref[h, REL_BUCKETS - 1], F32)
        for u in range(REL_BUCKETS - 1):
            acc = jnp.where(bucket == u, tbl_ref[h, u], acc)
        o_ref[0, h] = acc * LOG2E


def bias_tiles(rel_table, bases, R, C, cstep):
    H = rel_table.shape[1]
    T = bases.shape[0]
    grid_spec = pltpu.PrefetchScalarGridSpec(
        num_scalar_prefetch=1,
        grid=(T,),
        in_specs=[pl.BlockSpec(memory_space=pltpu.SMEM)],
        out_specs=pl.BlockSpec((1, H, R, C), lambda t, base: (t, 0, 0, 0)),
    )
    return pl.pallas_call(
        functools.partial(_bias_kernel, cstep=cstep),
        grid_spec=grid_spec,
        out_shape=jax.ShapeDtypeStruct((T, H, R, C), F32),
        compiler_params=_cparams(("arbitrary",)),
        name="bias_tiles",
    )(bases, rel_table.T)


def _masked_softmax(s, mask):
    s = jnp.where(mask, s, NEG)
    m = jnp.max(s, axis=-1, keepdims=True)
    p = jnp.where(mask, jnp.exp2(s - m), 0.0)
    return p * (1.0 / jnp.maximum(jnp.sum(p, axis=-1, keepdims=True), 1e-30))


def _off_mask(keep):
    return jnp.where(keep, 0.0, NEG)


def _softmax_update_add(s, m, l, acc, v, feature_major=False):
    m_new = jnp.maximum(m, jnp.max(s, axis=-1, keepdims=True))
    alpha = jnp.exp2(m - m_new)
    p = jnp.exp2(s - m_new)
    l = alpha * l + jnp.sum(p, axis=-1, keepdims=True)
    pb = p.astype(BF16)
    acc = alpha * acc + (_dot_nt(pb, v) if feature_major else _dot(pb, v))
    return m_new, l, acc


def _softmax_step(s, m, acc, v1):
    m_new = jnp.maximum(m, jnp.max(s, axis=-1, keepdims=True))
    p = jnp.exp2(s - m_new)
    return m_new, jnp.exp2(m - m_new) * acc + _dot(p.astype(BF16), v1)


def _softmax_init(R, W):
    return jnp.full((R, 1), NEG, F32), jnp.zeros((R, 1), F32), jnp.zeros((R, W), F32)


def _softmax_finish(l, acc):
    return acc * (1.0 / jnp.maximum(l, 1e-30))


def _top_blocks(score, n_real, n_top, axis=1):
    jidx = _iota(score.shape, axis)
    rank = jnp.zeros(score.shape, F32)
    for i in range(n_real):
        si = score[:, i:i + 1] if axis == 1 else score[i:i + 1, :]
        beats = (si > score) | ((si == score) & (jidx > i))
        rank = rank + jnp.where(beats, 1.0, 0.0)
    return jnp.where(rank < n_top, 1.0, 0.0)


def _block_scores(imp, q_pos, n_sel, axis=1):
    j = _iota(imp.shape, axis)
    cur = q_pos >> _log2(SLC_BLOCK)
    valid = j * SLC_BLOCK <= q_pos
    forced = valid & ((j == 0) | (j == cur) | (j == cur - 1))
    score = jnp.where(forced, FORCE, jnp.where(valid, imp, -FORCE))
    return jnp.where(j < n_sel, score, -3e38)


def _block_of_key_01(n_blocks, tk, pos0):
    j = _iota((n_blocks, tk), 0)
    c = _iota((n_blocks, tk), 1)
    return jnp.where(((pos0 + c) >> _log2(SLC_BLOCK)) == j, 1.0, 0.0).astype(BF16)


def _half_to(x, src_high, dst_high):
    return x if src_high == dst_high else pltpu.roll(x, HEAD_DIM, 1)


def _nsa_prompt_kernel(q_ref, g_ref, kc_ref, vc_ref, ks_ref, vs_ref, kw_ref, vw_ref, bd_ref, bc_ref, covt_ref,
                       brange_ref, o_ref, woff_sc, knorm_sc, *, tq, n_cmp, n_sel, n_top, group):
    i = pl.program_id(2)
    G = group
    R = G * tq
    nt = tq // LANES
    JP = _round_up(n_sel, 8)
    n_win = WINDOW // tq
    low = _iota((tq, LANES), 1) < HEAD_DIM
    SHIFT_LANE = HEAD_DIM - 1

    @pl.when(i == 0)
    def _():
        dist0 = _iota((tq, tq), 0) - _iota((tq, tq), 1)
        for dd in range(n_win + 1):
            woff_sc[dd] = _off_mask((dist0 + dd * tq >= 0) & (dist0 + dd * tq < WINDOW))
        for a, ref in enumerate((ks_ref, kw_ref, kc_ref)):
            k2 = ref[0] * ref[0]
            low_k = _iota(k2.shape, 1) < HEAD_DIM
            for hh in range(2):
                n2 = jnp.sum(jnp.where(low_k if hh == 0 else jnp.logical_not(low_k), k2, 0.0), axis=-1, keepdims=True)
                knorm_sc[2 * a + hh] = jnp.broadcast_to(jnp.sqrt(jnp.max(n2, axis=0, keepdims=True)), (8, LANES))

    def tile_off(dd):
        return jnp.concatenate([woff_sc[dd]] * G, axis=0)

    def prepare(hh):
        own = low if hh == 0 else jnp.logical_not(low)
        pieces = []
        for g in range(G):
            x = q_ref[0, :, (hh * 2 + g // 2) * LANES:(hh * 2 + g // 2 + 1) * LANES] * QSCALE
            pieces.append(_half_to(x, g % 2 == 1, hh == 1))
        qs = jnp.concatenate([jnp.where(own, x, 0.0) for x in pieces], axis=0).astype(BF16)

        is_shift = (_iota((tq, LANES), 1) - (HEAD_DIM if hh == 0 else 0)) == SHIFT_LANE
        qn = [jnp.sqrt(jnp.sum(jnp.where(own, x * x, 0.0), axis=-1, keepdims=True)) for x in pieces]
        kn = jnp.maximum(jnp.maximum(knorm_sc[hh][0:1, 0:1], knorm_sc[2 + hh][0:1, 0:1]),
                         knorm_sc[4 + hh][0:1, 0:1])
        bmax, bspan = brange_ref[0], brange_ref[1]
        fixed_shift = 2.0 * jnp.max(jnp.concatenate(qn, axis=0) * kn) + bspan <= MAX_SHIFT_SPAN
        shifted = [jnp.where(own, x, jnp.where(is_shift, -(n * kn + bmax), 0.0)) for x, n in zip(pieces, qn)]
        q_sh = jnp.concatenate(shifted, axis=0).astype(BF16)

        lane_c = _iota((LANES, LANES), 1)
        own_c = (lane_c < HEAD_DIM) if hh == 0 else (lane_c >= HEAD_DIM)
        shift_c = (lane_c - (HEAD_DIM if hh == 0 else 0)) == SHIFT_LANE
        bias_c = jnp.concatenate([bc_ref[a, hh * G + g] for g in range(G) for a in range(nt)], axis=0)
        kc = jnp.where(own_c, kc_ref[0], jnp.where(shift_c, 1.0, 0.0)).astype(BF16)
        s = _dot_nt(q_sh, kc) + bias_c
        rowc = _iota(s.shape, 0) & (tq - 1)
        colc = _iota(s.shape, 1)
        dist_c = i * tq + rowc - (colc * CMP_STRIDE + (CMP_BLOCK - 1))
        valid_c = (dist_c >= 0) & (colc < n_cmp)
        s = jnp.where(valid_c, s, NEG)
        s = lax.cond(fixed_shift, lambda s: s, lambda s: s - jnp.max(s, axis=-1, keepdims=True), s)
        p_c = jnp.where(valid_c, jnp.exp2(s), 0.0).astype(BF16)
        o_c = _dot(p_c, jnp.where(own_c, vc_ref[0], 1.0).astype(BF16))
        covt = covt_ref[0:JP, :]
        ones = jnp.ones((8, LANES), BF16)
        impt = jnp.zeros((JP, tq), F32)
        for g in range(G):
            p_g = p_c[g * tq:(g + 1) * tq]
            impt = impt + _dot_nt(covt, p_g) * (1.0 / jnp.maximum(_dot_nt(ones, p_g)[0:1], 1e-30))
        q_pos = i * tq + _iota(impt.shape, 1)
        selt = _top_blocks(_block_scores(impt, q_pos, n_sel, 0), n_sel, n_top, 0)
        selt = jnp.concatenate([selt, jnp.zeros((LANES - JP, tq), F32)], axis=0)
        sel = selt.T

        sel_off = _half_to(_off_mask(sel > 0.5), False, hh == 0)
        qsel = jnp.concatenate([jnp.where(own, x, sel_off) for x in pieces], axis=0).astype(BF16)
        q_slc = jnp.concatenate([jnp.where(own | is_shift, x, sel_off) for x in shifted], axis=0).astype(BF16)
        return dict(qs=qs, qsel=qsel, q_slc=q_slc, q_win=q_sh, o_c=o_c, fixed_shift=fixed_shift)

    def tiles(hh, P):
        qs, qsel = P['qs'], P['qsel']
        own = low if hh == 0 else jnp.logical_not(low)
        spare_block = _iota((tq, LANES), 1) - (HEAD_DIM if hh == 0 else 0)
        is_shift = spare_block == SHIFT_LANE
        key_row = _iota((tq, LANES), 0)

        def values(ref, start):
            return jnp.where(own, ref[0, pl.ds(start, tq), :], 1.0).astype(BF16)

        def dense_bias(d):
            blocks = []
            for g in range(G):
                for a in range(nt):
                    blocks.append(jnp.concatenate(
                        [bd_ref[jnp.maximum(nt * d + a - b, 0), hh * G + g] for b in range(nt)], axis=1))
            return jnp.concatenate(blocks, axis=0)

        def slc_tile(j, carry, diag):
            start = pl.multiple_of(j * tq, tq)
            in_block = ((start + key_row) >> _log2(SLC_BLOCK)) == spare_block
            kt = jnp.where(own, ks_ref[0, pl.ds(start, tq), :], jnp.where(in_block, 1.0, 0.0)).astype(BF16)
            s = _dot_nt(qsel, kt) + dense_bias(i - j)
            if diag:
                s = s + tile_off(0)
            return _softmax_step(s, *carry, values(vs_ref, start))

        def win_tile(dd, carry):
            start = pl.multiple_of((i - dd) * tq, tq)
            kt = kw_ref[0, pl.ds(start, tq), :].astype(BF16)
            s = _dot_nt(qs, kt) + dense_bias(dd) + tile_off(dd)
            return _softmax_step(s, *carry, values(vw_ref, start))

        def running_max():
            init = (jnp.full((R, 1), NEG, F32), jnp.zeros((R, LANES), F32))
            carry = lax.fori_loop(0, i, lambda j, c: slc_tile(j, c, False), init)
            acc_s = lax.fori_loop(i, i + 1, lambda j, c: slc_tile(j, c, True), carry)[1]
            return acc_s, lax.fori_loop(0, jnp.minimum(i, n_win) + 1, win_tile, init)[1]

        def slc_shifted(j, acc, diag):
            start = pl.multiple_of(j * tq, tq)
            in_block = ((start + key_row) >> _log2(SLC_BLOCK)) == spare_block
            kt = jnp.where(own, ks_ref[0, pl.ds(start, tq), :],
                           jnp.where(in_block | is_shift, 1.0, 0.0)).astype(BF16)
            s = _dot_nt(P['q_slc'], kt) + dense_bias(i - j)
            if diag:
                s = s + tile_off(0)
            return acc + _dot(jnp.exp2(s).astype(BF16), values(vs_ref, start))

        def win_shifted(dd, acc):
            start = pl.multiple_of((i - dd) * tq, tq)
            kt = jnp.where(own, kw_ref[0, pl.ds(start, tq), :], jnp.where(is_shift, 1.0, 0.0)).astype(BF16)
            s = _dot_nt(P['q_win'], kt) + dense_bias(dd) + tile_off(dd)
            return acc + _dot(jnp.exp2(s).astype(BF16), values(vw_ref, start))

        return running_max, slc_shifted, win_shifted

    def attend(prepared):
        (run0, slc0, win0), (run1, slc1, win1) = [tiles(hh, prepared[hh]) for hh in range(2)]

        def fixed_shift():
            zero = jnp.zeros((R, LANES), F32)
            acc = lax.fori_loop(0, i, lambda j, a: (slc0(j, a[0], False), slc1(j, a[1], False)), (zero, zero))
            acc = lax.fori_loop(i, i + 1, lambda j, a: (slc0(j, a[0], True), slc1(j, a[1], True)), acc)
            win = lax.fori_loop(0, jnp.minimum(i, n_win) + 1, lambda dd, a: (win0(dd, a[0]), win1(dd, a[1])),
                                (zero, zero))
            return (acc[0], win[0]), (acc[1], win[1])

        both = prepared[0]['fixed_shift'] & prepared[1]['fixed_shift']
        return lax.cond(both, fixed_shift, lambda: (run0(), run1()))

    def finish(hh, o_c, acc_s, acc_w):
        def gate_col(br):
            return jnp.concatenate([g_ref[0, 0, :, hh * 3 * G + g * 3 + br:hh * 3 * G + g * 3 + br + 1]
                                    for g in range(G)], axis=0)

        def weight_col(br, acc):
            c = HEAD_DIM if hh == 0 else 0
            return gate_col(br) * (1.0 / jnp.maximum(acc[:, c:c + 1], 1e-30))

        o = weight_col(0, o_c) * o_c + weight_col(1, acc_s) * acc_s + weight_col(2, acc_w) * acc_w
        for vc in range(G // 2):
            lo_piece = _half_to(o[(2 * vc) * tq:(2 * vc + 1) * tq], hh == 1, False)
            hi_piece = _half_to(o[(2 * vc + 1) * tq:(2 * vc + 2) * tq], hh == 1, True)
            o_ref[0, :, (hh * 2 + vc) * LANES:(hh * 2 + vc + 1) * LANES] = jnp.where(low, lo_piece, hi_piece)

    prepared = [prepare(hh) for hh in range(2)]
    attended = attend(prepared)
    for hh in range(2):
        finish(hh, prepared[hh]['o_c'], *attended[hh])


def nsa_prompt(q, gates, kcmp, vcmp, ks, vs, kw, vw, bias_dense, bias_cmp, cover, bias_range, n_cmp, n_sel):
    B, L, WQ = q.shape
    WK = ks.shape[2]
    group = WQ // WK
    tq = 256
    nt = tq // LANES
    assert n_sel < HEAD_DIM
    n_pairs = WK // LANES
    n_top = min(SLC_TOPN, n_sel)
    qw = WQ // n_pairs
    hp = bias_dense.shape[1] // n_pairs
    kv = pl.BlockSpec((1, L, LANES), lambda b, p, i: (b, 0, p))
    cm = pl.BlockSpec((1, kcmp.shape[1], LANES), lambda b, p, i: (b, 0, p))
    return pl.pallas_call(
        functools.partial(_nsa_prompt_kernel, tq=tq, n_cmp=n_cmp, n_sel=n_sel, n_top=n_top, group=group),
        grid=(B, n_pairs, L // tq),
        in_specs=[
            pl.BlockSpec((1, tq, qw), lambda b, p, i: (b, i, p)),
            pl.BlockSpec((1, 1, tq, LANES), lambda b, p, i: (p, b, i, 0)),
            cm, cm, kv, kv, kv, kv,
            pl.BlockSpec((bias_dense.shape[0], hp, LANES, LANES), lambda b, p, i: (0, p, 0, 0)),
            pl.BlockSpec((nt, hp, LANES, LANES), lambda b, p, i: (i, p, 0, 0)),
            _const_spec(cover.shape),
            pl.BlockSpec(memory_space=pltpu.SMEM),
        ],
        out_specs=pl.BlockSpec((1, tq, qw), lambda b, p, i: (b, i, p)),
        out_shape=jax.ShapeDtypeStruct(q.shape, F32),
        scratch_shapes=[pltpu.VMEM((WINDOW // tq + 1, tq, tq), F32), pltpu.VMEM((6, 8, LANES), F32)],
        compiler_params=_cparams(("parallel", "parallel", "arbitrary")),
        name="nsa_prompt",
    )(q, gates, kcmp, vcmp, ks, vs, kw, vw, bias_dense, bias_cmp, cover, bias_range)


def _nsa_sample_kernel(pt_ref, q_ref, g_ref, kc_ref, vc_ref, ksn_ref, vsn_ref, kwp_ref, vwp_ref,
                       kwn_ref, vwn_ref, bs_ref, bn_ref, bc_ref, bw_ref, cov_ref, e_ref, *rest,
                       n_tok, q_base, n_cmp, n_sel, n_top, group, pages):
    k_refs, v_refs = rest[:pages], rest[pages:2 * pages]
    o_ref, qs_sc, sel_sc, oc_sc, m_sc, l_sc, acc_sc = rest[2 * pages:]
    ci = pl.program_id(1)
    R, W = qs_sc.shape
    n_heads = R // n_tok
    J = sel_sc.shape[1]

    def tok(shape):
        return _iota(shape, 0) & (n_tok - 1)

    @pl.when(ci == 0)
    def _():
        blocks = []
        for h in range(n_heads):
            hkv = h // group
            x = q_ref[0, :, (h // 2) * LANES:(h // 2 + 1) * LANES] * QSCALE
            x = _half_to(x, h % 2 == 1, hkv % 2 == 1)
            keep = (_iota(x.shape, 1) >= HEAD_DIM) if hkv % 2 == 1 else (_iota(x.shape, 1) < HEAD_DIM)
            x = jnp.where(keep, x, 0.0)
            blocks.append(jnp.concatenate(
                [x if cc == hkv // 2 else jnp.zeros_like(x) for cc in range(W // LANES)], axis=1))
        qs = jnp.concatenate(blocks, axis=0).astype(BF16)
        qs_sc[...] = qs
        s = _dot_nt(qs, kc_ref[0].astype(BF16)) + bc_ref[...]
        n = _iota(s.shape, 1)
        dist_c = q_base + tok(s.shape) - (n * CMP_STRIDE + (CMP_BLOCK - 1))
        p_c = _masked_softmax(s, (dist_c >= 0) & (n < n_cmp)).astype(BF16)
        oc_sc[...] = _dot(p_c, vc_ref[0].astype(BF16))
        ic = _dot(p_c, cov_ref[...])
        rows = group * n_tok
        imp_blocks = []
        for hkv in range(n_heads // group):
            a = ic[hkv * rows:hkv * rows + n_tok]
            for g in range(1, group):
                a = a + ic[hkv * rows + g * n_tok:hkv * rows + (g + 1) * n_tok]
            imp_blocks += [a] * group
        imp = jnp.concatenate(imp_blocks, axis=0)
        sel_sc[...] = _top_blocks(_block_scores(imp, q_base + tok(imp.shape), n_sel), n_sel, n_top)
        m0, l0, a0 = _softmax_init(R, W)
        m_sc[...] = m0
        l_sc[...] = l0
        acc_sc[...] = a0

    qs = qs_sc[...]
    sel = sel_sc[...].astype(BF16)

    kt = jnp.concatenate([r[0, 0] for r in k_refs], axis=1).astype(BF16)
    vt = jnp.concatenate([r[0, 0] for r in v_refs], axis=1).astype(BF16)
    s = _dot(qs, kt) + bs_ref[...] + _off_mask(_dot(sel, e_ref[...]) > 0.5)
    m, l, acc = _softmax_update_add(s, m_sc[...], l_sc[...], acc_sc[...], vt, True)
    m_sc[...] = m
    l_sc[...] = l
    acc_sc[...] = acc

    @pl.when(ci == pl.num_programs(1) - 1)
    def _():
        tn = ksn_ref.shape[1]
        c = _iota((R, tn), 1)
        causal_new = c <= tok((R, tn))
        s = _dot_nt(qs, ksn_ref[0].astype(BF16)) + bn_ref[...]
        chosen = _dot(sel, _block_of_key_01(J, tn, q_base))
        _, l1, a1 = _softmax_update_add(s + _off_mask((chosen > 0.5) & causal_new), m, l, acc,
                                        vsn_ref[0].astype(BF16))
        o_s = _softmax_finish(l1, a1)
        s = _dot(qs, kwp_ref[0, 0].astype(BF16)) + bw_ref[...]
        dist = WINDOW + tok(s.shape) - _iota(s.shape, 1)
        cw = _softmax_update_add(s + _off_mask((dist >= 0) & (dist < WINDOW)), *_softmax_init(R, W),
                                 vwp_ref[0, 0].astype(BF16), True)
        s = _dot_nt(qs, kwn_ref[0].astype(BF16)) + bn_ref[...]
        _, l2, a2 = _softmax_update_add(s + _off_mask(causal_new), *cw, vwn_ref[0].astype(BF16))
        o_w = _softmax_finish(l2, a2)

        def gate_col(br):
            cols = []
            for h in range(n_heads):
                hkv, g = divmod(h, group)
                cidx = (hkv % 2) * 3 * group + g * 3 + br
                cols.append(g_ref[hkv // 2, 0, :, cidx:cidx + 1])
            return jnp.concatenate(cols, axis=0)

        o = gate_col(0) * oc_sc[...] + gate_col(1) * o_s + gate_col(2) * o_w
        low = _iota((n_tok, LANES), 1) < HEAD_DIM
        for oc in range(n_heads // 2):
            hkv = (2 * oc) // group
            src = slice((hkv // 2) * LANES, (hkv // 2 + 1) * LANES)
            lo_piece = _half_to(o[(2 * oc) * n_tok:(2 * oc + 1) * n_tok, src], hkv % 2 == 1, False)
            hi_piece = _half_to(o[(2 * oc + 1) * n_tok:(2 * oc + 2) * n_tok, src], hkv % 2 == 1, True)
            o_ref[0, :, oc * LANES:(oc + 1) * LANES] = jnp.where(low, lo_piece, hi_piece)


def nsa_sample(q, gates, kcmp, vcmp, pool_k, pool_v, win_k, win_v, layer, page_table, ks_new, vs_new,
               kw_new, vw_new, bias_slc, bias_new, bias_cmp, bias_win, cover, n_cmp, n_sel):
    B, n_tok, WQ = q.shape
    WK, page = pool_k.shape[2], pool_k.shape[3]
    n_pages = page_table.shape[1]
    past = n_pages * page
    group = WQ // WK
    n_heads = WQ // HEAD_DIM
    R = n_heads * n_tok
    J = cover.shape[1]
    pages = PAGE_GROUP
    chunk = pages * page
    block_of_key = (jnp.arange(past)[None, :] // SLC_BLOCK == jnp.arange(J)[:, None]).astype(BF16)

    def bspec(a):
        return pl.BlockSpec((1,) + a.shape[1:], lambda b, c, pt: (b,) + (0,) * (a.ndim - 1))

    def cspec(a):
        return pl.BlockSpec(a.shape, lambda b, c, pt: (0,) * a.ndim, pipeline_mode=pl.Buffered(1))

    def page_spec(kk):
        return pl.BlockSpec((1, 1, WK, page), lambda b, c, pt: (layer, pt[b, c * pages + kk], 0, 0))

    wspec = pl.BlockSpec((1, 1, WK, win_k.shape[3]), lambda b, c, pt: (layer, b, 0, 0))
    grid_spec = pltpu.PrefetchScalarGridSpec(
        num_scalar_prefetch=1,
        grid=(B, n_pages // pages),
        in_specs=[
            bspec(q),
            pl.BlockSpec((gates.shape[0], 1, n_tok, LANES), lambda b, c, pt: (0, b, 0, 0)),
            bspec(kcmp), bspec(vcmp), bspec(ks_new), bspec(vs_new), wspec, wspec, bspec(kw_new), bspec(vw_new),
            pl.BlockSpec((R, chunk), lambda b, c, pt: (0, c)),
            cspec(bias_new), cspec(bias_cmp), cspec(bias_win), cspec(cover),
            pl.BlockSpec((J, chunk), lambda b, c, pt: (0, c)),
        ] + [page_spec(kk) for kk in range(pages)] * 2,
        out_specs=bspec(q),
        scratch_shapes=[pltpu.VMEM((R, WK), BF16), pltpu.VMEM((R, J), F32), pltpu.VMEM((R, WK), F32),
                        pltpu.VMEM((R, 1), F32), pltpu.VMEM((R, 1), F32), pltpu.VMEM((R, WK), F32)],
    )
    return pl.pallas_call(
        functools.partial(_nsa_sample_kernel, n_tok=n_tok, q_base=past, n_cmp=n_cmp, n_sel=n_sel,
                          n_top=min(SLC_TOPN, n_sel), group=group, pages=pages),
        grid_spec=grid_spec,
        out_shape=jax.ShapeDtypeStruct(q.shape, F32),
        compiler_params=_cparams(("parallel", "arbitrary")),
        name="nsa_sample",
    )(page_table, q, gates, kcmp, vcmp, ks_new, vs_new, win_k, win_v, kw_new, vw_new,
      bias_slc, bias_new, bias_cmp, bias_win, cover, block_of_key, *([pool_k] * pages), *([pool_v] * pages))


def _block_diag_ones(width):
    h = jnp.arange(width) // HEAD_DIM
    return (h[:, None] == h[None, :]).astype(BF16)


def _cover_01(n_cmp, n_sel, rows, cols):
    ci = jnp.arange(rows) * CMP_STRIDE
    sj = jnp.arange(cols) * SLC_BLOCK
    hit = (ci[:, None] < sj[None, :] + SLC_BLOCK) & (ci[:, None] + CMP_BLOCK > sj[None, :])
    hit = hit & (jnp.arange(rows)[:, None] < n_cmp) & (jnp.arange(cols)[None, :] < n_sel)
    return hit.astype(BF16)


def _round_up(n, m):
    return -(-n // m) * m


def _compress_weights(w1, b1, w2, b2, n_kv):
    hidden = w1.shape[1]
    eye = jnp.eye(n_kv, dtype=F32)
    w1r = w1.reshape(CMP_BLOCK // CMP_STRIDE, CMP_STRIDE, HEAD_DIM, hidden)
    rk = 2
    w1e = jnp.einsum('mrde,hk->mrhdke', w1r, eye).reshape(
        CMP_BLOCK // CMP_STRIDE, CMP_STRIDE // rk, rk * n_kv * HEAD_DIM, n_kv * hidden).astype(BF16)
    w2e = jnp.einsum('ed,hk->hekd', w2, eye).reshape(n_kv * hidden, n_kv * HEAD_DIM).astype(BF16)
    return w1e, jnp.tile(b1, n_kv)[None, :], w2e, jnp.tile(b2, n_kv)[None, :]


def _trunk(x, mods, past, P):
    B, L, D = x.shape
    depth = P['norm_g'].shape[0]
    even_states, odd_states = [], []
    for li in range(depth):
        def m(k):
            return mods[li, :, k][:, None, :]

        def g(k):
            return P['norm_g'][li, k][None, :]

        x = ffn(x, m(0), m(1), m(2), g(0), P['ffn_w_in'][li, 0], P['ffn_w_out'][li, 0])
        if li % 2 == 0:
            e = li // 2
            w = P['even_w_in'][e]
            n_sb = P['sb_width']
            HK, HV = P['gla_hk'], P['gla_hv']
            offs = [0, n_sb, 2 * n_sb, 3 * n_sb, 3 * n_sb + HK, 3 * n_sb + 2 * HK, 3 * n_sb + 2 * HK + HV,
                    3 * n_sb + 2 * HK + 2 * HV]
            ws = [w[:, offs[k]:offs[k + 1]] for k in range(7)]
            rank = w.shape[1] - offs[7]
            wa1 = jnp.pad(w[:, offs[7]:], ((0, 0), (0, LANES - rank)))
            wa2 = jnp.pad(P['gla_w_a2'][e], ((0, LANES - rank), (0, 0)))
            qa, ka, va, qb, kb, vb, rb, la = even_proj(x, m(3), m(4), g(1), ws, wa1, wa2, P['gla_b_a2'][e][None, :])
            dv = P['gla_onorm_g'].shape[1]
            n_gh = HV // dv
            dk = HK // n_gh
            if past is None:
                o_a = sb_prompt(qa, ka, va)
                s0t = jnp.zeros((B, dv, HK), F32)
            else:
                o_a = sb_sample(qa, ka, va, past['sb_k'], past['sb_v'], e, past['page_table'])
                s0t = past['gla'][e].transpose(0, 3, 1, 2).reshape(B, dv, HK)
            o_b, st = gla(qb, kb, vb, rb, la, s0t, P['gla_onorm_g'][e][None, :])
            wo = P['even_w_out'][e]
            mix_acts, mix_ws = [o_a, o_b], [wo[:n_sb], wo[n_sb:]]
            s_fin = st.reshape(B, dv, n_gh, dk).transpose(0, 2, 3, 1)
            even_states.append((ka.reshape(B, L, -1, HEAD_DIM), va.reshape(B, L, -1, HEAD_DIM), s_fin))
        else:
            o_i = li // 2
            w = P['odd_w_in'][o_i]
            WQ, WK = P['nsa_wq'], P['nsa_wk']
            n_kv = WK // HEAD_DIM
            group = WQ // WK
            offs = [0, WQ] + [WQ + (k + 1) * WK for k in range(6)]
            ws = [w[:, offs[k]:offs[k + 1]] for k in range(7)]
            wg = w[:, offs[7]:]
            half = wg.shape[1] // 2
            ws += [jnp.pad(wg[:, :half], ((0, 0), (0, LANES - half))),
                   jnp.pad(wg[:, half:], ((0, 0), (0, LANES - half)))]
            qk_g = P['nsa_qk_g'][o_i]
            consts = [_block_diag_ones(WQ), _block_diag_ones(WK), jnp.tile(qk_g[0], WQ // HEAD_DIM)[None, :],
                      jnp.tile(qk_g[2], n_kv)[None, :], jnp.tile(qk_g[3], n_kv)[None, :]]
            q, kc, vc, ks, vs, kw, vw, gates = odd_proj(x, m(3), m(4), g(1), ws, consts)
            pair = LANES // HEAD_DIM
            gk = jnp.tile(qk_g[1], pair)[None, :]
            bdp = _block_diag_ones(LANES)
            cw = [_compress_weights(P['cmp_w1'][o_i, t], P['cmp_b1'][o_i, t], P['cmp_w2'][o_i, t],
                                    P['cmp_b2'][o_i, t], pair) for t in range(2)]
            rel = P['rel_table']
            if past is None:
                n_rows = (L // CMP_STRIDE) * CMP_STRIDE
                assert n_rows == L and L % LANES == 0
                n_cmp = L // CMP_STRIDE - CMP_BLOCK // CMP_STRIDE + 1
                n_sel = -(-L // SLC_BLOCK)
                kcmp = compress(kc, n_rows, *cw[0], bdp, gk, True)
                vcmp = compress(vc, n_rows, *cw[1], bdp, gk, False)
                assert kcmp.shape[1] == LANES and n_sel <= LANES
                nq = L // LANES
                bias_dense = bias_tiles(rel, jnp.arange(nq, dtype=jnp.int32) * LANES, LANES, LANES, 1)
                bias_cmp = bias_tiles(rel, jnp.arange(nq, dtype=jnp.int32) * LANES - (CMP_BLOCK - 1),
                                      LANES, LANES, CMP_STRIDE)
                cover = _cover_01(n_cmp, n_sel, LANES, LANES).T
                bias_range = jnp.stack([jnp.max(rel), jnp.max(rel) - jnp.min(rel)]) * LOG2E
                o = nsa_prompt(q, gates, kcmp, vcmp, ks, vs, kw, vw, bias_dense, bias_cmp, cover, bias_range,
                               n_cmp, n_sel)
                w_keep = min(WINDOW, L)
                win_k, win_v = kw[:, L - w_keep:], vw[:, L - w_keep:]
            else:
                pt = past['page_table']
                past_len = pt.shape[1] * past['cmp_k'].shape[3]
                T = past_len + L
                n_rows = (T // CMP_STRIDE) * CMP_STRIDE
                assert n_rows == past_len and past['win_k'].shape[3] == WINDOW and L <= LANES
                n_cmp = n_rows // CMP_STRIDE - CMP_BLOCK // CMP_STRIDE + 1
                n_sel = -(-T // SLC_BLOCK)
                kcmp = compress_paged(past['cmp_k'], o_i, pt, *cw[0], bdp, gk, True)
                vcmp = compress_paged(past['cmp_v'], o_i, pt, *cw[1], bdp, gk, False)
                J = _round_up(n_sel, LANES)
                n_heads = WQ // HEAD_DIM
                R = n_heads * L

                def sample_bias(base, C, cstep):
                    return bias_tiles(rel, jnp.array([base], jnp.int32), L, C, cstep).reshape(R, C)

                bias_slc = sample_bias(past_len, past_len, 1)
                bias_new = sample_bias(0, LANES, 1)
                bias_cmp = sample_bias(past_len - (CMP_BLOCK - 1), kcmp.shape[1], CMP_STRIDE)
                bias_win = sample_bias(WINDOW, WINDOW, 1)
                cover = _cover_01(n_cmp, n_sel, kcmp.shape[1], J)
                pad = ((0, 0), (0, LANES - L), (0, 0))
                o = nsa_sample(q, gates, kcmp, vcmp, past['slc_k'], past['slc_v'], past['win_k'], past['win_v'],
                               o_i, pt, jnp.pad(ks, pad), jnp.pad(vs, pad), jnp.pad(kw, pad), jnp.pad(vw, pad),
                               bias_slc, bias_new, bias_cmp, bias_win, cover, n_cmp, n_sel)
                win_k = jnp.concatenate([past['win_k_rows'][o_i], kw], axis=1)[:, L:]
                win_v = jnp.concatenate([past['win_v_rows'][o_i], vw], axis=1)[:, L:]
            mix_acts, mix_ws = [o], [P['odd_w_out'][o_i]]

            def rows(t):
                return t.reshape(B, t.shape[1], n_kv, HEAD_DIM)

            odd_states.append(tuple(rows(t) for t in (kc, vc, ks, vs, win_k, win_v)))
        x = ffn(x, m(6), m(7), m(8), g(2), P['ffn_w_in'][li, 1], P['ffn_w_out'][li, 1], m(5), mix_acts, mix_ws)
    even_new = [jnp.stack([s[i] for s in even_states]) for i in range(3)]
    odd_new = [jnp.stack([s[i] for s in odd_states]) for i in range(6)]
    return x, even_new, odd_new


def kernel(x_prompt, x_sample, cache_sb_k, cache_sb_v, state_gla, cache_cmp_k, cache_cmp_v, cache_slc_k,
           cache_slc_v, cache_win_k, cache_win_v, page_table, c_prompt, c_sample, norm_g, ada_w, ada_b,
           ffn_w_in, ffn_w_out, even_w_in, gla_w_a2, gla_b_a2, gla_onorm_g, even_w_out, odd_w_in, nsa_qk_g,
           cmp_w1, cmp_b1, cmp_w2, cmp_b2, rel_table, odd_w_out):
    D = x_prompt.shape[-1]
    depth = norm_g.shape[0]
    Bp, Bs = x_prompt.shape[0], x_sample.shape[0]
    mods = ada_mod(jnp.concatenate([c_prompt, c_sample], axis=0), ada_w, ada_b).reshape(depth, Bp + Bs, 9, D)
    n_sb = cache_sb_k.shape[3] * cache_sb_k.shape[4]
    dv = gla_onorm_g.shape[1]
    HK = gla_w_a2.shape[2]
    n_gh = state_gla.shape[2]
    WK = cache_cmp_k.shape[3] * cache_cmp_k.shape[4]
    P = {
        'norm_g': norm_g, 'ffn_w_in': ffn_w_in.astype(BF16), 'ffn_w_out': ffn_w_out.astype(BF16),
        'even_w_in': even_w_in.astype(BF16), 'gla_w_a2': gla_w_a2.astype(BF16), 'gla_b_a2': gla_b_a2,
        'gla_onorm_g': gla_onorm_g, 'even_w_out': even_w_out.astype(BF16), 'odd_w_in': odd_w_in.astype(BF16),
        'nsa_qk_g': nsa_qk_g, 'cmp_w1': cmp_w1, 'cmp_b1': cmp_b1, 'cmp_w2': cmp_w2, 'cmp_b2': cmp_b2,
        'rel_table': rel_table, 'odd_w_out': odd_w_out.astype(BF16),
        'sb_width': n_sb, 'gla_hk': HK, 'gla_hv': n_gh * dv, 'nsa_wq': odd_w_out.shape[1], 'nsa_wk': WK,
    }

    def pool(c):
        return c.transpose(0, 1, 3, 4, 2).reshape(c.shape[0], c.shape[1], c.shape[3] * c.shape[4], c.shape[2])

    def rows(c):
        return c.reshape(c.shape[0], c.shape[1], c.shape[2], c.shape[3] * c.shape[4])

    past = {
        'page_table': page_table, 'sb_k': pool(cache_sb_k), 'sb_v': pool(cache_sb_v), 'gla': state_gla,
        'cmp_k': pool(cache_cmp_k), 'cmp_v': pool(cache_cmp_v), 'slc_k': pool(cache_slc_k),
        'slc_v': pool(cache_slc_v), 'win_k': pool(cache_win_k), 'win_v': pool(cache_win_v),
        'win_k_rows': rows(cache_win_k), 'win_v_rows': rows(cache_win_v),
    }
    y_p, p_even, p_odd = _trunk(x_prompt, mods[:, :Bp], None, P)
    y_s, s_even, s_odd = _trunk(x_sample, mods[:, Bp:], past, P)
    return (y_p, y_s, *p_even, *p_odd, *s_even, *s_odd)
```

```python
import functools
import math

import jax
import jax.numpy as jnp
from jax import lax
from jax.experimental import pallas as pl
from jax.experimental.pallas import tpu as pltpu

F32 = jnp.float32
BF16 = jnp.bfloat16

EPS = 1e-6
NEG = -1e30
FORCE = 1e6
HEAD_DIM = 64
GLA_TAU = 16.0
GLA_CHUNK = 64
CMP_BLOCK = 32
CMP_STRIDE = 16
SLC_BLOCK = 64
SLC_TOPN = 16
WINDOW = 512
REL_BUCKETS = 32
REL_MAX_DIST = 1024
PAGE_GROUP = 16
LANES = 128
VMEM_LIMIT = 56 * 1024 * 1024


def _cparams(sem, vmem=VMEM_LIMIT):
    return pltpu.CompilerParams(dimension_semantics=sem, vmem_limit_bytes=vmem)


def _dot(a, b):
    return jnp.dot(a, b, preferred_element_type=F32)


def _dot_nt(a, b):
    return lax.dot_general(a, b, (((1,), (1,)), ((), ())), preferred_element_type=F32)


def _dot_tn(a, b):
    return lax.dot_general(a, b, (((0,), (0,)), ((), ())), preferred_element_type=F32)


def _split2(x):
    hi = x.astype(BF16)
    lo = (x - hi.astype(F32)).astype(BF16)
    return hi, lo


def _dot_f32_by01(x, m01):
    hi, lo = _split2(x)
    return _dot(hi, m01) + _dot(lo, m01)


def _dot_01_by_f32(m01, x):
    hi, lo = _split2(x)
    return _dot(m01, hi) + _dot(m01, lo)


def _softplus(z):
    return jnp.maximum(z, 0.0) + jnp.log1p(jnp.exp(-jnp.abs(z)))


LOG2E = 1.4426950408889634
QSCALE = HEAD_DIM ** -0.5 * LOG2E
MAX_SHIFT_SPAN = 100.0


def _sb_logs2(z2):
    l2 = jnp.log(1.0 + jnp.exp2(-jnp.abs(z2))) * LOG2E
    lk = -(jnp.maximum(z2, 0.0) + l2)
    return lk, z2 + lk


def _silu(a):
    return a * jax.nn.sigmoid(a)


def _iota(shape, dim):
    return lax.broadcasted_iota(jnp.int32, shape, dim)


def _modulate(x, g, shift, scale):
    ms = jnp.mean(x * x, axis=-1, keepdims=True)
    y = x * lax.rsqrt(ms + EPS) * g
    return y * (1.0 + scale) + shift


def _head_rms(x, bd, g):
    ms = _dot_f32_by01(x * x, bd) * (1.0 / HEAD_DIM)
    return x * lax.rsqrt(ms + EPS) * g


def _const_spec(shape):
    nd = len(shape)
    return pl.BlockSpec(shape, lambda *_: (0,) * nd, pipeline_mode=pl.Buffered(1))


def _row_tiles(B, L):
    if L >= 512:
        return 1, 512
    return B, L


def _ada_kernel(c_ref, w_ref, b_ref, o_ref):
    c = c_ref[...]
    a = _silu(c).astype(BF16)
    o_ref[0] = _dot(a, w_ref[0].astype(BF16)) + b_ref[0]


def ada_mod(c, ada_w, ada_b):
    R, D = c.shape
    depth, _, N = ada_w.shape
    tn = 1024
    return pl.pallas_call(
        _ada_kernel,
        grid=(depth, N // tn),
        in_specs=[
            pl.BlockSpec((R, D), lambda l, n: (0, 0)),
            pl.BlockSpec((1, D, tn), lambda l, n: (l, 0, n)),
            pl.BlockSpec((1, 1, tn), lambda l, n: (l, 0, n)),
        ],
        out_specs=pl.BlockSpec((1, R, tn), lambda l, n: (l, 0, n)),
        out_shape=jax.ShapeDtypeStruct((depth, R, N), F32),
        compiler_params=_cparams(("parallel", "parallel")),
        name="ada_mod",
    )(c, ada_w, ada_b.reshape(depth, 1, N))


def _ffn_kernel(x_ref, sh_ref, sc_ref, gt_ref, g_ref, win_ref, wout_ref, *rest, d_ff, tf, n_mix):
    o_ref = rest[-1]
    x = x_ref[...]
    tb, tl, D = x.shape
    if n_mix:
        mix = jnp.zeros((tb * tl, D), F32)
        for a, w in zip(rest[1:1 + n_mix], rest[1 + n_mix:1 + 2 * n_mix]):
            av = a[...]
            mix = mix + _dot(av.reshape(tb * tl, av.shape[-1]).astype(BF16), w[...])
        x = x + rest[0][...] * mix.reshape(tb, tl, D)
    h = _modulate(x, g_ref[...], sh_ref[...], sc_ref[...])
    hb = h.reshape(tb * tl, D).astype(BF16)
    acc = jnp.zeros((tb * tl, D), F32)
    for j in range(d_ff // tf):
        a = _dot(hb, win_ref[:, j * tf:(j + 1) * tf])
        b = _dot(hb, win_ref[:, d_ff + j * tf:d_ff + (j + 1) * tf])
        u = (_silu(a) * b).astype(BF16)
        acc = acc + _dot(u, wout_ref[j * tf:(j + 1) * tf, :])
    o_ref[...] = x + 0.5 * gt_ref[...] * acc.reshape(tb, tl, D)


def ffn(x, shift, scale, gate, g, w_in, w_out, mix_gate=None, mix_acts=(), mix_ws=()):
    B, L, D = x.shape
    d_ff = w_out.shape[0]
    tb, tl = _row_tiles(B, L)
    xs = pl.BlockSpec((tb, tl, D), lambda b, l: (b, l, 0))
    ms = pl.BlockSpec((tb, 1, D), lambda b, l: (b, 0, 0))
    n_mix = len(mix_acts)
    mix_specs = []
    if n_mix:
        mix_specs = [ms] + [pl.BlockSpec((tb, tl, a.shape[-1]), lambda b, l: (b, l, 0)) for a in mix_acts] \
            + [_const_spec(w.shape) for w in mix_ws]
    return pl.pallas_call(
        functools.partial(_ffn_kernel, d_ff=d_ff, tf=256, n_mix=n_mix),
        grid=(B // tb, L // tl),
        in_specs=[xs, ms, ms, ms, _const_spec((1, D)), _const_spec(w_in.shape), _const_spec(w_out.shape)]
        + mix_specs,
        out_specs=xs,
        out_shape=jax.ShapeDtypeStruct(x.shape, F32),
        compiler_params=_cparams(("parallel", "parallel")),
        name="ffn",
    )(x, shift, scale, gate, g, w_in, w_out, *(([mix_gate] if n_mix else []) + list(mix_acts) + list(mix_ws)))


def _even_proj_kernel(x_ref, sh_ref, sc_ref, g_ref, wqa, wka, wva, wqb, wkb, wvb, wrb, wa1, wa2, ba2,
                      qa, ka, va, qb, kb, vb, rb, la):
    x = x_ref[...]
    tb, tl, D = x.shape
    h = _modulate(x, g_ref[...], sh_ref[...], sc_ref[...])
    hb = h.reshape(tb * tl, D).astype(BF16)
    for w, o in ((wqa, qa), (wka, ka), (wva, va), (wqb, qb), (wkb, kb), (wvb, vb), (wrb, rb)):
        o[...] = _dot(hb, w[...]).reshape(o.shape)
    a1 = _dot(hb, wa1[...]).astype(BF16)
    z = _dot(a1, wa2[...]) + ba2[...]
    la[...] = (-_softplus(-z) * (1.0 / GLA_TAU)).reshape(la.shape)


def even_proj(x, shift, scale, g, ws, wa1, wa2, ba2):
    B, L, D = x.shape
    tb, tl = _row_tiles(B, L)
    xs = pl.BlockSpec((tb, tl, D), lambda b, l: (b, l, 0))
    ms = pl.BlockSpec((tb, 1, D), lambda b, l: (b, 0, 0))
    widths = [w.shape[1] for w in ws] + [wa2.shape[1]]
    return pl.pallas_call(
        _even_proj_kernel,
        grid=(B // tb, L // tl),
        in_specs=[xs, ms, ms, _const_spec((1, D))] + [_const_spec(w.shape) for w in ws]
        + [_const_spec(wa1.shape), _const_spec(wa2.shape), _const_spec(ba2.shape)],
        out_specs=[pl.BlockSpec((tb, tl, n), lambda b, l: (b, l, 0)) for n in widths],
        out_shape=[jax.ShapeDtypeStruct((B, L, n), F32) for n in widths],
        compiler_params=_cparams(("parallel", "parallel")),
        name="even_proj",
    )(x, shift, scale, g, *ws, wa1, wa2, ba2)


def _odd_proj_kernel(x_ref, sh_ref, sc_ref, g_ref, wq, wkc, wvc, wks, wvs, wkw, wvw, wg0, wg1,
                     hq, hk, gq, gks, gkw,
                     q, kc, vc, ks, vs, kw, vw, gates):
    x = x_ref[...]
    tb, tl, D = x.shape
    h = _modulate(x, g_ref[...], sh_ref[...], sc_ref[...])
    hb = h.reshape(tb * tl, D).astype(BF16)
    q[...] = _head_rms(_dot(hb, wq[...]), hq[...], gq[...]).reshape(q.shape)
    kc[...] = _dot(hb, wkc[...]).reshape(kc.shape)
    vc[...] = _dot(hb, wvc[...]).reshape(vc.shape)
    ks[...] = _head_rms(_dot(hb, wks[...]), hk[...], gks[...]).reshape(ks.shape)
    vs[...] = _dot(hb, wvs[...]).reshape(vs.shape)
    kw[...] = _head_rms(_dot(hb, wkw[...]), hk[...], gkw[...]).reshape(kw.shape)
    vw[...] = _dot(hb, wvw[...]).reshape(vw.shape)
    gates[0] = jax.nn.sigmoid(_dot(hb, wg0[...])).reshape(gates.shape[1:])
    gates[1] = jax.nn.sigmoid(_dot(hb, wg1[...])).reshape(gates.shape[1:])


def odd_proj(x, shift, scale, g, ws, consts):
    B, L, D = x.shape
    tb, tl = _row_tiles(B, L)
    xs = pl.BlockSpec((tb, tl, D), lambda b, l: (b, l, 0))
    ms = pl.BlockSpec((tb, 1, D), lambda b, l: (b, 0, 0))
    widths = [w.shape[1] for w in ws[:7]]
    out_specs = [pl.BlockSpec((tb, tl, n), lambda b, l: (b, l, 0)) for n in widths]
    out_specs.append(pl.BlockSpec((2, tb, tl, LANES), lambda b, l: (0, b, l, 0)))
    out_shape = [jax.ShapeDtypeStruct((B, L, n), F32) for n in widths]
    out_shape.append(jax.ShapeDtypeStruct((2, B, L, LANES), F32))
    return pl.pallas_call(
        _odd_proj_kernel,
        grid=(B // tb, L // tl),
        in_specs=[xs, ms, ms, _const_spec((1, D))] + [_const_spec(w.shape) for w in ws]
        + [_const_spec(c.shape) for c in consts],
        out_specs=out_specs,
        out_shape=out_shape,
        compiler_params=_cparams(("parallel", "parallel")),
        name="odd_proj",
    )(x, shift, scale, g, *ws, *consts)


def _log2(n):
    assert n > 0 and n & (n - 1) == 0, n
    return n.bit_length() - 1


def _later_keys_01(tk):
    return jnp.where(_iota((tk, tk), 0) > _iota((tk, tk), 1), 1.0, 0.0).astype(BF16)


def _sb_tile(q2, kt, vt, run, acc, u01, before, feature_major=False):
    lk, ls = _sb_logs2(_dot(q2, kt) if feature_major else _dot_nt(q2, kt))
    if before is not None:
        lk = jnp.where(before, lk, 0.0)
    cs = _dot_f32_by01(lk, u01)
    w = jnp.exp2(ls + cs + run)
    if before is not None:
        w = jnp.where(before, w, 0.0)
    wb = w.astype(BF16)
    acc = acc + (_dot_nt(wb, vt) if feature_major else _dot(wb, vt))
    run = run + cs[:, 0:1] + lk[:, 0:1]
    return run, acc


def _sb_prompt_kernel(q_ref, k_ref, v_ref, o_ref, *, tq):
    i = pl.program_id(2)
    n_pairs = q_ref.shape[2] // LANES
    lane = _iota((tq, LANES), 1)
    u01 = _later_keys_01(tq)
    before = _iota((2 * tq, tq), 1) < (_iota((2 * tq, tq), 0) & (tq - 1))
    q2 = []
    for p in range(n_pairs):
        q = q_ref[0, :, p * LANES:(p + 1) * LANES] * QSCALE
        q2.append(jnp.concatenate([jnp.where(lane < HEAD_DIM, q, 0.0), jnp.where(lane >= HEAD_DIM, q, 0.0)],
                                  axis=0).astype(BF16))

    def tile(j, carry, mask):
        start = pl.multiple_of(j * tq, tq)
        out = []
        for p in range(n_pairs):
            kt = k_ref[0, pl.ds(start, tq), p * LANES:(p + 1) * LANES].astype(BF16)
            vt = v_ref[0, pl.ds(start, tq), p * LANES:(p + 1) * LANES].astype(BF16)
            out.append(_sb_tile(q2[p], kt, vt, *carry[p], u01, mask))
        return tuple(out)

    init = tuple((jnp.zeros((2 * tq, 1), F32), jnp.zeros((2 * tq, LANES), F32)) for _ in range(n_pairs))
    carry = lax.fori_loop(i, i + 1, lambda j, c: tile(j, c, before), init)
    carry = lax.fori_loop(1, i + 1, lambda jj, c: tile(i - jj, c, None), carry)
    for p in range(n_pairs):
        acc = carry[p][1]
        o_ref[0, :, p * LANES:(p + 1) * LANES] = jnp.where(lane < HEAD_DIM, acc[:tq], acc[tq:])


def sb_prompt(q, k, v):
    B, L, W = q.shape
    tq = 256
    wb = 4 * LANES
    qs = pl.BlockSpec((1, tq, wb), lambda b, p, i: (b, i, p))
    ks = pl.BlockSpec((1, L, wb), lambda b, p, i: (b, 0, p))
    return pl.pallas_call(
        functools.partial(_sb_prompt_kernel, tq=tq),
        grid=(B, W // wb, L // tq),
        in_specs=[qs, ks, ks],
        out_specs=qs,
        out_shape=jax.ShapeDtypeStruct(q.shape, F32),
        compiler_params=_cparams(("parallel", "parallel", "arbitrary")),
        name="sb_prompt",
    )(q, k, v)


def _sb_sample_kernel(pt_ref, q_ref, kn_ref, vn_ref, *rest, n_tok, group):
    k_refs, v_refs = rest[:group], rest[group:2 * group]
    o_ref, run_ref, acc_ref = rest[2 * group:]
    s = pl.program_id(1)
    q = q_ref[0] * QSCALE
    W = q.shape[1]
    n_heads = W // HEAD_DIM
    R = n_heads * n_tok
    tk = kn_ref.shape[1]
    own = (_iota((R, W), 0) >> _log2(n_tok)) == (_iota((R, W), 1) >> _log2(HEAD_DIM))
    q2 = jnp.where(own, jnp.concatenate([q] * n_heads, axis=0), 0.0).astype(BF16)
    u01 = _later_keys_01(tk)

    @pl.when(s == 0)
    def _():
        before = _iota((R, tk), 1) < (_iota((R, tk), 0) & (n_tok - 1))
        run, acc = _sb_tile(q2, kn_ref[0].astype(BF16), vn_ref[0].astype(BF16),
                            jnp.zeros((R, 1), F32), jnp.zeros((R, W), F32), u01, before)
        run_ref[...] = run
        acc_ref[...] = acc

    @pl.when(s > 0)
    def _():
        kt = jnp.concatenate([r[0, 0] for r in k_refs], axis=1).astype(BF16)
        vt = jnp.concatenate([r[0, 0] for r in v_refs], axis=1).astype(BF16)

        def stack(x):
            return jnp.concatenate([x[:, kk * tk:(kk + 1) * tk] for kk in range(group)], axis=0)

        lk, ls = _sb_logs2(_dot(q2, kt))
        lk = stack(lk)
        cs = _dot_f32_by01(lk, u01)
        tot = cs[:, 0:1] + lk[:, 0:1]
        run = run_ref[...]
        runs = []
        for kk in range(group):
            runs.append(run)
            run = run + tot[kk * R:(kk + 1) * R]
        w = jnp.exp2(stack(ls) + cs + jnp.concatenate(runs, axis=0))
        w = jnp.concatenate([w[kk * R:(kk + 1) * R] for kk in range(group)], axis=1).astype(BF16)
        acc_ref[...] = acc_ref[...] + _dot_nt(w, vt)
        run_ref[...] = run

    @pl.when(s == pl.num_programs(1) - 1)
    def _():
        acc = jnp.where(own, acc_ref[...], 0.0)
        out = acc[0:n_tok]
        for h in range(1, n_heads):
            out = out + acc[h * n_tok:(h + 1) * n_tok]
        o_ref[0] = out


def sb_sample(q, k_new, v_new, pool_k, pool_v, layer, page_table):
    B, n_tok, W = q.shape
    page = pool_k.shape[3]
    n_pages = page_table.shape[1]
    group = PAGE_GROUP
    n_steps = 1 + n_pages // group
    pad = ((0, 0), (0, page - n_tok), (0, 0))
    kn, vn = jnp.pad(k_new, pad), jnp.pad(v_new, pad)

    def page_spec(kk):
        def imap(b, s, pt):
            return (layer, pt[b, n_pages - 1 - (jnp.maximum(s, 1) - 1) * group - kk], 0, 0)
        return pl.BlockSpec((1, 1, W, page), imap)

    qs = pl.BlockSpec((1, n_tok, W), lambda b, s, pt: (b, 0, 0))
    ns = pl.BlockSpec((1, page, W), lambda b, s, pt: (b, 0, 0))
    R = (W // HEAD_DIM) * n_tok
    grid_spec = pltpu.PrefetchScalarGridSpec(
        num_scalar_prefetch=1,
        grid=(B, n_steps),
        in_specs=[qs, ns, ns] + [page_spec(kk) for kk in range(group)] * 2,
        out_specs=qs,
        scratch_shapes=[pltpu.VMEM((R, 1), F32), pltpu.VMEM((R, W), F32)],
    )
    return pl.pallas_call(
        functools.partial(_sb_sample_kernel, n_tok=n_tok, group=group),
        grid_spec=grid_spec,
        out_shape=jax.ShapeDtypeStruct(q.shape, F32),
        compiler_params=_cparams(("parallel", "arbitrary")),
        name="sb_sample",
    )(page_table, q, kn, vn, *([pool_k] * group), *([pool_v] * group))


def _gla_kernel(q_ref, k_ref, v_ref, r_ref, la_ref, s0_ref, g_ref, o_ref, sfin_ref, st, *, dk, dv):
    c = pl.program_id(1)

    @pl.when(c == 0)
    def _():
        st[...] = s0_ref[...]

    for ib in range(q_ref.shape[0]):
        _gla_chunk(q_ref.at[ib], k_ref.at[ib], v_ref.at[ib], r_ref.at[ib], la_ref.at[ib], g_ref, o_ref.at[ib],
                   st.at[ib], dk=dk, dv=dv)

    @pl.when(c == pl.num_programs(1) - 1)
    def _():
        sfin_ref[...] = st[...]


def _gla_chunk(q_ref, k_ref, v_ref, r_ref, la_ref, g_ref, o_ref, st, *, dk, dv):
    q = q_ref[...] * (dk ** -0.5)
    k = k_ref[...]
    la = la_ref[...]
    C, HK = q.shape
    n_heads = HK // dk
    causal = _iota((C, C), 0) >= _iota((C, C), 1)
    ltri = jnp.where(causal, 1.0, 0.0).astype(BF16)
    hi = la.astype(BF16)
    r1 = la - hi.astype(F32)
    mid = r1.astype(BF16)
    lo = (r1 - mid.astype(F32)).astype(BF16)
    b = _dot(ltri, hi) + _dot(ltri, mid) + _dot(ltri, lo)
    b_last = b[C - 1:C, :]
    q_in = q * jnp.exp(b)
    k_in = (k * jnp.exp(-b)).astype(BF16)
    k_dec = k * jnp.exp(b_last - b)
    lane_h = _iota((C, HK), 1) >> _log2(dk)
    s_old = st[...]
    s_old_b = s_old.astype(BF16)
    upd = jnp.zeros(s_old.shape, F32)
    for h in range(n_heads):
        own = lane_h == h
        qh = jnp.where(own, q_in, 0.0).astype(BF16)
        att = jnp.where(causal, _dot_nt(qh, k_in), 0.0)
        vh = v_ref[:, h * dv:(h + 1) * dv].astype(BF16)
        oh = _dot(att.astype(BF16), vh) + _dot_nt(qh, s_old_b)
        y = oh * lax.rsqrt(jnp.mean(oh * oh, axis=-1, keepdims=True) + EPS) * g_ref[...]
        o_ref[:, h * dv:(h + 1) * dv] = y * _silu(r_ref[:, h * dv:(h + 1) * dv])
        kd = jnp.where(own, k_dec, 0.0).astype(BF16)
        upd = upd + _dot_tn(vh, kd)
    st[...] = s_old * jnp.exp(b_last) + upd


def gla(q, k, v, r, la, s0t, g):
    B, L, HK = q.shape
    HV = v.shape[2]
    dv = g.shape[1]
    dk = HK // (HV // dv)
    C = min(GLA_CHUNK, L)
    bb = next(n for n in (8, 4, 2, 1) if B % n == 0)
    ks = pl.BlockSpec((bb, C, HK), lambda b, c: (b, c, 0))
    vs = pl.BlockSpec((bb, C, HV), lambda b, c: (b, c, 0))
    ss = pl.BlockSpec((bb, dv, HK), lambda b, c: (b, 0, 0))
    return pl.pallas_call(
        functools.partial(_gla_kernel, dk=dk, dv=dv),
        grid=(B // bb, L // C),
        in_specs=[ks, ks, vs, vs, ks, ss, pl.BlockSpec((1, dv), lambda b, c: (0, 0))],
        out_specs=[vs, ss],
        out_shape=[jax.ShapeDtypeStruct(v.shape, F32), jax.ShapeDtypeStruct(s0t.shape, F32)],
        scratch_shapes=[pltpu.VMEM((bb, dv, HK), F32)],
        compiler_params=_cparams(("parallel", "arbitrary")),
        name="gla",
    )(q, k, v, r, la, s0t, g)


def _gelu_tanh(x):
    return x * (0.5 * (1.0 + jnp.tanh(math.sqrt(2.0 / math.pi) * (x + 0.044715 * (x * x * x)))))


def _compress_mlp(seg_rows, w1_ref, b1_ref, w2_ref, b2_ref, bd_ref, g_ref, o_ref, *, n_pairs, nseg, norm):
    rk = w1_ref.shape[2] // LANES
    wh = w1_ref.shape[3]
    R = n_pairs * nseg
    p0 = jnp.zeros((R, wh), F32)
    p1 = jnp.zeros((R, wh), F32)
    for rr in range(CMP_STRIDE // rk):
        rows = jnp.concatenate(
            [jnp.concatenate([seg_rows(p, rr * rk + u) for u in range(rk)], axis=1) for p in range(n_pairs)],
            axis=0).astype(BF16)
        p0 = p0 + _dot(rows, w1_ref[0, rr])
        p1 = p1 + _dot(rows, w1_ref[1, rr])
    hid = b1_ref[...] + p0 + pltpu.roll(p1, R - 1, 0)
    y = _dot(_gelu_tanh(hid).astype(BF16), w2_ref[...]) + b2_ref[...]
    if norm:
        y = _head_rms(y, bd_ref[...], g_ref[...])
    y = jnp.where((_iota(y.shape, 0) & (nseg - 1)) < nseg - 1, y, 0.0)
    for p in range(n_pairs):
        o_ref[0, :, p * LANES:(p + 1) * LANES] = y[p * nseg:(p + 1) * nseg]


def _compress_kernel(*refs, n_pairs, nseg, norm):
    x_refs = refs[:n_pairs]
    _compress_mlp(lambda p, r: x_refs[p][0, pl.ds(r, nseg, stride=CMP_STRIDE), :], *refs[n_pairs:],
                  n_pairs=n_pairs, nseg=nseg, norm=norm)


def compress(x, n_rows, w1e, b1e, w2e, b2e, bd, g, norm):
    B, _, W = x.shape
    nseg = n_rows // CMP_STRIDE
    _log2(nseg)
    n_pairs = W // LANES
    return pl.pallas_call(
        functools.partial(_compress_kernel, n_pairs=n_pairs, nseg=nseg, norm=norm),
        grid=(B,),
        in_specs=[pl.BlockSpec((1, n_rows, LANES), functools.partial(lambda b, p: (b, 0, p), p=p))
                  for p in range(n_pairs)]
        + [_const_spec(a.shape) for a in (w1e, b1e, w2e, b2e, bd, g)],
        out_specs=pl.BlockSpec((1, nseg, W), lambda b: (b, 0, 0)),
        out_shape=jax.ShapeDtypeStruct((B, nseg, W), F32),
        compiler_params=_cparams(("parallel",)),
        name="compress",
    )(*([x] * n_pairs), w1e, b1e, w2e, b2e, bd, g)


def _compress_paged_kernel(pt_ref, *refs, group, n_pairs, nseg, norm):
    page_refs = refs[:group]
    xs = refs[-1]
    s = pl.program_id(1)
    page = page_refs[0].shape[3]
    for kk, pr in enumerate(page_refs):
        start = pl.multiple_of((s * group + kk) * page, page)
        for p in range(n_pairs):
            xs[p, pl.ds(start, page), :] = pr[0, 0, p * LANES:(p + 1) * LANES, :].T

    @pl.when(s == pl.num_programs(1) - 1)
    def _():
        _compress_mlp(lambda p, r: xs[p, pl.ds(r, nseg, stride=CMP_STRIDE), :], *refs[group:-1],
                      n_pairs=n_pairs, nseg=nseg, norm=norm)


def compress_paged(pool, layer, page_table, w1e, b1e, w2e, b2e, bd, g, norm):
    _, _, W, page = pool.shape
    B, n_pages = page_table.shape
    group = PAGE_GROUP
    n_rows = n_pages * page
    nseg = n_rows // CMP_STRIDE
    _log2(nseg)
    n_pairs = W // LANES

    def page_spec(kk):
        return pl.BlockSpec((1, 1, W, page), lambda b, s, pt: (layer, pt[b, s * group + kk], 0, 0))

    consts = (w1e, b1e, w2e, b2e, bd, g)
    grid_spec = pltpu.PrefetchScalarGridSpec(
        num_scalar_prefetch=1,
        grid=(B, n_pages // group),
        in_specs=[page_spec(kk) for kk in range(group)]
        + [pl.BlockSpec(a.shape, functools.partial(lambda b, s, pt, nd: (0,) * nd, nd=a.ndim),
                        pipeline_mode=pl.Buffered(1)) for a in consts],
        out_specs=pl.BlockSpec((1, nseg, W), lambda b, s, pt: (b, 0, 0)),
        scratch_shapes=[pltpu.VMEM((n_pairs, n_rows, LANES), F32)],
    )
    return pl.pallas_call(
        functools.partial(_compress_paged_kernel, group=group, n_pairs=n_pairs, nseg=nseg, norm=norm),
        grid_spec=grid_spec,
        out_shape=jax.ShapeDtypeStruct((B, nseg, W), F32),
        compiler_params=_cparams(("parallel", "arbitrary")),
        name="compress_paged",
    )(page_table, *([pool] * group), *consts)


_REL_EXACT = REL_BUCKETS // 2
_REL_THRESHOLDS = tuple(
    math.ceil(_REL_EXACT * (REL_MAX_DIST / _REL_EXACT) ** (k / (REL_BUCKETS - _REL_EXACT)) - 1e-9)
    for k in range(1, REL_BUCKETS - _REL_EXACT))


def _bias_kernel(base_ref, tbl_ref, o_ref, *, cstep):
    t = pl.program_id(0)
    _, H, R, C = o_ref.shape
    dist = base_ref[t] + _iota((R, C), 0) - cstep * _iota((R, C), 1)
    d = jnp.maximum(dist, 0)
    big = jnp.full((R, C), _REL_EXACT, jnp.int32)
    for th in _REL_THRESHOLDS:
        big = big + jnp.where(d >= th, 1, 0)
    bucket = jnp.where(d < _REL_EXACT, d, big)
    for h in range(H):
        acc = jnp.full((R, C), tbl_ref[h, REL_BUCKETS - 1], F32)
        for u in range(REL_BUCKETS - 1):
            acc = jnp.where(bucket == u, tbl_ref[h, u], acc)
        o_ref[0, h] = acc * LOG2E


def bias_tiles(rel_table, bases, R, C, cstep):
    H = rel_table.shape[1]
    T = bases.shape[0]
    grid_spec = pltpu.PrefetchScalarGridSpec(
        num_scalar_prefetch=1,
        grid=(T,),
        in_specs=[pl.BlockSpec(memory_space=pltpu.SMEM)],
        out_specs=pl.BlockSpec((1, H, R, C), lambda t, base: (t, 0, 0, 0)),
    )
    return pl.pallas_call(
        functools.partial(_bias_kernel, cstep=cstep),
        grid_spec=grid_spec,
        out_shape=jax.ShapeDtypeStruct((T, H, R, C), F32),
        compiler_params=_cparams(("arbitrary",)),
        name="bias_tiles",
    )(bases, rel_table.T)


def _masked_softmax(s, mask):
    s = jnp.where(mask, s, NEG)
    m = jnp.max(s, axis=-1, keepdims=True)
    p = jnp.where(mask, jnp.exp2(s - m), 0.0)
    return p * (1.0 / jnp.maximum(jnp.sum(p, axis=-1, keepdims=True), 1e-30))


def _off_mask(keep):
    return jnp.where(keep, 0.0, NEG)


def _softmax_update_add(s, m, l, acc, v, feature_major=False):
    m_new = jnp.maximum(m, jnp.max(s, axis=-1, keepdims=True))
    alpha = jnp.exp2(m - m_new)
    p = jnp.exp2(s - m_new)
    l = alpha * l + jnp.sum(p, axis=-1, keepdims=True)
    pb = p.astype(BF16)
    acc = alpha * acc + (_dot_nt(pb, v) if feature_major else _dot(pb, v))
    return m_new, l, acc


def _softmax_step(s, m, acc, v1):
    m_new = jnp.maximum(m, jnp.max(s, axis=-1, keepdims=True))
    p = jnp.exp2(s - m_new)
    return m_new, jnp.exp2(m - m_new) * acc + _dot(p.astype(BF16), v1)


def _softmax_init(R, W):
    return jnp.full((R, 1), NEG, F32), jnp.zeros((R, 1), F32), jnp.zeros((R, W), F32)


def _softmax_finish(l, acc):
    return acc * (1.0 / jnp.maximum(l, 1e-30))


def _top_blocks(score, n_real, n_top, axis=1):
    jidx = _iota(score.shape, axis)
    rank = jnp.zeros(score.shape, F32)
    for i in range(n_real):
        si = score[:, i:i + 1] if axis == 1 else score[i:i + 1, :]
        beats = (si > score) | ((si == score) & (jidx > i))
        rank = rank + jnp.where(beats, 1.0, 0.0)
    return jnp.where(rank < n_top, 1.0, 0.0)


def _block_scores(imp, q_pos, n_sel, axis=1):
    j = _iota(imp.shape, axis)
    cur = q_pos >> _log2(SLC_BLOCK)
    valid = j * SLC_BLOCK <= q_pos
    forced = valid & ((j == 0) | (j == cur) | (j == cur - 1))
    score = jnp.where(forced, FORCE, jnp.where(valid, imp, -FORCE))
    return jnp.where(j < n_sel, score, -3e38)


def _block_of_key_01(n_blocks, tk, pos0):
    j = _iota((n_blocks, tk), 0)
    c = _iota((n_blocks, tk), 1)
    return jnp.where(((pos0 + c) >> _log2(SLC_BLOCK)) == j, 1.0, 0.0).astype(BF16)


def _half_to(x, src_high, dst_high):
    return x if src_high == dst_high else pltpu.roll(x, HEAD_DIM, 1)


def _nsa_prompt_kernel(q_ref, g_ref, kc_ref, vc_ref, ks_ref, vs_ref, kw_ref, vw_ref, bd_ref, bc_ref, covt_ref,
                       brange_ref, o_ref, woff_sc, knorm_sc, *, tq, n_cmp, n_sel, n_top, group):
    i = pl.program_id(2)
    G = group
    R = G * tq
    nt = tq // LANES
    JP = _round_up(n_sel, 8)
    n_win = WINDOW // tq
    low = _iota((tq, LANES), 1) < HEAD_DIM
    SHIFT_LANE = HEAD_DIM - 1

    @pl.when(i == 0)
    def _():
        dist0 = _iota((tq, tq), 0) - _iota((tq, tq), 1)
        for dd in range(n_win + 1):
            woff_sc[dd] = _off_mask((dist0 + dd * tq >= 0) & (dist0 + dd * tq < WINDOW))
        for a, ref in enumerate((ks_ref, kw_ref)):
            k2 = ref[0] * ref[0]
            low_k = _iota(k2.shape, 1) < HEAD_DIM
            for hh in range(2):
                n2 = jnp.sum(jnp.where(low_k if hh == 0 else jnp.logical_not(low_k), k2, 0.0), axis=-1, keepdims=True)
                knorm_sc[2 * a + hh] = jnp.broadcast_to(jnp.sqrt(jnp.max(n2, axis=0, keepdims=True)), (8, LANES))

    def tile_off(dd):
        return jnp.concatenate([woff_sc[dd]] * G, axis=0)

    def prepare(hh):
        own = low if hh == 0 else jnp.logical_not(low)
        pieces = []
        for g in range(G):
            x = q_ref[0, :, (hh * 2 + g // 2) * LANES:(hh * 2 + g // 2 + 1) * LANES] * QSCALE
            pieces.append(_half_to(x, g % 2 == 1, hh == 1))
        qs = jnp.concatenate([jnp.where(own, x, 0.0) for x in pieces], axis=0).astype(BF16)

        bias_c = jnp.concatenate([bc_ref[a, hh * G + g] for g in range(G) for a in range(nt)], axis=0)
        s = _dot_nt(qs, kc_ref[0].astype(BF16)) + bias_c
        rowc = _iota(s.shape, 0) & (tq - 1)
        colc = _iota(s.shape, 1)
        dist_c = i * tq + rowc - (colc * CMP_STRIDE + (CMP_BLOCK - 1))
        p_c = _masked_softmax(s, (dist_c >= 0) & (colc < n_cmp)).astype(BF16)
        o_c = _dot(p_c, vc_ref[0].astype(BF16))
        covt = covt_ref[0:JP, :]
        impt = _dot_nt(covt, p_c[0:tq])
        for g in range(1, G):
            impt = impt + _dot_nt(covt, p_c[g * tq:(g + 1) * tq])
        q_pos = i * tq + _iota(impt.shape, 1)
        selt = _top_blocks(_block_scores(impt, q_pos, n_sel, 0), n_sel, n_top, 0)
        selt = jnp.concatenate([selt, jnp.zeros((LANES - JP, tq), F32)], axis=0)
        sel = selt.T

        sel_off = _half_to(_off_mask(sel > 0.5), False, hh == 0)
        qsel = jnp.concatenate([jnp.where(own, x, sel_off) for x in pieces], axis=0).astype(BF16)

        is_shift = (_iota((tq, LANES), 1) - (HEAD_DIM if hh == 0 else 0)) == SHIFT_LANE
        qn = [jnp.sqrt(jnp.sum(jnp.where(own, x * x, 0.0), axis=-1, keepdims=True)) for x in pieces]
        kn_s, kn_w = knorm_sc[hh][0:1, 0:1], knorm_sc[2 + hh][0:1, 0:1]
        bmax, bspan = brange_ref[0], brange_ref[1]
        q_slc = jnp.concatenate([jnp.where(own, x, jnp.where(is_shift, -(n * kn_s + bmax), sel_off))
                                 for x, n in zip(pieces, qn)], axis=0).astype(BF16)
        q_win = jnp.concatenate([jnp.where(own, x, jnp.where(is_shift, -(n * kn_w + bmax), 0.0))
                                 for x, n in zip(pieces, qn)], axis=0).astype(BF16)
        span = 2.0 * jnp.max(jnp.concatenate(qn, axis=0) * jnp.maximum(kn_s, kn_w)) + bspan
        return dict(qs=qs, qsel=qsel, q_slc=q_slc, q_win=q_win, o_c=o_c, fixed_shift=span <= MAX_SHIFT_SPAN)

    def tiles(hh, P):
        qs, qsel = P['qs'], P['qsel']
        own = low if hh == 0 else jnp.logical_not(low)
        spare_block = _iota((tq, LANES), 1) - (HEAD_DIM if hh == 0 else 0)
        is_shift = spare_block == SHIFT_LANE
        key_row = _iota((tq, LANES), 0)

        def values(ref, start):
            return jnp.where(own, ref[0, pl.ds(start, tq), :], 1.0).astype(BF16)

        def dense_bias(d):
            blocks = []
            for g in range(G):
                for a in range(nt):
                    blocks.append(jnp.concatenate(
                        [bd_ref[jnp.maximum(nt * d + a - b, 0), hh * G + g] for b in range(nt)], axis=1))
            return jnp.concatenate(blocks, axis=0)

        def slc_tile(j, carry, diag):
            start = pl.multiple_of(j * tq, tq)
            in_block = ((start + key_row) >> _log2(SLC_BLOCK)) == spare_block
            kt = jnp.where(own, ks_ref[0, pl.ds(start, tq), :], jnp.where(in_block, 1.0, 0.0)).astype(BF16)
            s = _dot_nt(qsel, kt) + dense_bias(i - j)
            if diag:
                s = s + tile_off(0)
            return _softmax_step(s, *carry, values(vs_ref, start))

        def win_tile(dd, carry):
            start = pl.multiple_of((i - dd) * tq, tq)
            kt = kw_ref[0, pl.ds(start, tq), :].astype(BF16)
            s = _dot_nt(qs, kt) + dense_bias(dd) + tile_off(dd)
            return _softmax_step(s, *carry, values(vw_ref, start))

        def running_max():
            init = (jnp.full((R, 1), NEG, F32), jnp.zeros((R, LANES), F32))
            carry = lax.fori_loop(0, i, lambda j, c: slc_tile(j, c, False), init)
            acc_s = lax.fori_loop(i, i + 1, lambda j, c: slc_tile(j, c, True), carry)[1]
            return acc_s, lax.fori_loop(0, jnp.minimum(i, n_win) + 1, win_tile, init)[1]

        def slc_shifted(j, acc, diag):
            start = pl.multiple_of(j * tq, tq)
            in_block = ((start + key_row) >> _log2(SLC_BLOCK)) == spare_block
            kt = jnp.where(own, ks_ref[0, pl.ds(start, tq), :],
                           jnp.where(in_block | is_shift, 1.0, 0.0)).astype(BF16)
            s = _dot_nt(P['q_slc'], kt) + dense_bias(i - j)
            if diag:
                s = s + tile_off(0)
            return acc + _dot(jnp.exp2(s).astype(BF16), values(vs_ref, start))

        def win_shifted(dd, acc):
            start = pl.multiple_of((i - dd) * tq, tq)
            kt = jnp.where(own, kw_ref[0, pl.ds(start, tq), :], jnp.where(is_shift, 1.0, 0.0)).astype(BF16)
            s = _dot_nt(P['q_win'], kt) + dense_bias(dd) + tile_off(dd)
            return acc + _dot(jnp.exp2(s).astype(BF16), values(vw_ref, start))

        return running_max, slc_shifted, win_shifted

    def attend(prepared):
        (run0, slc0, win0), (run1, slc1, win1) = [tiles(hh, prepared[hh]) for hh in range(2)]

        def fixed_shift():
            zero = jnp.zeros((R, LANES), F32)
            acc = lax.fori_loop(0, i, lambda j, a: (slc0(j, a[0], False), slc1(j, a[1], False)), (zero, zero))
            acc = lax.fori_loop(i, i + 1, lambda j, a: (slc0(j, a[0], True), slc1(j, a[1], True)), acc)
            win = lax.fori_loop(0, jnp.minimum(i, n_win) + 1, lambda dd, a: (win0(dd, a[0]), win1(dd, a[1])),
                                (zero, zero))
            return (acc[0], win[0]), (acc[1], win[1])

        both = prepared[0]['fixed_shift'] & prepared[1]['fixed_shift']
        return lax.cond(both, fixed_shift, lambda: (run0(), run1()))

    def finish(hh, o_c, acc_s, acc_w):
        def gate_col(br):
            return jnp.concatenate([g_ref[0, 0, :, hh * 3 * G + g * 3 + br:hh * 3 * G + g * 3 + br + 1]
                                    for g in range(G)], axis=0)

        def weight_col(br, acc):
            c = HEAD_DIM if hh == 0 else 0
            return gate_col(br) * (1.0 / jnp.maximum(acc[:, c:c + 1], 1e-30))

        o = gate_col(0) * o_c + weight_col(1, acc_s) * acc_s + weight_col(2, acc_w) * acc_w
        for vc in range(G // 2):
            lo_piece = _half_to(o[(2 * vc) * tq:(2 * vc + 1) * tq], hh == 1, False)
            hi_piece = _half_to(o[(2 * vc + 1) * tq:(2 * vc + 2) * tq], hh == 1, True)
            o_ref[0, :, (hh * 2 + vc) * LANES:(hh * 2 + vc + 1) * LANES] = jnp.where(low, lo_piece, hi_piece)

    prepared = [prepare(hh) for hh in range(2)]
    attended = attend(prepared)
    for hh in range(2):
        finish(hh, prepared[hh]['o_c'], *attended[hh])


def nsa_prompt(q, gates, kcmp, vcmp, ks, vs, kw, vw, bias_dense, bias_cmp, cover, bias_range, n_cmp, n_sel):
    B, L, WQ = q.shape
    WK = ks.shape[2]
    group = WQ // WK
    tq = 256
    nt = tq // LANES
    assert n_sel < HEAD_DIM
    n_pairs = WK // LANES
    n_top = min(SLC_TOPN, n_sel)
    qw = WQ // n_pairs
    hp = bias_dense.shape[1] // n_pairs
    kv = pl.BlockSpec((1, L, LANES), lambda b, p, i: (b, 0, p))
    cm = pl.BlockSpec((1, kcmp.shape[1], LANES), lambda b, p, i: (b, 0, p))
    return pl.pallas_call(
        functools.partial(_nsa_prompt_kernel, tq=tq, n_cmp=n_cmp, n_sel=n_sel, n_top=n_top, group=group),
        grid=(B, n_pairs, L // tq),
        in_specs=[
            pl.BlockSpec((1, tq, qw), lambda b, p, i: (b, i, p)),
            pl.BlockSpec((1, 1, tq, LANES), lambda b, p, i: (p, b, i, 0)),
            cm, cm, kv, kv, kv, kv,
            pl.BlockSpec((bias_dense.shape[0], hp, LANES, LANES), lambda b, p, i: (0, p, 0, 0)),
            pl.BlockSpec((nt, hp, LANES, LANES), lambda b, p, i: (i, p, 0, 0)),
            _const_spec(cover.shape),
            pl.BlockSpec(memory_space=pltpu.SMEM),
        ],
        out_specs=pl.BlockSpec((1, tq, qw), lambda b, p, i: (b, i, p)),
        out_shape=jax.ShapeDtypeStruct(q.shape, F32),
        scratch_shapes=[pltpu.VMEM((WINDOW // tq + 1, tq, tq), F32), pltpu.VMEM((4, 8, LANES), F32)],
        compiler_params=_cparams(("parallel", "parallel", "arbitrary")),
        name="nsa_prompt",
    )(q, gates, kcmp, vcmp, ks, vs, kw, vw, bias_dense, bias_cmp, cover, bias_range)


def _nsa_sample_kernel(pt_ref, q_ref, g_ref, kc_ref, vc_ref, ksn_ref, vsn_ref, kwp_ref, vwp_ref,
                       kwn_ref, vwn_ref, bs_ref, bn_ref, bc_ref, bw_ref, cov_ref, e_ref, *rest,
                       n_tok, q_base, n_cmp, n_sel, n_top, group, pages):
    k_refs, v_refs = rest[:pages], rest[pages:2 * pages]
    o_ref, qs_sc, sel_sc, oc_sc, m_sc, l_sc, acc_sc = rest[2 * pages:]
    ci = pl.program_id(1)
    R, W = qs_sc.shape
    n_heads = R // n_tok
    J = sel_sc.shape[1]

    def tok(shape):
        return _iota(shape, 0) & (n_tok - 1)

    @pl.when(ci == 0)
    def _():
        blocks = []
        for h in range(n_heads):
            hkv = h // group
            x = q_ref[0, :, (h // 2) * LANES:(h // 2 + 1) * LANES] * QSCALE
            x = _half_to(x, h % 2 == 1, hkv % 2 == 1)
            keep = (_iota(x.shape, 1) >= HEAD_DIM) if hkv % 2 == 1 else (_iota(x.shape, 1) < HEAD_DIM)
            x = jnp.where(keep, x, 0.0)
            blocks.append(jnp.concatenate(
                [x if cc == hkv // 2 else jnp.zeros_like(x) for cc in range(W // LANES)], axis=1))
        qs = jnp.concatenate(blocks, axis=0).astype(BF16)
        qs_sc[...] = qs
        s = _dot_nt(qs, kc_ref[0].astype(BF16)) + bc_ref[...]
        n = _iota(s.shape, 1)
        dist_c = q_base + tok(s.shape) - (n * CMP_STRIDE + (CMP_BLOCK - 1))
        p_c = _masked_softmax(s, (dist_c >= 0) & (n < n_cmp)).astype(BF16)
        oc_sc[...] = _dot(p_c, vc_ref[0].astype(BF16))
        ic = _dot(p_c, cov_ref[...])
        rows = group * n_tok
        imp_blocks = []
        for hkv in range(n_heads // group):
            a = ic[hkv * rows:hkv * rows + n_tok]
            for g in range(1, group):
                a = a + ic[hkv * rows + g * n_tok:hkv * rows + (g + 1) * n_tok]
            imp_blocks += [a] * group
        imp = jnp.concatenate(imp_blocks, axis=0)
        sel_sc[...] = _top_blocks(_block_scores(imp, q_base + tok(imp.shape), n_sel), n_sel, n_top)
        m0, l0, a0 = _softmax_init(R, W)
        m_sc[...] = m0
        l_sc[...] = l0
        acc_sc[...] = a0

    qs = qs_sc[...]
    sel = sel_sc[...].astype(BF16)

    kt = jnp.concatenate([r[0, 0] for r in k_refs], axis=1).astype(BF16)
    vt = jnp.concatenate([r[0, 0] for r in v_refs], axis=1).astype(BF16)
    s = _dot(qs, kt) + bs_ref[...] + _off_mask(_dot(sel, e_ref[...]) > 0.5)
    m, l, acc = _softmax_update_add(s, m_sc[...], l_sc[...], acc_sc[...], vt, True)
    m_sc[...] = m
    l_sc[...] = l
    acc_sc[...] = acc

    @pl.when(ci == pl.num_programs(1) - 1)
    def _():
        tn = ksn_ref.shape[1]
        c = _iota((R, tn), 1)
        causal_new = c <= tok((R, tn))
        s = _dot_nt(qs, ksn_ref[0].astype(BF16)) + bn_ref[...]
        chosen = _dot(sel, _block_of_key_01(J, tn, q_base))
        _, l1, a1 = _softmax_update_add(s + _off_mask((chosen > 0.5) & causal_new), m, l, acc,
                                        vsn_ref[0].astype(BF16))
        o_s = _softmax_finish(l1, a1)
        s = _dot(qs, kwp_ref[0, 0].astype(BF16)) + bw_ref[...]
        dist = WINDOW + tok(s.shape) - _iota(s.shape, 1)
        cw = _softmax_update_add(s + _off_mask((dist >= 0) & (dist < WINDOW)), *_softmax_init(R, W),
                                 vwp_ref[0, 0].astype(BF16), True)
        s = _dot_nt(qs, kwn_ref[0].astype(BF16)) + bn_ref[...]
        _, l2, a2 = _softmax_update_add(s + _off_mask(causal_new), *cw, vwn_ref[0].astype(BF16))
        o_w = _softmax_finish(l2, a2)

        def gate_col(br):
            cols = []
            for h in range(n_heads):
                hkv, g = divmod(h, group)
                cidx = (hkv % 2) * 3 * group + g * 3 + br
                cols.append(g_ref[hkv // 2, 0, :, cidx:cidx + 1])
            return jnp.concatenate(cols, axis=0)

        o = gate_col(0) * oc_sc[...] + gate_col(1) * o_s + gate_col(2) * o_w
        low = _iota((n_tok, LANES), 1) < HEAD_DIM
        for oc in range(n_heads // 2):
            hkv = (2 * oc) // group
            src = slice((hkv // 2) * LANES, (hkv // 2 + 1) * LANES)
            lo_piece = _half_to(o[(2 * oc) * n_tok:(2 * oc + 1) * n_tok, src], hkv % 2 == 1, False)
            hi_piece = _half_to(o[(2 * oc + 1) * n_tok:(2 * oc + 2) * n_tok, src], hkv % 2 == 1, True)
            o_ref[0, :, oc * LANES:(oc + 1) * LANES] = jnp.where(low, lo_piece, hi_piece)


def nsa_sample(q, gates, kcmp, vcmp, pool_k, pool_v, win_k, win_v, layer, page_table, ks_new, vs_new,
               kw_new, vw_new, bias_slc, bias_new, bias_cmp, bias_win, cover, n_cmp, n_sel):
    B, n_tok, WQ = q.shape
    WK, page = pool_k.shape[2], pool_k.shape[3]
    n_pages = page_table.shape[1]
    past = n_pages * page
    group = WQ // WK
    n_heads = WQ // HEAD_DIM
    R = n_heads * n_tok
    J = cover.shape[1]
    pages = PAGE_GROUP
    chunk = pages * page
    block_of_key = (jnp.arange(past)[None, :] // SLC_BLOCK == jnp.arange(J)[:, None]).astype(BF16)

    def bspec(a):
        return pl.BlockSpec((1,) + a.shape[1:], lambda b, c, pt: (b,) + (0,) * (a.ndim - 1))

    def cspec(a):
        return pl.BlockSpec(a.shape, lambda b, c, pt: (0,) * a.ndim, pipeline_mode=pl.Buffered(1))

    def page_spec(kk):
        return pl.BlockSpec((1, 1, WK, page), lambda b, c, pt: (layer, pt[b, c * pages + kk], 0, 0))

    wspec = pl.BlockSpec((1, 1, WK, win_k.shape[3]), lambda b, c, pt: (layer, b, 0, 0))
    grid_spec = pltpu.PrefetchScalarGridSpec(
        num_scalar_prefetch=1,
        grid=(B, n_pages // pages),
        in_specs=[
            bspec(q),
            pl.BlockSpec((gates.shape[0], 1, n_tok, LANES), lambda b, c, pt: (0, b, 0, 0)),
            bspec(kcmp), bspec(vcmp), bspec(ks_new), bspec(vs_new), wspec, wspec, bspec(kw_new), bspec(vw_new),
            pl.BlockSpec((R, chunk), lambda b, c, pt: (0, c)),
            cspec(bias_new), cspec(bias_cmp), cspec(bias_win), cspec(cover),
            pl.BlockSpec((J, chunk), lambda b, c, pt: (0, c)),
        ] + [page_spec(kk) for kk in range(pages)] * 2,
        out_specs=bspec(q),
        scratch_shapes=[pltpu.VMEM((R, WK), BF16), pltpu.VMEM((R, J), F32), pltpu.VMEM((R, WK), F32),
                        pltpu.VMEM((R, 1), F32), pltpu.VMEM((R, 1), F32), pltpu.VMEM((R, WK), F32)],
    )
    return pl.pallas_call(
        functools.partial(_nsa_sample_kernel, n_tok=n_tok, q_base=past, n_cmp=n_cmp, n_sel=n_sel,
                          n_top=min(SLC_TOPN, n_sel), group=group, pages=pages),
        grid_spec=grid_spec,
        out_shape=jax.ShapeDtypeStruct(q.shape, F32),
        compiler_params=_cparams(("parallel", "arbitrary")),
        name="nsa_sample",
    )(page_table, q, gates, kcmp, vcmp, ks_new, vs_new, win_k, win_v, kw_new, vw_new,
      bias_slc, bias_new, bias_cmp, bias_win, cover, block_of_key, *([pool_k] * pages), *([pool_v] * pages))


def _block_diag_ones(width):
    h = jnp.arange(width) // HEAD_DIM
    return (h[:, None] == h[None, :]).astype(BF16)


def _cover_01(n_cmp, n_sel, rows, cols):
    ci = jnp.arange(rows) * CMP_STRIDE
    sj = jnp.arange(cols) * SLC_BLOCK
    hit = (ci[:, None] < sj[None, :] + SLC_BLOCK) & (ci[:, None] + CMP_BLOCK > sj[None, :])
    hit = hit & (jnp.arange(rows)[:, None] < n_cmp) & (jnp.arange(cols)[None, :] < n_sel)
    return hit.astype(BF16)


def _round_up(n, m):
    return -(-n // m) * m


def _compress_weights(w1, b1, w2, b2, n_kv):
    hidden = w1.shape[1]
    eye = jnp.eye(n_kv, dtype=F32)
    w1r = w1.reshape(CMP_BLOCK // CMP_STRIDE, CMP_STRIDE, HEAD_DIM, hidden)
    rk = 2
    w1e = jnp.einsum('mrde,hk->mrhdke', w1r, eye).reshape(
        CMP_BLOCK // CMP_STRIDE, CMP_STRIDE // rk, rk * n_kv * HEAD_DIM, n_kv * hidden).astype(BF16)
    w2e = jnp.einsum('ed,hk->hekd', w2, eye).reshape(n_kv * hidden, n_kv * HEAD_DIM).astype(BF16)
    return w1e, jnp.tile(b1, n_kv)[None, :], w2e, jnp.tile(b2, n_kv)[None, :]


def _trunk(x, mods, past, P):
    B, L, D = x.shape
    depth = P['norm_g'].shape[0]
    even_states, odd_states = [], []
    for li in range(depth):
        def m(k):
            return mods[li, :, k][:, None, :]

        def g(k):
            return P['norm_g'][li, k][None, :]

        x = ffn(x, m(0), m(1), m(2), g(0), P['ffn_w_in'][li, 0], P['ffn_w_out'][li, 0])
        if li % 2 == 0:
            e = li // 2
            w = P['even_w_in'][e]
            n_sb = P['sb_width']
            HK, HV = P['gla_hk'], P['gla_hv']
            offs = [0, n_sb, 2 * n_sb, 3 * n_sb, 3 * n_sb + HK, 3 * n_sb + 2 * HK, 3 * n_sb + 2 * HK + HV,
                    3 * n_sb + 2 * HK + 2 * HV]
            ws = [w[:, offs[k]:offs[k + 1]] for k in range(7)]
            rank = w.shape[1] - offs[7]
            wa1 = jnp.pad(w[:, offs[7]:], ((0, 0), (0, LANES - rank)))
            wa2 = jnp.pad(P['gla_w_a2'][e], ((0, LANES - rank), (0, 0)))
            qa, ka, va, qb, kb, vb, rb, la = even_proj(x, m(3), m(4), g(1), ws, wa1, wa2, P['gla_b_a2'][e][None, :])
            dv = P['gla_onorm_g'].shape[1]
            n_gh = HV // dv
            dk = HK // n_gh
            if past is None:
                o_a = sb_prompt(qa, ka, va)
                s0t = jnp.zeros((B, dv, HK), F32)
            else:
                o_a = sb_sample(qa, ka, va, past['sb_k'], past['sb_v'], e, past['page_table'])
                s0t = past['gla'][e].transpose(0, 3, 1, 2).reshape(B, dv, HK)
            o_b, st = gla(qb, kb, vb, rb, la, s0t, P['gla_onorm_g'][e][None, :])
            wo = P['even_w_out'][e]
            mix_acts, mix_ws = [o_a, o_b], [wo[:n_sb], wo[n_sb:]]
            s_fin = st.reshape(B, dv, n_gh, dk).transpose(0, 2, 3, 1)
            even_states.append((ka.reshape(B, L, -1, HEAD_DIM), va.reshape(B, L, -1, HEAD_DIM), s_fin))
        else:
            o_i = li // 2
            w = P['odd_w_in'][o_i]
            WQ, WK = P['nsa_wq'], P['nsa_wk']
            n_kv = WK // HEAD_DIM
            group = WQ // WK
            offs = [0, WQ] + [WQ + (k + 1) * WK for k in range(6)]
            ws = [w[:, offs[k]:offs[k + 1]] for k in range(7)]
            wg = w[:, offs[7]:]
            half = wg.shape[1] // 2
            ws += [jnp.pad(wg[:, :half], ((0, 0), (0, LANES - half))),
                   jnp.pad(wg[:, half:], ((0, 0), (0, LANES - half)))]
            qk_g = P['nsa_qk_g'][o_i]
            consts = [_block_diag_ones(WQ), _block_diag_ones(WK), jnp.tile(qk_g[0], WQ // HEAD_DIM)[None, :],
                      jnp.tile(qk_g[2], n_kv)[None, :], jnp.tile(qk_g[3], n_kv)[None, :]]
            q, kc, vc, ks, vs, kw, vw, gates = odd_proj(x, m(3), m(4), g(1), ws, consts)
            pair = LANES // HEAD_DIM
            gk = jnp.tile(qk_g[1], pair)[None, :]
            bdp = _block_diag_ones(LANES)
            cw = [_compress_weights(P['cmp_w1'][o_i, t], P['cmp_b1'][o_i, t], P['cmp_w2'][o_i, t],
                                    P['cmp_b2'][o_i, t], pair) for t in range(2)]
            rel = P['rel_table']
            if past is None:
                n_rows = (L // CMP_STRIDE) * CMP_STRIDE
                assert n_rows == L and L % LANES == 0
                n_cmp = L // CMP_STRIDE - CMP_BLOCK // CMP_STRIDE + 1
                n_sel = -(-L // SLC_BLOCK)
                kcmp = compress(kc, n_rows, *cw[0], bdp, gk, True)
                vcmp = compress(vc, n_rows, *cw[1], bdp, gk, False)
                assert kcmp.shape[1] == LANES and n_sel <= LANES
                nq = L // LANES
                bias_dense = bias_tiles(rel, jnp.arange(nq, dtype=jnp.int32) * LANES, LANES, LANES, 1)
                bias_cmp = bias_tiles(rel, jnp.arange(nq, dtype=jnp.int32) * LANES - (CMP_BLOCK - 1),
                                      LANES, LANES, CMP_STRIDE)
                cover = _cover_01(n_cmp, n_sel, LANES, LANES).T
                bias_range = jnp.stack([jnp.max(rel), jnp.max(rel) - jnp.min(rel)]) * LOG2E
                o = nsa_prompt(q, gates, kcmp, vcmp, ks, vs, kw, vw, bias_dense, bias_cmp, cover, bias_range,
                               n_cmp, n_sel)
                w_keep = min(WINDOW, L)
                win_k, win_v = kw[:, L - w_keep:], vw[:, L - w_keep:]
            else:
                pt = past['page_table']
                past_len = pt.shape[1] * past['cmp_k'].shape[3]
                T = past_len + L
                n_rows = (T // CMP_STRIDE) * CMP_STRIDE
                assert n_rows == past_len and past['win_k'].shape[3] == WINDOW and L <= LANES
                n_cmp = n_rows // CMP_STRIDE - CMP_BLOCK // CMP_STRIDE + 1
                n_sel = -(-T // SLC_BLOCK)
                kcmp = compress_paged(past['cmp_k'], o_i, pt, *cw[0], bdp, gk, True)
                vcmp = compress_paged(past['cmp_v'], o_i, pt, *cw[1], bdp, gk, False)
                J = _round_up(n_sel, LANES)
                n_heads = WQ // HEAD_DIM
                R = n_heads * L

                def sample_bias(base, C, cstep):
                    return bias_tiles(rel, jnp.array([base], jnp.int32), L, C, cstep).reshape(R, C)

                bias_slc = sample_bias(past_len, past_len, 1)
                bias_new = sample_bias(0, LANES, 1)
                bias_cmp = sample_bias(past_len - (CMP_BLOCK - 1), kcmp.shape[1], CMP_STRIDE)
                bias_win = sample_bias(WINDOW, WINDOW, 1)
                cover = _cover_01(n_cmp, n_sel, kcmp.shape[1], J)
                pad = ((0, 0), (0, LANES - L), (0, 0))
                o = nsa_sample(q, gates, kcmp, vcmp, past['slc_k'], past['slc_v'], past['win_k'], past['win_v'],
                               o_i, pt, jnp.pad(ks, pad), jnp.pad(vs, pad), jnp.pad(kw, pad), jnp.pad(vw, pad),
                               bias_slc, bias_new, bias_cmp, bias_win, cover, n_cmp, n_sel)
                win_k = jnp.concatenate([past['win_k_rows'][o_i], kw], axis=1)[:, L:]
                win_v = jnp.concatenate([past['win_v_rows'][o_i], vw], axis=1)[:, L:]
            mix_acts, mix_ws = [o], [P['odd_w_out'][o_i]]

            def rows(t):
                return t.reshape(B, t.shape[1], n_kv, HEAD_DIM)

            odd_states.append(tuple(rows(t) for t in (kc, vc, ks, vs, win_k, win_v)))
        x = ffn(x, m(6), m(7), m(8), g(2), P['ffn_w_in'][li, 1], P['ffn_w_out'][li, 1], m(5), mix_acts, mix_ws)
    even_new = [jnp.stack([s[i] for s in even_states]) for i in range(3)]
    odd_new = [jnp.stack([s[i] for s in odd_states]) for i in range(6)]
    return x, even_new, odd_new


def kernel(x_prompt, x_sample, cache_sb_k, cache_sb_v, state_gla, cache_cmp_k, cache_cmp_v, cache_slc_k,
           cache_slc_v, cache_win_k, cache_win_v, page_table, c_prompt, c_sample, norm_g, ada_w, ada_b,
           ffn_w_in, ffn_w_out, even_w_in, gla_w_a2, gla_b_a2, gla_onorm_g, even_w_out, odd_w_in, nsa_qk_g,
           cmp_w1, cmp_b1, cmp_w2, cmp_b2, rel_table, odd_w_out):
    D = x_prompt.shape[-1]
    depth = norm_g.shape[0]
    Bp, Bs = x_prompt.shape[0], x_sample.shape[0]
    mods = ada_mod(jnp.concatenate([c_prompt, c_sample], axis=0), ada_w, ada_b).reshape(depth, Bp + Bs, 9, D)
    n_sb = cache_sb_k.shape[3] * cache_sb_k.shape[4]
    dv = gla_onorm_g.shape[1]
    HK = gla_w_a2.shape[2]
    n_gh = state_gla.shape[2]
    WK = cache_cmp_k.shape[3] * cache_cmp_k.shape[4]
    P = {
        'norm_g': norm_g, 'ffn_w_in': ffn_w_in.astype(BF16), 'ffn_w_out': ffn_w_out.astype(BF16),
        'even_w_in': even_w_in.astype(BF16), 'gla_w_a2': gla_w_a2.astype(BF16), 'gla_b_a2': gla_b_a2,
        'gla_onorm_g': gla_onorm_g, 'even_w_out': even_w_out.astype(BF16), 'odd_w_in': odd_w_in.astype(BF16),
        'nsa_qk_g': nsa_qk_g, 'cmp_w1': cmp_w1, 'cmp_b1': cmp_b1, 'cmp_w2': cmp_w2, 'cmp_b2': cmp_b2,
        'rel_table': rel_table, 'odd_w_out': odd_w_out.astype(BF16),
        'sb_width': n_sb, 'gla_hk': HK, 'gla_hv': n_gh * dv, 'nsa_wq': odd_w_out.shape[1], 'nsa_wk': WK,
    }

    def pool(c):
        return c.transpose(0, 1, 3, 4, 2).reshape(c.shape[0], c.shape[1], c.shape[3] * c.shape[4], c.shape[2])

    def rows(c):
        return c.reshape(c.shape[0], c.shape[1], c.shape[2], c.shape[3] * c.shape[4])

    past = {
        'page_table': page_table, 'sb_k': pool(cache_sb_k), 'sb_v': pool(cache_sb_v), 'gla': state_gla,
        'cmp_k': pool(cache_cmp_k), 'cmp_v': pool(cache_cmp_v), 'slc_k': pool(cache_slc_k),
        'slc_v': pool(cache_slc_v), 'win_k': pool(cache_win_k), 'win_v': pool(cache_win_v),
        'win_k_rows': rows(cache_win_k), 'win_v_rows': rows(cache_win_v),
    }
    y_p, p_even, p_odd = _trunk(x_prompt, mods[:, :Bp], None, P)
    y_s, s_even, s_odd = _trunk(x_sample, mods[:, Bp:], past, P)
    return (y_p, y_s, *p_even, *p_odd, *s_even, *s_odd)
```

```python
import functools
import math

import jax
import jax.numpy as jnp
from jax import lax
from jax.experimental import pallas as pl
from jax.experimental.pallas import tpu as pltpu

F32 = jnp.float32
BF16 = jnp.bfloat16

EPS = 1e-6
NEG = -1e30
FORCE = 1e6
HEAD_DIM = 64
GLA_TAU = 16.0
GLA_CHUNK = 64
CMP_BLOCK = 32
CMP_STRIDE = 16
SLC_BLOCK = 64
SLC_TOPN = 16
WINDOW = 512
REL_BUCKETS = 32
REL_MAX_DIST = 1024
PAGE_GROUP = 16
LANES = 128
VMEM_LIMIT = 56 * 1024 * 1024


def _cparams(sem, vmem=VMEM_LIMIT):
    return pltpu.CompilerParams(dimension_semantics=sem, vmem_limit_bytes=vmem)


def _dot(a, b):
    return jnp.dot(a, b, preferred_element_type=F32)


def _dot_nt(a, b):
    return lax.dot_general(a, b, (((1,), (1,)), ((), ())), preferred_element_type=F32)


def _dot_tn(a, b):
    return lax.dot_general(a, b, (((0,), (0,)), ((), ())), preferred_element_type=F32)


def _split2(x):
    hi = x.astype(BF16)
    lo = (x - hi.astype(F32)).astype(BF16)
    return hi, lo


def _dot_f32_by01(x, m01):
    hi, lo = _split2(x)
    return _dot(hi, m01) + _dot(lo, m01)


def _dot_01_by_f32(m01, x):
    hi, lo = _split2(x)
    return _dot(m01, hi) + _dot(m01, lo)


def _softplus(z):
    return jnp.maximum(z, 0.0) + jnp.log1p(jnp.exp(-jnp.abs(z)))


LOG2E = 1.4426950408889634
QSCALE = HEAD_DIM ** -0.5 * LOG2E
MAX_SHIFT_SPAN = 100.0


def _sb_logs2(z2):
    l2 = jnp.log(1.0 + jnp.exp2(-jnp.abs(z2))) * LOG2E
    lk = -(jnp.maximum(z2, 0.0) + l2)
    return lk, z2 + lk


def _silu(a):
    return a * jax.nn.sigmoid(a)


def _iota(shape, dim):
    return lax.broadcasted_iota(jnp.int32, shape, dim)


def _modulate(x, g, shift, scale):
    ms = jnp.mean(x * x, axis=-1, keepdims=True)
    y = x * lax.rsqrt(ms + EPS) * g
    return y * (1.0 + scale) + shift


def _head_rms(x, bd, g):
    ms = _dot_f32_by01(x * x, bd) * (1.0 / HEAD_DIM)
    return x * lax.rsqrt(ms + EPS) * g


def _const_spec(shape):
    nd = len(shape)
    return pl.BlockSpec(shape, lambda *_: (0,) * nd, pipeline_mode=pl.Buffered(1))


def _row_tiles(B, L):
    if L >= 512:
        return 1, 512
    return B, L


def _ada_kernel(c_ref, w_ref, b_ref, o_ref):
    c = c_ref[...]
    a = _silu(c).astype(BF16)
    o_ref[0] = _dot(a, w_ref[0].astype(BF16)) + b_ref[0]


def ada_mod(c, ada_w, ada_b):
    R, D = c.shape
    depth, _, N = ada_w.shape
    tn = 1024
    return pl.pallas_call(
        _ada_kernel,
        grid=(depth, N // tn),
        in_specs=[
            pl.BlockSpec((R, D), lambda l, n: (0, 0)),
            pl.BlockSpec((1, D, tn), lambda l, n: (l, 0, n)),
            pl.BlockSpec((1, 1, tn), lambda l, n: (l, 0, n)),
        ],
        out_specs=pl.BlockSpec((1, R, tn), lambda l, n: (l, 0, n)),
        out_shape=jax.ShapeDtypeStruct((depth, R, N), F32),
        compiler_params=_cparams(("parallel", "parallel")),
        name="ada_mod",
    )(c, ada_w, ada_b.reshape(depth, 1, N))


def _ffn_kernel(x_ref, sh_ref, sc_ref, gt_ref, g_ref, win_ref, wout_ref, *rest, d_ff, tf, n_mix):
    o_ref = rest[-1]
    x = x_ref[...]
    tb, tl, D = x.shape
    if n_mix:
        mix = jnp.zeros((tb * tl, D), F32)
        for a, w in zip(rest[1:1 + n_mix], rest[1 + n_mix:1 + 2 * n_mix]):
            av = a[...]
            mix = mix + _dot(av.reshape(tb * tl, av.shape[-1]).astype(BF16), w[...])
        x = x + rest[0][...] * mix.reshape(tb, tl, D)
    h = _modulate(x, g_ref[...], sh_ref[...], sc_ref[...])
    hb = h.reshape(tb * tl, D).astype(BF16)
    acc = jnp.zeros((tb * tl, D), F32)
    for j in range(d_ff // tf):
        a = _dot(hb, win_ref[:, j * tf:(j + 1) * tf])
        b = _dot(hb, win_ref[:, d_ff + j * tf:d_ff + (j + 1) * tf])
        u = (_silu(a) * b).astype(BF16)
        acc = acc + _dot(u, wout_ref[j * tf:(j + 1) * tf, :])
    o_ref[...] = x + 0.5 * gt_ref[...] * acc.reshape(tb, tl, D)


def ffn(x, shift, scale, gate, g, w_in, w_out, mix_gate=None, mix_acts=(), mix_ws=()):
    B, L, D = x.shape
    d_ff = w_out.shape[0]
    tb, tl = _row_tiles(B, L)
    xs = pl.BlockSpec((tb, tl, D), lambda b, l: (b, l, 0))
    ms = pl.BlockSpec((tb, 1, D), lambda b, l: (b, 0, 0))
    n_mix = len(mix_acts)
    mix_specs = []
    if n_mix:
        mix_specs = [ms] + [pl.BlockSpec((tb, tl, a.shape[-1]), lambda b, l: (b, l, 0)) for a in mix_acts] \
            + [_const_spec(w.shape) for w in mix_ws]
    return pl.pallas_call(
        functools.partial(_ffn_kernel, d_ff=d_ff, tf=256, n_mix=n_mix),
        grid=(B // tb, L // tl),
        in_specs=[xs, ms, ms, ms, _const_spec((1, D)), _const_spec(w_in.shape), _const_spec(w_out.shape)]
        + mix_specs,
        out_specs=xs,
        out_shape=jax.ShapeDtypeStruct(x.shape, F32),
        compiler_params=_cparams(("parallel", "parallel")),
        name="ffn",
    )(x, shift, scale, gate, g, w_in, w_out, *(([mix_gate] if n_mix else []) + list(mix_acts) + list(mix_ws)))


def _even_proj_kernel(x_ref, sh_ref, sc_ref, g_ref, wqa, wka, wva, wqb, wkb, wvb, wrb, wa1, wa2, ba2,
                      qa, ka, va, qb, kb, vb, rb, la):
    x = x_ref[...]
    tb, tl, D = x.shape
    h = _modulate(x, g_ref[...], sh_ref[...], sc_ref[...])
    hb = h.reshape(tb * tl, D).astype(BF16)
    for w, o in ((wqa, qa), (wka, ka), (wva, va), (wqb, qb), (wkb, kb), (wvb, vb), (wrb, rb)):
        o[...] = _dot(hb, w[...]).reshape(o.shape)
    a1 = _dot(hb, wa1[...]).astype(BF16)
    z = _dot(a1, wa2[...]) + ba2[...]
    la[...] = (-_softplus(-z) * (1.0 / GLA_TAU)).reshape(la.shape)


def even_proj(x, shift, scale, g, ws, wa1, wa2, ba2):
    B, L, D = x.shape
    tb, tl = _row_tiles(B, L)
    xs = pl.BlockSpec((tb, tl, D), lambda b, l: (b, l, 0))
    ms = pl.BlockSpec((tb, 1, D), lambda b, l: (b, 0, 0))
    widths = [w.shape[1] for w in ws] + [wa2.shape[1]]
    return pl.pallas_call(
        _even_proj_kernel,
        grid=(B // tb, L // tl),
        in_specs=[xs, ms, ms, _const_spec((1, D))] + [_const_spec(w.shape) for w in ws]
        + [_const_spec(wa1.shape), _const_spec(wa2.shape), _const_spec(ba2.shape)],
        out_specs=[pl.BlockSpec((tb, tl, n), lambda b, l: (b, l, 0)) for n in widths],
        out_shape=[jax.ShapeDtypeStruct((B, L, n), F32) for n in widths],
        compiler_params=_cparams(("parallel", "parallel")),
        name="even_proj",
    )(x, shift, scale, g, *ws, wa1, wa2, ba2)


def _odd_proj_kernel(x_ref, sh_ref, sc_ref, g_ref, wq, wkc, wvc, wks, wvs, wkw, wvw, wg0, wg1,
                     hq, hk, gq, gks, gkw,
                     q, kc, vc, ks, vs, kw, vw, gates):
    x = x_ref[...]
    tb, tl, D = x.shape
    h = _modulate(x, g_ref[...], sh_ref[...], sc_ref[...])
    hb = h.reshape(tb * tl, D).astype(BF16)
    q[...] = _head_rms(_dot(hb, wq[...]), hq[...], gq[...]).reshape(q.shape)
    kc[...] = _dot(hb, wkc[...]).reshape(kc.shape)
    vc[...] = _dot(hb, wvc[...]).reshape(vc.shape)
    ks[...] = _head_rms(_dot(hb, wks[...]), hk[...], gks[...]).reshape(ks.shape)
    vs[...] = _dot(hb, wvs[...]).reshape(vs.shape)
    kw[...] = _head_rms(_dot(hb, wkw[...]), hk[...], gkw[...]).reshape(kw.shape)
    vw[...] = _dot(hb, wvw[...]).reshape(vw.shape)
    gates[0] = jax.nn.sigmoid(_dot(hb, wg0[...])).reshape(gates.shape[1:])
    gates[1] = jax.nn.sigmoid(_dot(hb, wg1[...])).reshape(gates.shape[1:])


def odd_proj(x, shift, scale, g, ws, consts):
    B, L, D = x.shape
    tb, tl = _row_tiles(B, L)
    xs = pl.BlockSpec((tb, tl, D), lambda b, l: (b, l, 0))
    ms = pl.BlockSpec((tb, 1, D), lambda b, l: (b, 0, 0))
    widths = [w.shape[1] for w in ws[:7]]
    out_specs = [pl.BlockSpec((tb, tl, n), lambda b, l: (b, l, 0)) for n in widths]
    out_specs.append(pl.BlockSpec((2, tb, tl, LANES), lambda b, l: (0, b, l, 0)))
    out_shape = [jax.ShapeDtypeStruct((B, L, n), F32) for n in widths]
    out_shape.append(jax.ShapeDtypeStruct((2, B, L, LANES), F32))
    return pl.pallas_call(
        _odd_proj_kernel,
        grid=(B // tb, L // tl),
        in_specs=[xs, ms, ms, _const_spec((1, D))] + [_const_spec(w.shape) for w in ws]
        + [_const_spec(c.shape) for c in consts],
        out_specs=out_specs,
        out_shape=out_shape,
        compiler_params=_cparams(("parallel", "parallel")),
        name="odd_proj",
    )(x, shift, scale, g, *ws, *consts)


def _log2(n):
    assert n > 0 and n & (n - 1) == 0, n
    return n.bit_length() - 1


def _later_keys_01(tk):
    return jnp.where(_iota((tk, tk), 0) > _iota((tk, tk), 1), 1.0, 0.0).astype(BF16)


def _sb_tile(q2, kt, vt, run, acc, u01, before, feature_major=False):
    lk, ls = _sb_logs2(_dot(q2, kt) if feature_major else _dot_nt(q2, kt))
    if before is not None:
        lk = jnp.where(before, lk, 0.0)
    cs = _dot_f32_by01(lk, u01)
    w = jnp.exp2(ls + cs + run)
    if before is not None:
        w = jnp.where(before, w, 0.0)
    wb = w.astype(BF16)
    acc = acc + (_dot_nt(wb, vt) if feature_major else _dot(wb, vt))
    run = run + cs[:, 0:1] + lk[:, 0:1]
    return run, acc


def _sb_prompt_kernel(q_ref, k_ref, v_ref, o_ref, *, tq):
    i = pl.program_id(2)
    n_pairs = q_ref.shape[2] // LANES
    lane = _iota((tq, LANES), 1)
    u01 = _later_keys_01(tq)
    before = _iota((2 * tq, tq), 1) < (_iota((2 * tq, tq), 0) & (tq - 1))
    q2 = []
    for p in range(n_pairs):
        q = q_ref[0, :, p * LANES:(p + 1) * LANES] * QSCALE
        q2.append(jnp.concatenate([jnp.where(lane < HEAD_DIM, q, 0.0), jnp.where(lane >= HEAD_DIM, q, 0.0)],
                                  axis=0).astype(BF16))

    def tile(j, carry, mask):
        start = pl.multiple_of(j * tq, tq)
        out = []
        for p in range(n_pairs):
            kt = k_ref[0, pl.ds(start, tq), p * LANES:(p + 1) * LANES].astype(BF16)
            vt = v_ref[0, pl.ds(start, tq), p * LANES:(p + 1) * LANES].astype(BF16)
            out.append(_sb_tile(q2[p], kt, vt, *carry[p], u01, mask))
        return tuple(out)

    init = tuple((jnp.zeros((2 * tq, 1), F32), jnp.zeros((2 * tq, LANES), F32)) for _ in range(n_pairs))
    carry = lax.fori_loop(i, i + 1, lambda j, c: tile(j, c, before), init)
    carry = lax.fori_loop(1, i + 1, lambda jj, c: tile(i - jj, c, None), carry)
    for p in range(n_pairs):
        acc = carry[p][1]
        o_ref[0, :, p * LANES:(p + 1) * LANES] = jnp.where(lane < HEAD_DIM, acc[:tq], acc[tq:])


def sb_prompt(q, k, v):
    B, L, W = q.shape
    tq = 256
    wb = 4 * LANES
    qs = pl.BlockSpec((1, tq, wb), lambda b, p, i: (b, i, p))
    ks = pl.BlockSpec((1, L, wb), lambda b, p, i: (b, 0, p))
    return pl.pallas_call(
        functools.partial(_sb_prompt_kernel, tq=tq),
        grid=(B, W // wb, L // tq),
        in_specs=[qs, ks, ks],
        out_specs=qs,
        out_shape=jax.ShapeDtypeStruct(q.shape, F32),
        compiler_params=_cparams(("parallel", "parallel", "arbitrary")),
        name="sb_prompt",
    )(q, k, v)


def _sb_sample_kernel(pt_ref, q_ref, kn_ref, vn_ref, *rest, n_tok, group):
    k_refs, v_refs = rest[:group], rest[group:2 * group]
    o_ref, run_ref, acc_ref = rest[2 * group:]
    s = pl.program_id(1)
    q = q_ref[0] * QSCALE
    W = q.shape[1]
    n_heads = W // HEAD_DIM
    R = n_heads * n_tok
    tk = kn_ref.shape[1]
    own = (_iota((R, W), 0) >> _log2(n_tok)) == (_iota((R, W), 1) >> _log2(HEAD_DIM))
    q2 = jnp.where(own, jnp.concatenate([q] * n_heads, axis=0), 0.0).astype(BF16)
    u01 = _later_keys_01(tk)

    @pl.when(s == 0)
    def _():
        before = _iota((R, tk), 1) < (_iota((R, tk), 0) & (n_tok - 1))
        run, acc = _sb_tile(q2, kn_ref[0].astype(BF16), vn_ref[0].astype(BF16),
                            jnp.zeros((R, 1), F32), jnp.zeros((R, W), F32), u01, before)
        run_ref[...] = run
        acc_ref[...] = acc

    @pl.when(s > 0)
    def _():
        kt = jnp.concatenate([r[0, 0] for r in k_refs], axis=1).astype(BF16)
        vt = jnp.concatenate([r[0, 0] for r in v_refs], axis=1).astype(BF16)

        def stack(x):
            return jnp.concatenate([x[:, kk * tk:(kk + 1) * tk] for kk in range(group)], axis=0)

        lk, ls = _sb_logs2(_dot(q2, kt))
        lk = stack(lk)
        cs = _dot_f32_by01(lk, u01)
        tot = cs[:, 0:1] + lk[:, 0:1]
        run = run_ref[...]
        runs = []
        for kk in range(group):
            runs.append(run)
            run = run + tot[kk * R:(kk + 1) * R]
        w = jnp.exp2(stack(ls) + cs + jnp.concatenate(runs, axis=0))
        w = jnp.concatenate([w[kk * R:(kk + 1) * R] for kk in range(group)], axis=1).astype(BF16)
        acc_ref[...] = acc_ref[...] + _dot_nt(w, vt)
        run_ref[...] = run

    @pl.when(s == pl.num_programs(1) - 1)
    def _():
        acc = jnp.where(own, acc_ref[...], 0.0)
        out = acc[0:n_tok]
        for h in range(1, n_heads):
            out = out + acc[h * n_tok:(h + 1) * n_tok]
        o_ref[0] = out


def sb_sample(q, k_new, v_new, pool_k, pool_v, layer, page_table):
    B, n_tok, W = q.shape
    page = pool_k.shape[3]
    n_pages = page_table.shape[1]
    group = PAGE_GROUP
    n_steps = 1 + n_pages // group
    pad = ((0, 0), (0, page - n_tok), (0, 0))
    kn, vn = jnp.pad(k_new, pad), jnp.pad(v_new, pad)

    def page_spec(kk):
        def imap(b, s, pt):
            return (layer, pt[b, n_pages - 1 - (jnp.maximum(s, 1) - 1) * group - kk], 0, 0)
        return pl.BlockSpec((1, 1, W, page), imap)

    qs = pl.BlockSpec((1, n_tok, W), lambda b, s, pt: (b, 0, 0))
    ns = pl.BlockSpec((1, page, W), lambda b, s, pt: (b, 0, 0))
    R = (W // HEAD_DIM) * n_tok
    grid_spec = pltpu.PrefetchScalarGridSpec(
        num_scalar_prefetch=1,
        grid=(B, n_steps),
        in_specs=[qs, ns, ns] + [page_spec(kk) for kk in range(group)] * 2,
        out_specs=qs,
        scratch_shapes=[pltpu.VMEM((R, 1), F32), pltpu.VMEM((R, W), F32)],
    )
    return pl.pallas_call(
        functools.partial(_sb_sample_kernel, n_tok=n_tok, group=group),
        grid_spec=grid_spec,
        out_shape=jax.ShapeDtypeStruct(q.shape, F32),
        compiler_params=_cparams(("parallel", "arbitrary")),
        name="sb_sample",
    )(page_table, q, kn, vn, *([pool_k] * group), *([pool_v] * group))


def _gla_kernel(q_ref, k_ref, v_ref, r_ref, la_ref, s0_ref, g_ref, o_ref, sfin_ref, st, *, dk, dv):
    c = pl.program_id(1)

    @pl.when(c == 0)
    def _():
        st[...] = s0_ref[...]

    for ib in range(q_ref.shape[0]):
        _gla_chunk(q_ref.at[ib], k_ref.at[ib], v_ref.at[ib], r_ref.at[ib], la_ref.at[ib], g_ref, o_ref.at[ib],
                   st.at[ib], dk=dk, dv=dv)

    @pl.when(c == pl.num_programs(1) - 1)
    def _():
        sfin_ref[...] = st[...]


def _gla_chunk(q_ref, k_ref, v_ref, r_ref, la_ref, g_ref, o_ref, st, *, dk, dv):
    q = q_ref[...] * (dk ** -0.5)
    k = k_ref[...]
    la = la_ref[...]
    C, HK = q.shape
    n_heads = HK // dk
    causal = _iota((C, C), 0) >= _iota((C, C), 1)
    ltri = jnp.where(causal, 1.0, 0.0).astype(BF16)
    hi = la.astype(BF16)
    r1 = la - hi.astype(F32)
    mid = r1.astype(BF16)
    lo = (r1 - mid.astype(F32)).astype(BF16)
    b = _dot(ltri, hi) + _dot(ltri, mid) + _dot(ltri, lo)
    b_last = b[C - 1:C, :]
    q_in = q * jnp.exp(b)
    k_in = (k * jnp.exp(-b)).astype(BF16)
    k_dec = k * jnp.exp(b_last - b)
    lane_h = _iota((C, HK), 1) >> _log2(dk)
    s_old = st[...]
    s_old_b = s_old.astype(BF16)
    upd = jnp.zeros(s_old.shape, F32)
    for h in range(n_heads):
        own = lane_h == h
        qh = jnp.where(own, q_in, 0.0).astype(BF16)
        att = jnp.where(causal, _dot_nt(qh, k_in), 0.0)
        vh = v_ref[:, h * dv:(h + 1) * dv].astype(BF16)
        oh = _dot(att.astype(BF16), vh) + _dot_nt(qh, s_old_b)
        y = oh * lax.rsqrt(jnp.mean(oh * oh, axis=-1, keepdims=True) + EPS) * g_ref[...]
        o_ref[:, h * dv:(h + 1) * dv] = y * _silu(r_ref[:, h * dv:(h + 1) * dv])
        kd = jnp.where(own, k_dec, 0.0).astype(BF16)
        upd = upd + _dot_tn(vh, kd)
    st[...] = s_old * jnp.exp(b_last) + upd


def gla(q, k, v, r, la, s0t, g):
    B, L, HK = q.shape
    HV = v.shape[2]
    dv = g.shape[1]
    dk = HK // (HV // dv)
    C = min(GLA_CHUNK, L)
    bb = next(n for n in (8, 4, 2, 1) if B % n == 0)
    ks = pl.BlockSpec((bb, C, HK), lambda b, c: (b, c, 0))
    vs = pl.BlockSpec((bb, C, HV), lambda b, c: (b, c, 0))
    ss = pl.BlockSpec((bb, dv, HK), lambda b, c: (b, 0, 0))
    return pl.pallas_call(
        functools.partial(_gla_kernel, dk=dk, dv=dv),
        grid=(B // bb, L // C),
        in_specs=[ks, ks, vs, vs, ks, ss, pl.BlockSpec((1, dv), lambda b, c: (0, 0))],
        out_specs=[vs, ss],
        out_shape=[jax.ShapeDtypeStruct(v.shape, F32), jax.ShapeDtypeStruct(s0t.shape, F32)],
        scratch_shapes=[pltpu.VMEM((bb, dv, HK), F32)],
        compiler_params=_cparams(("parallel", "arbitrary")),
        name="gla",
    )(q, k, v, r, la, s0t, g)


def _gelu_tanh(x):
    return x * (0.5 * (1.0 + jnp.tanh(math.sqrt(2.0 / math.pi) * (x + 0.044715 * (x * x * x)))))


def _compress_mlp(seg_rows, w1_ref, b1_ref, w2_ref, b2_ref, bd_ref, g_ref, o_ref, *, n_pairs, nseg, norm):
    rk = w1_ref.shape[2] // LANES
    wh = w1_ref.shape[3]
    R = n_pairs * nseg
    p0 = jnp.zeros((R, wh), F32)
    p1 = jnp.zeros((R, wh), F32)
    for rr in range(CMP_STRIDE // rk):
        rows = jnp.concatenate(
            [jnp.concatenate([seg_rows(p, rr * rk + u) for u in range(rk)], axis=1) for p in range(n_pairs)],
            axis=0).astype(BF16)
        p0 = p0 + _dot(rows, w1_ref[0, rr])
        p1 = p1 + _dot(rows, w1_ref[1, rr])
    hid = b1_ref[...] + p0 + pltpu.roll(p1, R - 1, 0)
    y = _dot(_gelu_tanh(hid).astype(BF16), w2_ref[...]) + b2_ref[...]
    if norm:
        y = _head_rms(y, bd_ref[...], g_ref[...])
    y = jnp.where((_iota(y.shape, 0) & (nseg - 1)) < nseg - 1, y, 0.0)
    for p in range(n_pairs):
        o_ref[0, :, p * LANES:(p + 1) * LANES] = y[p * nseg:(p + 1) * nseg]


def _compress_kernel(*refs, n_pairs, nseg, norm):
    x_refs = refs[:n_pairs]
    _compress_mlp(lambda p, r: x_refs[p][0, pl.ds(r, nseg, stride=CMP_STRIDE), :], *refs[n_pairs:],
                  n_pairs=n_pairs, nseg=nseg, norm=norm)


def compress(x, n_rows, w1e, b1e, w2e, b2e, bd, g, norm):
    B, _, W = x.shape
    nseg = n_rows // CMP_STRIDE
    _log2(nseg)
    n_pairs = W // LANES
    return pl.pallas_call(
        functools.partial(_compress_kernel, n_pairs=n_pairs, nseg=nseg, norm=norm),
        grid=(B,),
        in_specs=[pl.BlockSpec((1, n_rows, LANES), functools.partial(lambda b, p: (b, 0, p), p=p))
                  for p in range(n_pairs)]
        + [_const_spec(a.shape) for a in (w1e, b1e, w2e, b2e, bd, g)],
        out_specs=pl.BlockSpec((1, nseg, W), lambda b: (b, 0, 0)),
        out_shape=jax.ShapeDtypeStruct((B, nseg, W), F32),
        compiler_params=_cparams(("parallel",)),
        name="compress",
    )(*([x] * n_pairs), w1e, b1e, w2e, b2e, bd, g)


def _compress_paged_kernel(pt_ref, *refs, group, n_pairs, nseg, norm):
    page_refs = refs[:group]
    xs = refs[-1]
    s = pl.program_id(1)
    page = page_refs[0].shape[3]
    for kk, pr in enumerate(page_refs):
        start = pl.multiple_of((s * group + kk) * page, page)
        for p in range(n_pairs):
            xs[p, pl.ds(start, page), :] = pr[0, 0, p * LANES:(p + 1) * LANES, :].T

    @pl.when(s == pl.num_programs(1) - 1)
    def _():
        _compress_mlp(lambda p, r: xs[p, pl.ds(r, nseg, stride=CMP_STRIDE), :], *refs[group:-1],
                      n_pairs=n_pairs, nseg=nseg, norm=norm)


def compress_paged(pool, layer, page_table, w1e, b1e, w2e, b2e, bd, g, norm):
    _, _, W, page = pool.shape
    B, n_pages = page_table.shape
    group = PAGE_GROUP
    n_rows = n_pages * page
    nseg = n_rows // CMP_STRIDE
    _log2(nseg)
    n_pairs = W // LANES

    def page_spec(kk):
        return pl.BlockSpec((1, 1, W, page), lambda b, s, pt: (layer, pt[b, s * group + kk], 0, 0))

    consts = (w1e, b1e, w2e, b2e, bd, g)
    grid_spec = pltpu.PrefetchScalarGridSpec(
        num_scalar_prefetch=1,
        grid=(B, n_pages // group),
        in_specs=[page_spec(kk) for kk in range(group)]
        + [pl.BlockSpec(a.shape, functools.partial(lambda b, s, pt, nd: (0,) * nd, nd=a.ndim),
                        pipeline_mode=pl.Buffered(1)) for a in consts],
        out_specs=pl.BlockSpec((1, nseg, W), lambda b, s, pt: (b, 0, 0)),
        scratch_shapes=[pltpu.VMEM((n_pairs, n_rows, LANES), F32)],
    )
    return pl.pallas_call(
        functools.partial(_compress_paged_kernel, group=group, n_pairs=n_pairs, nseg=nseg, norm=norm),
        grid_spec=grid_spec,
        out_shape=jax.ShapeDtypeStruct((B, nseg, W), F32),
        compiler_params=_cparams(("parallel", "arbitrary")),
        name="compress_paged",
    )(page_table, *([pool] * group), *consts)


_REL_EXACT = REL_BUCKETS // 2
_REL_THRESHOLDS = tuple(
    math.ceil(_REL_EXACT * (REL_MAX_DIST / _REL_EXACT) ** (k / (REL_BUCKETS - _REL_EXACT)) - 1e-9)
    for k in range(1, REL_BUCKETS - _REL_EXACT))


def _bias_kernel(base_ref, tbl_ref, o_ref, *, cstep):
    t = pl.program_id(0)
    _, H, R, C = o_ref.shape
    dist = base_ref[t] + _iota((R, C), 0) - cstep * _iota((R, C), 1)
    d = jnp.maximum(dist, 0)
    big = jnp.full((R, C), _REL_EXACT, jnp.int32)
    for th in _REL_THRESHOLDS:
        big = big + jnp.where(d >= th, 1, 0)
    bucket = jnp.where(d < _REL_EXACT, d, big)
    for h in range(H):
        acc = jnp.full((R, C), tbl_ref[h, REL_BUCKETS - 1], F32)
        for u in range(REL_BUCKETS - 1):
            acc = jnp.where(bucket == u, tbl_ref[h, u], acc)
        o_ref[0, h] = acc * LOG2E


def bias_tiles(rel_table, bases, R, C, cstep):
    H = rel_table.shape[1]
    T = bases.shape[0]
    grid_spec = pltpu.PrefetchScalarGridSpec(
        num_scalar_prefetch=1,
        grid=(T,),
        in_specs=[pl.BlockSpec(memory_space=pltpu.SMEM)],
        out_specs=pl.BlockSpec((1, H, R, C), lambda t, base: (t, 0, 0, 0)),
    )
    return pl.pallas_call(
        functools.partial(_bias_kernel, cstep=cstep),
        grid_spec=grid_spec,
        out_shape=jax.ShapeDtypeStruct((T, H, R, C), F32),
        compiler_params=_cparams(("arbitrary",)),
        name="bias_tiles",
    )(bases, rel_table.T)


def _masked_softmax(s, mask):
    s = jnp.where(mask, s, NEG)
    m = jnp.max(s, axis=-1, keepdims=True)
    p = jnp.where(mask, jnp.exp2(s - m), 0.0)
    return p * (1.0 / jnp.maximum(jnp.sum(p, axis=-1, keepdims=True), 1e-30))


def _off_mask(keep):
    return jnp.where(keep, 0.0, NEG)


def _softmax_update_add(s, m, l, acc, v, feature_major=False):
    m_new = jnp.maximum(m, jnp.max(s, axis=-1, keepdims=True))
    alpha = jnp.exp2(m - m_new)
    p = jnp.exp2(s - m_new)
    l = alpha * l + jnp.sum(p, axis=-1, keepdims=True)
    pb = p.astype(BF16)
    acc = alpha * acc + (_dot_nt(pb, v) if feature_major else _dot(pb, v))
    return m_new, l, acc


def _softmax_step(s, m, acc, v1):
    m_new = jnp.maximum(m, jnp.max(s, axis=-1, keepdims=True))
    p = jnp.exp2(s - m_new)
    return m_new, jnp.exp2(m - m_new) * acc + _dot(p.astype(BF16), v1)


def _softmax_init(R, W):
    return jnp.full((R, 1), NEG, F32), jnp.zeros((R, 1), F32), jnp.zeros((R, W), F32)


def _softmax_finish(l, acc):
    return acc * (1.0 / jnp.maximum(l, 1e-30))


def _top_blocks(score, n_real, n_top, axis=1):
    jidx = _iota(score.shape, axis)
    rank = jnp.zeros(score.shape, F32)
    for i in range(n_real):
        si = score[:, i:i + 1] if axis == 1 else score[i:i + 1, :]
        beats = (si > score) | ((si == score) & (jidx > i))
        rank = rank + jnp.where(beats, 1.0, 0.0)
    return jnp.where(rank < n_top, 1.0, 0.0)


def _block_scores(imp, q_pos, n_sel, axis=1):
    j = _iota(imp.shape, axis)
    cur = q_pos >> _log2(SLC_BLOCK)
    valid = j * SLC_BLOCK <= q_pos
    forced = valid & ((j == 0) | (j == cur) | (j == cur - 1))
    score = jnp.where(forced, FORCE, jnp.where(valid, imp, -FORCE))
    return jnp.where(j < n_sel, score, -3e38)


def _block_of_key_01(n_blocks, tk, pos0):
    j = _iota((n_blocks, tk), 0)
    c = _iota((n_blocks, tk), 1)
    return jnp.where(((pos0 + c) >> _log2(SLC_BLOCK)) == j, 1.0, 0.0).astype(BF16)


def _half_to(x, src_high, dst_high):
    return x if src_high == dst_high else pltpu.roll(x, HEAD_DIM, 1)


def _nsa_prompt_kernel(q_ref, g_ref, kc_ref, vc_ref, ks_ref, vs_ref, kw_ref, vw_ref, bd_ref, bc_ref, covt_ref,
                       brange_ref, o_ref, woff_sc, knorm_sc, *, tq, n_cmp, n_sel, n_top, group):
    i = pl.program_id(2)
    G = group
    R = G * tq
    nt = tq // LANES
    JP = _round_up(n_sel, 8)
    n_win = WINDOW // tq
    low = _iota((tq, LANES), 1) < HEAD_DIM
    SHIFT_LANE = HEAD_DIM - 1

    @pl.when(i == 0)
    def _():
        dist0 = _iota((tq, tq), 0) - _iota((tq, tq), 1)
        for dd in range(n_win + 1):
            woff_sc[dd] = _off_mask((dist0 + dd * tq >= 0) & (dist0 + dd * tq < WINDOW))
        for a, ref in enumerate((ks_ref, kw_ref)):
            k2 = ref[0] * ref[0]
            low_k = _iota(k2.shape, 1) < HEAD_DIM
            for hh in range(2):
                n2 = jnp.sum(jnp.where(low_k if hh == 0 else jnp.logical_not(low_k), k2, 0.0), axis=-1, keepdims=True)
                knorm_sc[2 * a + hh] = jnp.broadcast_to(jnp.sqrt(jnp.max(n2, axis=0, keepdims=True)), (8, LANES))

    def tile_off(dd):
        return jnp.concatenate([woff_sc[dd]] * G, axis=0)

    def prepare(hh):
        own = low if hh == 0 else jnp.logical_not(low)
        pieces = []
        for g in range(G):
            x = q_ref[0, :, (hh * 2 + g // 2) * LANES:(hh * 2 + g // 2 + 1) * LANES] * QSCALE
            pieces.append(_half_to(x, g % 2 == 1, hh == 1))
        qs = jnp.concatenate([jnp.where(own, x, 0.0) for x in pieces], axis=0).astype(BF16)

        bias_c = jnp.concatenate([bc_ref[a, hh * G + g] for g in range(G) for a in range(nt)], axis=0)
        s = _dot_nt(qs, kc_ref[0].astype(BF16)) + bias_c
        rowc = _iota(s.shape, 0) & (tq - 1)
        colc = _iota(s.shape, 1)
        dist_c = i * tq + rowc - (colc * CMP_STRIDE + (CMP_BLOCK - 1))
        p_c = _masked_softmax(s, (dist_c >= 0) & (colc < n_cmp)).astype(BF16)
        o_c = _dot(p_c, vc_ref[0].astype(BF16))
        covt = covt_ref[0:JP, :]
        impt = _dot_nt(covt, p_c[0:tq])
        for g in range(1, G):
            impt = impt + _dot_nt(covt, p_c[g * tq:(g + 1) * tq])
        q_pos = i * tq + _iota(impt.shape, 1)
        selt = _top_blocks(_block_scores(impt, q_pos, n_sel, 0), n_sel, n_top, 0)
        selt = jnp.concatenate([selt, jnp.zeros((LANES - JP, tq), F32)], axis=0)
        sel = selt.T

        sel_off = _half_to(_off_mask(sel > 0.5), False, hh == 0)
        qsel = jnp.concatenate([jnp.where(own, x, sel_off) for x in pieces], axis=0).astype(BF16)

        is_shift = (_iota((tq, LANES), 1) - (HEAD_DIM if hh == 0 else 0)) == SHIFT_LANE
        qn = [jnp.sqrt(jnp.sum(jnp.where(own, x * x, 0.0), axis=-1, keepdims=True)) for x in pieces]
        kn_s, kn_w = knorm_sc[hh][0:1, 0:1], knorm_sc[2 + hh][0:1, 0:1]
        bmax, bspan = brange_ref[0], brange_ref[1]
        q_slc = jnp.concatenate([jnp.where(own, x, jnp.where(is_shift, -(n * kn_s + bmax), sel_off))
                                 for x, n in zip(pieces, qn)], axis=0).astype(BF16)
        q_win = jnp.concatenate([jnp.where(own, x, jnp.where(is_shift, -(n * kn_w + bmax), 0.0))
                                 for x, n in zip(pieces, qn)], axis=0).astype(BF16)
        span = 2.0 * jnp.max(jnp.concatenate(qn, axis=0) * jnp.maximum(kn_s, kn_w)) + bspan
        return dict(qs=qs, qsel=qsel, q_slc=q_slc, q_win=q_win, o_c=o_c, fixed_shift=span <= MAX_SHIFT_SPAN)

    def tiles(hh, P):
        qs, qsel = P['qs'], P['qsel']
        own = low if hh == 0 else jnp.logical_not(low)
        spare_block = _iota((tq, LANES), 1) - (HEAD_DIM if hh == 0 else 0)
        is_shift = spare_block == SHIFT_LANE
        key_row = _iota((tq, LANES), 0)

        def values(ref, start):
            return jnp.where(own, ref[0, pl.ds(start, tq), :], 1.0).astype(BF16)

        def dense_bias(d):
            blocks = []
            for g in range(G):
                for a in range(nt):
                    blocks.append(jnp.concatenate(
                        [bd_ref[jnp.maximum(nt * d + a - b, 0), hh * G + g] for b in range(nt)], axis=1))
            return jnp.concatenate(blocks, axis=0)

        def slc_tile(j, carry, diag):
            start = pl.multiple_of(j * tq, tq)
            in_block = ((start + key_row) >> _log2(SLC_BLOCK)) == spare_block
            kt = jnp.where(own, ks_ref[0, pl.ds(start, tq), :], jnp.where(in_block, 1.0, 0.0)).astype(BF16)
            s = _dot_nt(qsel, kt) + dense_bias(i - j)
            if diag:
                s = s + tile_off(0)
            return _softmax_step(s, *carry, values(vs_ref, start))

        def win_tile(dd, carry):
            start = pl.multiple_of((i - dd) * tq, tq)
            kt = kw_ref[0, pl.ds(start, tq), :].astype(BF16)
            s = _dot_nt(qs, kt) + dense_bias(dd) + tile_off(dd)
            return _softmax_step(s, *carry, values(vw_ref, start))

        def running_max():
            init = (jnp.full((R, 1), NEG, F32), jnp.zeros((R, LANES), F32))
            carry = lax.fori_loop(0, i, lambda j, c: slc_tile(j, c, False), init)
            acc_s = lax.fori_loop(i, i + 1, lambda j, c: slc_tile(j, c, True), carry)[1]
            return acc_s, lax.fori_loop(0, jnp.minimum(i, n_win) + 1, win_tile, init)[1]

        def slc_shifted(j, acc, diag):
            start = pl.multiple_of(j * tq, tq)
            in_block = ((start + key_row) >> _log2(SLC_BLOCK)) == spare_block
            kt = jnp.where(own, ks_ref[0, pl.ds(start, tq), :],
                           jnp.where(in_block | is_shift, 1.0, 0.0)).astype(BF16)
            s = _dot_nt(P['q_slc'], kt) + dense_bias(i - j)
            if diag:
                s = s + tile_off(0)
            return acc + _dot(jnp.exp2(s).astype(BF16), values(vs_ref, start))

        def win_shifted(dd, acc):
            start = pl.multiple_of((i - dd) * tq, tq)
            kt = jnp.where(own, kw_ref[0, pl.ds(start, tq), :], jnp.where(is_shift, 1.0, 0.0)).astype(BF16)
            s = _dot_nt(P['q_win'], kt) + dense_bias(dd) + tile_off(dd)
            return acc + _dot(jnp.exp2(s).astype(BF16), values(vw_ref, start))

        return running_max, slc_shifted, win_shifted

    def attend(prepared):
        (run0, slc0, win0), (run1, slc1, win1) = [tiles(hh, prepared[hh]) for hh in range(2)]

        def fixed_shift():
            zero = jnp.zeros((R, LANES), F32)
            acc = lax.fori_loop(0, i, lambda j, a: (slc0(j, a[0], False), slc1(j, a[1], False)), (zero, zero))
            acc = lax.fori_loop(i, i + 1, lambda j, a: (slc0(j, a[0], True), slc1(j, a[1], True)), acc)
            win = lax.fori_loop(0, jnp.minimum(i, n_win) + 1, lambda dd, a: (win0(dd, a[0]), win1(dd, a[1])),
                                (zero, zero))
            return (acc[0], win[0]), (acc[1], win[1])

        both = prepared[0]['fixed_shift'] & prepared[1]['fixed_shift']
        return lax.cond(both, fixed_shift, lambda: (run0(), run1()))

    def finish(hh, o_c, acc_s, acc_w):
        def gate_col(br):
            return jnp.concatenate([g_ref[0, 0, :, hh * 3 * G + g * 3 + br:hh * 3 * G + g * 3 + br + 1]
                                    for g in range(G)], axis=0)

        def weight_col(br, acc):
            c = HEAD_DIM if hh == 0 else 0
            return gate_col(br) * (1.0 / jnp.maximum(acc[:, c:c + 1], 1e-30))

        o = gate_col(0) * o_c + weight_col(1, acc_s) * acc_s + weight_col(2, acc_w) * acc_w
        for vc in range(G // 2):
            lo_piece = _half_to(o[(2 * vc) * tq:(2 * vc + 1) * tq], hh == 1, False)
            hi_piece = _half_to(o[(2 * vc + 1) * tq:(2 * vc + 2) * tq], hh == 1, True)
            o_ref[0, :, (hh * 2 + vc) * LANES:(hh * 2 + vc + 1) * LANES] = jnp.where(low, lo_piece, hi_piece)

    prepared = [prepare(hh) for hh in range(2)]
    attended = attend(prepared)
    for hh in range(2):
        finish(hh, prepared[hh]['o_c'], *attended[hh])


def nsa_prompt(q, gates, kcmp, vcmp, ks, vs, kw, vw, bias_dense, bias_cmp, cover, bias_range, n_cmp, n_sel):
    B, L, WQ = q.shape
    WK = ks.shape[2]
    group = WQ // WK
    tq = 256
    nt = tq // LANES
    assert n_sel < HEAD_DIM
    n_pairs = WK // LANES
    n_top = min(SLC_TOPN, n_sel)
    qw = WQ // n_pairs
    hp = bias_dense.shape[1] // n_pairs
    kv = pl.BlockSpec((1, L, LANES), lambda b, p, i: (b, 0, p))
    cm = pl.BlockSpec((1, kcmp.shape[1], LANES), lambda b, p, i: (b, 0, p))
    return pl.pallas_call(
        functools.partial(_nsa_prompt_kernel, tq=tq, n_cmp=n_cmp, n_sel=n_sel, n_top=n_top, group=group),
        grid=(B, n_pairs, L // tq),
        in_specs=[
            pl.BlockSpec((1, tq, qw), lambda b, p, i: (b, i, p)),
            pl.BlockSpec((1, 1, tq, LANES), lambda b, p, i: (p, b, i, 0)),
            cm, cm, kv, kv, kv, kv,
            pl.BlockSpec((bias_dense.shape[0], hp, LANES, LANES), lambda b, p, i: (0, p, 0, 0)),
            pl.BlockSpec((nt, hp, LANES, LANES), lambda b, p, i: (i, p, 0, 0)),
            _const_spec(cover.shape),
            pl.BlockSpec(memory_space=pltpu.SMEM),
        ],
        out_specs=pl.BlockSpec((1, tq, qw), lambda b, p, i: (b, i, p)),
        out_shape=jax.ShapeDtypeStruct(q.shape, F32),
        scratch_shapes=[pltpu.VMEM((WINDOW // tq + 1, tq, tq), F32), pltpu.VMEM((4, 8, LANES), F32)],
        compiler_params=_cparams(("parallel", "parallel", "arbitrary")),
        name="nsa_prompt",
    )(q, gates, kcmp, vcmp, ks, vs, kw, vw, bias_dense, bias_cmp, cover, bias_range)


def _nsa_sample_kernel(pt_ref, q_ref, g_ref, kc_ref, vc_ref, ksn_ref, vsn_ref, kwp_ref, vwp_ref,
                       kwn_ref, vwn_ref, bs_ref, bn_ref, bc_ref, bw_ref, cov_ref, e_ref, *rest,
                       n_tok, q_base, n_cmp, n_sel, n_top, group, pages):
    k_refs, v_refs = rest[:pages], rest[pages:2 * pages]
    o_ref, qs_sc, sel_sc, oc_sc, m_sc, l_sc, acc_sc = rest[2 * pages:]
    ci = pl.program_id(1)
    R, W = qs_sc.shape
    n_heads = R // n_tok
    J = sel_sc.shape[1]

    def tok(shape):
        return _iota(shape, 0) & (n_tok - 1)

    @pl.when(ci == 0)
    def _():
        blocks = []
        for h in range(n_heads):
            hkv = h // group
            x = q_ref[0, :, (h // 2) * LANES:(h // 2 + 1) * LANES] * QSCALE
            x = _half_to(x, h % 2 == 1, hkv % 2 == 1)
            keep = (_iota(x.shape, 1) >= HEAD_DIM) if hkv % 2 == 1 else (_iota(x.shape, 1) < HEAD_DIM)
            x = jnp.where(keep, x, 0.0)
            blocks.append(jnp.concatenate(
                [x if cc == hkv // 2 else jnp.zeros_like(x) for cc in range(W // LANES)], axis=1))
        qs = jnp.concatenate(blocks, axis=0).astype(BF16)
        qs_sc[...] = qs
        s = _dot_nt(qs, kc_ref[0].astype(BF16)) + bc_ref[...]
        n = _iota(s.shape, 1)
        dist_c = q_base + tok(s.shape) - (n * CMP_STRIDE + (CMP_BLOCK - 1))
        p_c = _masked_softmax(s, (dist_c >= 0) & (n < n_cmp)).astype(BF16)
        oc_sc[...] = _dot(p_c, vc_ref[0].astype(BF16))
        ic = _dot(p_c, cov_ref[...])
        rows = group * n_tok
        imp_blocks = []
        for hkv in range(n_heads // group):
            a = ic[hkv * rows:hkv * rows + n_tok]
            for g in range(1, group):
                a = a + ic[hkv * rows + g * n_tok:hkv * rows + (g + 1) * n_tok]
            imp_blocks.append(a)
        imp = jnp.concatenate(imp_blocks, axis=0)
        chosen = _top_blocks(_block_scores(imp, q_base + tok(imp.shape), n_sel), n_sel, n_top)
        sel_sc[...] = jnp.concatenate(
            [chosen[hkv * n_tok:(hkv + 1) * n_tok] for hkv in range(n_heads // group) for _ in range(group)], axis=0)
        m0, l0, a0 = _softmax_init(R, W)
        m_sc[...] = m0
        l_sc[...] = l0
        acc_sc[...] = a0

    qs = qs_sc[...]
    sel = sel_sc[...].astype(BF16)

    kt = jnp.concatenate([r[0, 0] for r in k_refs], axis=1).astype(BF16)
    vt = jnp.concatenate([r[0, 0] for r in v_refs], axis=1).astype(BF16)
    s = _dot(qs, kt) + bs_ref[...] + _off_mask(_dot(sel, e_ref[...]) > 0.5)
    m, l, acc = _softmax_update_add(s, m_sc[...], l_sc[...], acc_sc[...], vt, True)
    m_sc[...] = m
    l_sc[...] = l
    acc_sc[...] = acc

    @pl.when(ci == pl.num_programs(1) - 1)
    def _():
        tn = ksn_ref.shape[1]
        c = _iota((R, tn), 1)
        causal_new = c <= tok((R, tn))
        s = _dot_nt(qs, ksn_ref[0].astype(BF16)) + bn_ref[...]
        chosen = _dot(sel, _block_of_key_01(J, tn, q_base))
        _, l1, a1 = _softmax_update_add(s + _off_mask((chosen > 0.5) & causal_new), m, l, acc,
                                        vsn_ref[0].astype(BF16))
        o_s = _softmax_finish(l1, a1)
        s = _dot(qs, kwp_ref[0, 0].astype(BF16)) + bw_ref[...]
        dist = WINDOW + tok(s.shape) - _iota(s.shape, 1)
        cw = _softmax_update_add(s + _off_mask((dist >= 0) & (dist < WINDOW)), *_softmax_init(R, W),
                                 vwp_ref[0, 0].astype(BF16), True)
        s = _dot_nt(qs, kwn_ref[0].astype(BF16)) + bn_ref[...]
        _, l2, a2 = _softmax_update_add(s + _off_mask(causal_new), *cw, vwn_ref[0].astype(BF16))
        o_w = _softmax_finish(l2, a2)

        def gate_col(br):
            cols = []
            for h in range(n_heads):
                hkv, g = divmod(h, group)
                cidx = (hkv % 2) * 3 * group + g * 3 + br
                cols.append(g_ref[hkv // 2, 0, :, cidx:cidx + 1])
            return jnp.concatenate(cols, axis=0)

        o = gate_col(0) * oc_sc[...] + gate_col(1) * o_s + gate_col(2) * o_w
        low = _iota((n_tok, LANES), 1) < HEAD_DIM
        for oc in range(n_heads // 2):
            hkv = (2 * oc) // group
            src = slice((hkv // 2) * LANES, (hkv // 2 + 1) * LANES)
            lo_piece = _half_to(o[(2 * oc) * n_tok:(2 * oc + 1) * n_tok, src], hkv % 2 == 1, False)
            hi_piece = _half_to(o[(2 * oc + 1) * n_tok:(2 * oc + 2) * n_tok, src], hkv % 2 == 1, True)
            o_ref[0, :, oc * LANES:(oc + 1) * LANES] = jnp.where(low, lo_piece, hi_piece)


def nsa_sample(q, gates, kcmp, vcmp, pool_k, pool_v, win_k, win_v, layer, page_table, ks_new, vs_new,
               kw_new, vw_new, bias_slc, bias_new, bias_cmp, bias_win, cover, n_cmp, n_sel):
    B, n_tok, WQ = q.shape
    WK, page = pool_k.shape[2], pool_k.shape[3]
    n_pages = page_table.shape[1]
    past = n_pages * page
    group = WQ // WK
    n_heads = WQ // HEAD_DIM
    R = n_heads * n_tok
    J = cover.shape[1]
    pages = PAGE_GROUP
    chunk = pages * page
    block_of_key = (jnp.arange(past)[None, :] // SLC_BLOCK == jnp.arange(J)[:, None]).astype(BF16)

    def bspec(a):
        return pl.BlockSpec((1,) + a.shape[1:], lambda b, c, pt: (b,) + (0,) * (a.ndim - 1))

    def cspec(a):
        return pl.BlockSpec(a.shape, lambda b, c, pt: (0,) * a.ndim, pipeline_mode=pl.Buffered(1))

    def page_spec(kk):
        return pl.BlockSpec((1, 1, WK, page), lambda b, c, pt: (layer, pt[b, c * pages + kk], 0, 0))

    wspec = pl.BlockSpec((1, 1, WK, win_k.shape[3]), lambda b, c, pt: (layer, b, 0, 0))
    grid_spec = pltpu.PrefetchScalarGridSpec(
        num_scalar_prefetch=1,
        grid=(B, n_pages // pages),
        in_specs=[
            bspec(q),
            pl.BlockSpec((gates.shape[0], 1, n_tok, LANES), lambda b, c, pt: (0, b, 0, 0)),
            bspec(kcmp), bspec(vcmp), bspec(ks_new), bspec(vs_new), wspec, wspec, bspec(kw_new), bspec(vw_new),
            pl.BlockSpec((R, chunk), lambda b, c, pt: (0, c)),
            cspec(bias_new), cspec(bias_cmp), cspec(bias_win), cspec(cover),
            pl.BlockSpec((J, chunk), lambda b, c, pt: (0, c)),
        ] + [page_spec(kk) for kk in range(pages)] * 2,
        out_specs=bspec(q),
        scratch_shapes=[pltpu.VMEM((R, WK), BF16), pltpu.VMEM((R, J), F32), pltpu.VMEM((R, WK), F32),
                        pltpu.VMEM((R, 1), F32), pltpu.VMEM((R, 1), F32), pltpu.VMEM((R, WK), F32)],
    )
    return pl.pallas_call(
        functools.partial(_nsa_sample_kernel, n_tok=n_tok, q_base=past, n_cmp=n_cmp, n_sel=n_sel,
                          n_top=min(SLC_TOPN, n_sel), group=group, pages=pages),
        grid_spec=grid_spec,
        out_shape=jax.ShapeDtypeStruct(q.shape, F32),
        compiler_params=_cparams(("parallel", "arbitrary")),
        name="nsa_sample",
    )(page_table, q, gates, kcmp, vcmp, ks_new, vs_new, win_k, win_v, kw_new, vw_new,
      bias_slc, bias_new, bias_cmp, bias_win, cover, block_of_key, *([pool_k] * pages), *([pool_v] * pages))


def _block_diag_ones(width):
    h = jnp.arange(width) // HEAD_DIM
    return (h[:, None] == h[None, :]).astype(BF16)


def _cover_01(n_cmp, n_sel, rows, cols):
    ci = jnp.arange(rows) * CMP_STRIDE
    sj = jnp.arange(cols) * SLC_BLOCK
    hit = (ci[:, None] < sj[None, :] + SLC_BLOCK) & (ci[:, None] + CMP_BLOCK > sj[None, :])
    hit = hit & (jnp.arange(rows)[:, None] < n_cmp) & (jnp.arange(cols)[None, :] < n_sel)
    return hit.astype(BF16)


def _round_up(n, m):
    return -(-n // m) * m


def _compress_weights(w1, b1, w2, b2, n_kv):
    hidden = w1.shape[1]
    eye = jnp.eye(n_kv, dtype=F32)
    w1r = w1.reshape(CMP_BLOCK // CMP_STRIDE, CMP_STRIDE, HEAD_DIM, hidden)
    rk = 2
    w1e = jnp.einsum('mrde,hk->mrhdke', w1r, eye).reshape(
        CMP_BLOCK // CMP_STRIDE, CMP_STRIDE // rk, rk * n_kv * HEAD_DIM, n_kv * hidden).astype(BF16)
    w2e = jnp.einsum('ed,hk->hekd', w2, eye).reshape(n_kv * hidden, n_kv * HEAD_DIM).astype(BF16)
    return w1e, jnp.tile(b1, n_kv)[None, :], w2e, jnp.tile(b2, n_kv)[None, :]


def _trunk(x, mods, past, P):
    B, L, D = x.shape
    depth = P['norm_g'].shape[0]
    even_states, odd_states = [], []
    for li in range(depth):
        def m(k):
            return mods[li, :, k][:, None, :]

        def g(k):
            return P['norm_g'][li, k][None, :]

        x = ffn(x, m(0), m(1), m(2), g(0), P['ffn_w_in'][li, 0], P['ffn_w_out'][li, 0])
        if li % 2 == 0:
            e = li // 2
            w = P['even_w_in'][e]
            n_sb = P['sb_width']
            HK, HV = P['gla_hk'], P['gla_hv']
            offs = [0, n_sb, 2 * n_sb, 3 * n_sb, 3 * n_sb + HK, 3 * n_sb + 2 * HK, 3 * n_sb + 2 * HK + HV,
                    3 * n_sb + 2 * HK + 2 * HV]
            ws = [w[:, offs[k]:offs[k + 1]] for k in range(7)]
            rank = w.shape[1] - offs[7]
            wa1 = jnp.pad(w[:, offs[7]:], ((0, 0), (0, LANES - rank)))
            wa2 = jnp.pad(P['gla_w_a2'][e], ((0, LANES - rank), (0, 0)))
            qa, ka, va, qb, kb, vb, rb, la = even_proj(x, m(3), m(4), g(1), ws, wa1, wa2, P['gla_b_a2'][e][None, :])
            dv = P['gla_onorm_g'].shape[1]
            n_gh = HV // dv
            dk = HK // n_gh
            if past is None:
                o_a = sb_prompt(qa, ka, va)
                s0t = jnp.zeros((B, dv, HK), F32)
            else:
                o_a = sb_sample(qa, ka, va, past['sb_k'], past['sb_v'], e, past['page_table'])
                s0t = past['gla'][e].transpose(0, 3, 1, 2).reshape(B, dv, HK)
            o_b, st = gla(qb, kb, vb, rb, la, s0t, P['gla_onorm_g'][e][None, :])
            wo = P['even_w_out'][e]
            mix_acts, mix_ws = [o_a, o_b], [wo[:n_sb], wo[n_sb:]]
            s_fin = st.reshape(B, dv, n_gh, dk).transpose(0, 2, 3, 1)
            even_states.append((ka.reshape(B, L, -1, HEAD_DIM), va.reshape(B, L, -1, HEAD_DIM), s_fin))
        else:
            o_i = li // 2
            w = P['odd_w_in'][o_i]
            WQ, WK = P['nsa_wq'], P['nsa_wk']
            n_kv = WK // HEAD_DIM
            group = WQ // WK
            offs = [0, WQ] + [WQ + (k + 1) * WK for k in range(6)]
            ws = [w[:, offs[k]:offs[k + 1]] for k in range(7)]
            wg = w[:, offs[7]:]
            half = wg.shape[1] // 2
            ws += [jnp.pad(wg[:, :half], ((0, 0), (0, LANES - half))),
                   jnp.pad(wg[:, half:], ((0, 0), (0, LANES - half)))]
            qk_g = P['nsa_qk_g'][o_i]
            consts = [_block_diag_ones(WQ), _block_diag_ones(WK), jnp.tile(qk_g[0], WQ // HEAD_DIM)[None, :],
                      jnp.tile(qk_g[2], n_kv)[None, :], jnp.tile(qk_g[3], n_kv)[None, :]]
            q, kc, vc, ks, vs, kw, vw, gates = odd_proj(x, m(3), m(4), g(1), ws, consts)
            pair = LANES // HEAD_DIM
            gk = jnp.tile(qk_g[1], pair)[None, :]
            bdp = _block_diag_ones(LANES)
            cw = [_compress_weights(P['cmp_w1'][o_i, t], P['cmp_b1'][o_i, t], P['cmp_w2'][o_i, t],
                                    P['cmp_b2'][o_i, t], pair) for t in range(2)]
            rel = P['rel_table']
            if past is None:
                n_rows = (L // CMP_STRIDE) * CMP_STRIDE
                assert n_rows == L and L % LANES == 0
                n_cmp = L // CMP_STRIDE - CMP_BLOCK // CMP_STRIDE + 1
                n_sel = -(-L // SLC_BLOCK)
                kcmp = compress(kc, n_rows, *cw[0], bdp, gk, True)
                vcmp = compress(vc, n_rows, *cw[1], bdp, gk, False)
                assert kcmp.shape[1] == LANES and n_sel <= LANES
                nq = L // LANES
                bias_dense = bias_tiles(rel, jnp.arange(nq, dtype=jnp.int32) * LANES, LANES, LANES, 1)
                bias_cmp = bias_tiles(rel, jnp.arange(nq, dtype=jnp.int32) * LANES - (CMP_BLOCK - 1),
                                      LANES, LANES, CMP_STRIDE)
                cover = _cover_01(n_cmp, n_sel, LANES, LANES).T
                bias_range = jnp.stack([jnp.max(rel), jnp.max(rel) - jnp.min(rel)]) * LOG2E
                o = nsa_prompt(q, gates, kcmp, vcmp, ks, vs, kw, vw, bias_dense, bias_cmp, cover, bias_range,
                               n_cmp, n_sel)
                w_keep = min(WINDOW, L)
                win_k, win_v = kw[:, L - w_keep:], vw[:, L - w_keep:]
            else:
                pt = past['page_table']
                past_len = pt.shape[1] * past['cmp_k'].shape[3]
                T = past_len + L
                n_rows = (T // CMP_STRIDE) * CMP_STRIDE
                assert n_rows == past_len and past['win_k'].shape[3] == WINDOW and L <= LANES
                n_cmp = n_rows // CMP_STRIDE - CMP_BLOCK // CMP_STRIDE + 1
                n_sel = -(-T // SLC_BLOCK)
                kcmp = compress_paged(past['cmp_k'], o_i, pt, *cw[0], bdp, gk, True)
                vcmp = compress_paged(past['cmp_v'], o_i, pt, *cw[1], bdp, gk, False)
                J = _round_up(n_sel, LANES)
                n_heads = WQ // HEAD_DIM
                R = n_heads * L

                def sample_bias(base, C, cstep):
                    return bias_tiles(rel, jnp.array([base], jnp.int32), L, C, cstep).reshape(R, C)

                bias_slc = sample_bias(past_len, past_len, 1)
                bias_new = sample_bias(0, LANES, 1)
                bias_cmp = sample_bias(past_len - (CMP_BLOCK - 1), kcmp.shape[1], CMP_STRIDE)
                bias_win = sample_bias(WINDOW, WINDOW, 1)
                cover = _cover_01(n_cmp, n_sel, kcmp.shape[1], J)
                pad = ((0, 0), (0, LANES - L), (0, 0))
                o = nsa_sample(q, gates, kcmp, vcmp, past['slc_k'], past['slc_v'], past['win_k'], past['win_v'],
                               o_i, pt, jnp.pad(ks, pad), jnp.pad(vs, pad), jnp.pad(kw, pad), jnp.pad(vw, pad),
                               bias_slc, bias_new, bias_cmp, bias_win, cover, n_cmp, n_sel)
                win_k = jnp.concatenate([past['win_k_rows'][o_i], kw], axis=1)[:, L:]
                win_v = jnp.concatenate([past['win_v_rows'][o_i], vw], axis=1)[:, L:]
            mix_acts, mix_ws = [o], [P['odd_w_out'][o_i]]

            def rows(t):
                return t.reshape(B, t.shape[1], n_kv, HEAD_DIM)

            odd_states.append(tuple(rows(t) for t in (kc, vc, ks, vs, win_k, win_v)))
        x = ffn(x, m(6), m(7), m(8), g(2), P['ffn_w_in'][li, 1], P['ffn_w_out'][li, 1], m(5), mix_acts, mix_ws)
    even_new = [jnp.stack([s[i] for s in even_states]) for i in range(3)]
    odd_new = [jnp.stack([s[i] for s in odd_states]) for i in range(6)]
    return x, even_new, odd_new


def kernel(x_prompt, x_sample, cache_sb_k, cache_sb_v, state_gla, cache_cmp_k, cache_cmp_v, cache_slc_k,
           cache_slc_v, cache_win_k, cache_win_v, page_table, c_prompt, c_sample, norm_g, ada_w, ada_b,
           ffn_w_in, ffn_w_out, even_w_in, gla_w_a2, gla_b_a2, gla_onorm_g, even_w_out, odd_w_in, nsa_qk_g,
           cmp_w1, cmp_b1, cmp_w2, cmp_b2, rel_table, odd_w_out):
    D = x_prompt.shape[-1]
    depth = norm_g.shape[0]
    Bp, Bs = x_prompt.shape[0], x_sample.shape[0]
    mods = ada_mod(jnp.concatenate([c_prompt, c_sample], axis=0), ada_w, ada_b).reshape(depth, Bp + Bs, 9, D)
    n_sb = cache_sb_k.shape[3] * cache_sb_k.shape[4]
    dv = gla_onorm_g.shape[1]
    HK = gla_w_a2.shape[2]
    n_gh = state_gla.shape[2]
    WK = cache_cmp_k.shape[3] * cache_cmp_k.shape[4]
    P = {
        'norm_g': norm_g, 'ffn_w_in': ffn_w_in.astype(BF16), 'ffn_w_out': ffn_w_out.astype(BF16),
        'even_w_in': even_w_in.astype(BF16), 'gla_w_a2': gla_w_a2.astype(BF16), 'gla_b_a2': gla_b_a2,
        'gla_onorm_g': gla_onorm_g, 'even_w_out': even_w_out.astype(BF16), 'odd_w_in': odd_w_in.astype(BF16),
        'nsa_qk_g': nsa_qk_g, 'cmp_w1': cmp_w1, 'cmp_b1': cmp_b1, 'cmp_w2': cmp_w2, 'cmp_b2': cmp_b2,
        'rel_table': rel_table, 'odd_w_out': odd_w_out.astype(BF16),
        'sb_width': n_sb, 'gla_hk': HK, 'gla_hv': n_gh * dv, 'nsa_wq': odd_w_out.shape[1], 'nsa_wk': WK,
    }

    def pool(c):
        return c.transpose(0, 1, 3, 4, 2).reshape(c.shape[0], c.shape[1], c.shape[3] * c.shape[4], c.shape[2])

    def rows(c):
        return c.reshape(c.shape[0], c.shape[1], c.shape[2], c.shape[3] * c.shape[4])

    past = {
        'page_table': page_table, 'sb_k': pool(cache_sb_k), 'sb_v': pool(cache_sb_v), 'gla': state_gla,
        'cmp_k': pool(cache_cmp_k), 'cmp_v': pool(cache_cmp_v), 'slc_k': pool(cache_slc_k),
        'slc_v': pool(cache_slc_v), 'win_k': pool(cache_win_k), 'win_v': pool(cache_win_v),
        'win_k_rows': rows(cache_win_k), 'win_v_rows': rows(cache_win_v),
    }
    y_p, p_even, p_odd = _trunk(x_prompt, mods[:, :Bp], None, P)
    y_s, s_even, s_odd = _trunk(x_sample, mods[:, Bp:], past, P)
    return (y_p, y_s, *p_even, *p_odd, *s_even, *s_odd)
```
